```python
import jax, jax.numpy as jnp
from jax import lax
import numpy as np

D_MODEL = 1024
BATCH = 32
SEQ = 2048
DEPTH = 4

ROPE_BASE = 10000.0
NORM_EPS = 1e-6
MLA_HEADS = D_MODEL // 128
MLA_QK_NOPE = 128
MLA_QK_ROPE = 64
MLA_V_DIM = 128
MLA_Q_LORA = 3 * D_MODEL // 8
MLA_KV_LORA = D_MODEL // 8
ATTN_Q_BLOCK = 128
POOL_WINDOWS = (2, 4, 8, 16)
POOL_GROUPS = len(POOL_WINDOWS)
POOL_GROUP_DIM = D_MODEL // POOL_GROUPS
MIX_A = MLA_HEADS * MLA_V_DIM
MIX_B = POOL_GROUPS * POOL_GROUP_DIM
EVEN_MIX = MIX_A + MIX_B
EVEN_SPLITS = (MLA_Q_LORA, MLA_Q_LORA + MLA_KV_LORA, MLA_Q_LORA + MLA_KV_LORA + MLA_QK_ROPE,
               MLA_Q_LORA + MLA_KV_LORA + MLA_QK_ROPE + MIX_B)
EVEN_IN = EVEN_SPLITS[-1] + EVEN_MIX
RET_HEADS = D_MODEL // 256
RET_DK = 256
RET_DV = 512
RET_CHUNK = 128
RET_QK = RET_HEADS * RET_DK
RET_V = RET_HEADS * RET_DV
ODD_IN = 2 * RET_QK + 2 * RET_V
N_EVEN = (DEPTH + 1) // 2
N_ODD = DEPTH // 2

kernel_name = 'hybrid_mla_pool_retention_encoder'


def rms_norm(x, g):
    xf = x.astype(jnp.float32)
    y = xf * lax.rsqrt(jnp.mean(xf * xf, axis=-1, keepdims=True) + NORM_EPS)
    return (y * g.astype(jnp.float32)).astype(x.dtype)


def rope_tables(positions, dim):
    inv = 1.0 / (ROPE_BASE ** (jnp.arange(0, dim, 2, dtype=jnp.float32) / dim))
    ang = positions.astype(jnp.float32)[..., None] * inv
    return jnp.cos(ang)[:, :, None, :], jnp.sin(ang)[:, :, None, :]


def apply_rope(x, cos, sin):
    half = x.shape[-1] // 2
    xf = x.astype(jnp.float32)
    x1, x2 = xf[..., :half], xf[..., half:]
    return jnp.concatenate([x1 * cos - x2 * sin, x2 * cos + x1 * sin], axis=-1).astype(x.dtype)


def mla(cq, ckv, kr, cos, sin, q_norm_g, w_uq, kv_norm_g, w_ukv):
    B_, T, _ = cq.shape
    dqk = MLA_QK_NOPE + MLA_QK_ROPE
    q = (rms_norm(cq, q_norm_g) @ w_uq).reshape(B_, T, MLA_HEADS, dqk)
    q = jnp.concatenate([q[..., :MLA_QK_NOPE], apply_rope(q[..., MLA_QK_NOPE:], cos, sin)], axis=-1)
    kv = (rms_norm(ckv, kv_norm_g) @ w_ukv).reshape(B_, T, MLA_HEADS, MLA_QK_NOPE + MLA_V_DIM)
    k_rope = apply_rope(kr[:, :, None, :], cos, sin)
    k = jnp.concatenate([kv[..., :MLA_QK_NOPE],
                         jnp.broadcast_to(k_rope, (B_, T, MLA_HEADS, MLA_QK_ROPE))], axis=-1)
    v = kv[..., MLA_QK_NOPE:]
    nb = T // ATTN_Q_BLOCK
    qb = q.reshape(B_, nb, ATTN_Q_BLOCK, MLA_HEADS, dqk).transpose(1, 0, 2, 3, 4)
    scale = dqk ** -0.5

    def attend(q_blk):
        s = jnp.einsum('bqhd,bkhd->bhqk', q_blk, k, preferred_element_type=jnp.float32) * scale
        p = jax.nn.softmax(s, axis=-1).astype(v.dtype)
        return jnp.einsum('bhqk,bkhd->bqhd', p, v)

    o = lax.map(attend, qb)
    return o.transpose(1, 0, 2, 3, 4).reshape(B_, T, MIX_A)


def multiscale_pool(u, pool_w, pool_scale):
    B_, T, _ = u.shape
    uf = u.reshape(B_, T, POOL_GROUPS, POOL_GROUP_DIM).astype(jnp.float32)
    cs = jnp.concatenate([jnp.zeros_like(uf[:, :1]), jnp.cumsum(uf, axis=1)], axis=1)
    t = jnp.arange(T)
    outs = []
    for g, w in enumerate(POOL_WINDOWS):
        left = w // 2
        right = w - 1 - left
        lo = jnp.clip(t - left, 0, T - 1)
        hi = jnp.clip(t + right, 0, T - 1)
        csg = cs[:, :, g]
        s = jnp.take(csg, hi + 1, axis=1) - jnp.take(csg, lo, axis=1)
        cnt = (hi - lo + 1).astype(jnp.float32)
        outs.append(s / cnt[None, :, None] - uf[:, :, g])
    d = jnp.stack(outs, axis=2).astype(u.dtype)
    y = jnp.einsum('btgc,gcd->btgd', d, pool_w) * pool_scale.reshape(POOL_GROUPS, POOL_GROUP_DIM)
    return y.reshape(B_, T, MIX_B)


def hybrid_attn_pool_layer(x, cos_a, sin_a, norm_g, w_in, q_norm_g, w_uq, kv_norm_g, w_ukv,
                           pool_w, pool_scale, w_out):
    h = rms_norm(x, norm_g)
    proj = h @ w_in
    cq, ckv, kr, u, gate = jnp.split(proj, list(EVEN_SPLITS), axis=-1)
    a = mla(cq, ckv, kr, cos_a, sin_a, q_norm_g, w_uq, kv_norm_g, w_ukv)
    b = multiscale_pool(u, pool_w, pool_scale)
    y = jnp.concatenate([a, b], axis=-1) * jax.nn.silu(gate)
    return y @ w_out


def retention_chunkwise(q, k, v, log_g, include_diag):
    B_, T, H, dk = q.shape
    dv = v.shape[-1]
    C = RET_CHUNK
    N = T // C
    to_chunks = lambda a: a.reshape(B_, N, C, H, a.shape[-1]).transpose(1, 0, 3, 2, 4)
    qc, kc, vc = to_chunks(q), to_chunks(k), to_chunks(v)
    idx = jnp.arange(C).astype(jnp.float32)
    diff = idx[:, None] - idx[None, :]
    mask = (diff >= 0) if include_diag else (diff > 0)
    dmat = jnp.where(mask[None], jnp.exp(jnp.where(mask, diff, 0.0)[None] * log_g[:, None, None]), 0.0)
    xi = jnp.exp((idx + 1.0)[None, :] * log_g[:, None])[None, :, :, None]
    zeta = jnp.exp((C - 1.0 - idx)[None, :] * log_g[:, None])[None, :, :, None]
    chunk_decay = jnp.exp(C * log_g)[None, :, None, None]

    def step(state, inp):
        q_i, k_i, v_i = inp
        scores = jnp.einsum('bhid,bhjd->bhij', q_i, k_i) * dmat
        o_in = jnp.einsum('bhij,bhjv->bhiv', scores, v_i)
        o_cross = jnp.einsum('bhid,bhdv->bhiv', q_i, state) * xi
        state = state * chunk_decay + jnp.einsum('bhjd,bhjv->bhdv', k_i * zeta, v_i)
        return state, o_in + o_cross

    s0 = jnp.zeros((B_, H, dk, dv), jnp.float32)
    _, o = lax.scan(step, s0, (qc, kc, vc))
    return o.transpose(1, 0, 3, 2, 4).reshape(B_, T, H, dv)


def retention_layer(x, cos_r, sin_r, norm_g, w_in, decay_fwd, decay_bwd, gn_g, w_out):
    B_, T, _ = x.shape
    h = rms_norm(x, norm_g)
    proj = h @ w_in
    q, k, v, gate = jnp.split(proj, [RET_QK, 2 * RET_QK, 2 * RET_QK + RET_V], axis=-1)
    q = apply_rope(q.reshape(B_, T, RET_HEADS, RET_DK), cos_r, sin_r).astype(jnp.float32)
    k = apply_rope(k.reshape(B_, T, RET_HEADS, RET_DK), cos_r, sin_r).astype(jnp.float32) * (RET_DK ** -0.5)
    v = v.reshape(B_, T, RET_HEADS, RET_DV).astype(jnp.float32)
    lf = jax.nn.log_sigmoid(decay_fwd.astype(jnp.float32))
    lb = jax.nn.log_sigmoid(decay_bwd.astype(jnp.float32))
    o_f = retention_chunkwise(q, k, v, lf, True)
    o_b = jnp.flip(retention_chunkwise(jnp.flip(q, 1), jnp.flip(k, 1), jnp.flip(v, 1), lb, False), 1)
    o = o_f + o_b
    mu = jnp.mean(o, axis=-1, keepdims=True)
    var = jnp.mean(jnp.square(o - mu), axis=-1, keepdims=True)
    o = ((o - mu) * lax.rsqrt(var + NORM_EPS)).reshape(B_, T, RET_V) * gn_g.astype(jnp.float32)
    y = o.astype(x.dtype) * jax.nn.silu(gate)
    return y @ w_out


def setup_inputs(seed: int = 0) -> dict:
    key = jax.random.key(seed)
    ks = jax.random.split(key, 24)
    f32 = jnp.float32
    nrm = lambda k, shape, scale: jax.random.normal(k, shape, f32) * scale
    gain = lambda k, shape: 1.0 + 0.05 * jax.random.normal(k, shape, f32)
    x = jax.random.normal(ks[0], (BATCH, SEQ, D_MODEL), f32)
    positions = (jnp.arange(SEQ, dtype=jnp.int32)[None, :]
                 + jax.random.randint(ks[1], (BATCH, 1), 0, 4096, dtype=jnp.int32))
    base_decay = jnp.log(2.0 ** (5.0 + jnp.arange(RET_HEADS, dtype=f32)) - 1.0)
    return {
        'x': x,
        'positions': positions,
        'a_norm_g': gain(ks[2], (N_EVEN, D_MODEL)),
        'a_w_in': nrm(ks[3], (N_EVEN, D_MODEL, EVEN_IN), D_MODEL ** -0.5),
        'a_q_norm_g': gain(ks[4], (N_EVEN, MLA_Q_LORA)),
        'a_w_uq': nrm(ks[5], (N_EVEN, MLA_Q_LORA, MLA_HEADS * (MLA_QK_NOPE + MLA_QK_ROPE)), MLA_Q_LORA ** -0.5),
        'a_kv_norm_g': gain(ks[6], (N_EVEN, MLA_KV_LORA)),
        'a_w_ukv': nrm(ks[7], (N_EVEN, MLA_KV_LORA, MLA_HEADS * (MLA_QK_NOPE + MLA_V_DIM)), MLA_KV_LORA ** -0.5),
        'a_pool_w': nrm(ks[8], (N_EVEN, POOL_GROUPS, POOL_GROUP_DIM, POOL_GROUP_DIM), POOL_GROUP_DIM ** -0.5),
        'a_pool_scale': 1.0 + 0.1 * jax.random.normal(ks[9], (N_EVEN, MIX_B), f32),
        'a_w_out': nrm(ks[10], (N_EVEN, EVEN_MIX, D_MODEL), EVEN_MIX ** -0.5),
        'r_norm_g': gain(ks[11], (N_ODD, D_MODEL)),
        'r_w_in': nrm(ks[12], (N_ODD, D_MODEL, ODD_IN), D_MODEL ** -0.5),
        'r_decay_fwd': base_decay[None, :] + 0.1 * jax.random.normal(ks[13], (N_ODD, RET_HEADS), f32),
        'r_decay_bwd': base_decay[None, :] + 0.1 * jax.random.normal(ks[14], (N_ODD, RET_HEADS), f32),
        'r_gn_g': gain(ks[15], (N_ODD, RET_V)),
        'r_w_out': nrm(ks[16], (N_ODD, RET_V, D_MODEL), RET_V ** -0.5),
        'final_norm_g': gain(ks[17], (D_MODEL,)),
    }


def reference(x, positions, a_norm_g, a_w_in, a_q_norm_g, a_w_uq, a_kv_norm_g, a_w_ukv,
              a_pool_w, a_pool_scale, a_w_out, r_norm_g, r_w_in, r_decay_fwd, r_decay_bwd,
              r_gn_g, r_w_out, final_norm_g):
    cos_a, sin_a = rope_tables(positions, MLA_QK_ROPE)
    cos_r, sin_r = rope_tables(positions, RET_DK)
    for layer in range(DEPTH):
        i = layer // 2
        if layer % 2 == 0:
            x = x + hybrid_attn_pool_layer(x, cos_a, sin_a, a_norm_g[i], a_w_in[i], a_q_norm_g[i],
                                           a_w_uq[i], a_kv_norm_g[i], a_w_ukv[i], a_pool_w[i],
                                           a_pool_scale[i], a_w_out[i])
        else:
            x = x + retention_layer(x, cos_r, sin_r, r_norm_g[i], r_w_in[i], r_decay_fwd[i],
                                    r_decay_bwd[i], r_gn_g[i], r_w_out[i])
    return rms_norm(x, final_norm_g)
```

```python
import functools

import jax
import jax.numpy as jnp
from jax import lax
from jax.experimental import pallas as pl
from jax.experimental.pallas import tpu as pltpu

ROPE_BASE = 10000.0
NORM_EPS = 1e-6
LOG2E = 1.4426950408889634

D_MODEL = 1024
MLA_HEADS = 8
MLA_NOPE = 128
MLA_ROPE = 64
MLA_V = 128
MLA_Q_LORA = 384
MLA_KV_LORA = 128
POOL_WINDOWS = (2, 4, 8, 16)
POOL_DIM = 256
MIX_A = MLA_HEADS * MLA_V
MIX_B = len(POOL_WINDOWS) * POOL_DIM
RET_HEADS = 4
RET_DK = 256
RET_DV = 512
RET_QK = RET_HEADS * RET_DK
RET_V = RET_HEADS * RET_DV

LANES = 128
SUBLANES = 8
HEAD_PAD = 2 * LANES
POOL_HALO = SUBLANES
VMEM_LIMIT = 56 * 1024 * 1024

ROW_TILE = 512
ATTN_Q_TILE = 512
RET_CHUNK = 256

BF16 = jnp.bfloat16
F32 = jnp.float32


def _params(*sem):
    return pltpu.CompilerParams(dimension_semantics=sem, vmem_limit_bytes=VMEM_LIMIT)


def _const_spec(shape):
    nd = len(shape)
    return pl.BlockSpec(shape, lambda *_: (0,) * nd, pipeline_mode=pl.Buffered(1))


def _rms(x, g):
    return x * lax.rsqrt(jnp.mean(x * x, axis=-1, keepdims=True) + NORM_EPS) * g


def _silu(x):
    return x / (1.0 + jnp.exp(-x))


def _dot(a, b):
    return jnp.dot(a, b, preferred_element_type=F32)


def _dot_nt(a, b):
    return lax.dot_general(a, b, (((1,), (1,)), ((), ())), preferred_element_type=F32)


def _dot_tn(a, b):
    return lax.dot_general(a, b, (((0,), (0,)), ((), ())), preferred_element_type=F32)


def _rope_tab_kernel(pos_ref, inv_a_ref, msk_ref, sgn_ref, inv_r_ref, ca_ref, sa_ref, cr_ref, sr_ref):
    pos = pos_ref[0].astype(F32)
    ang_a = pos * inv_a_ref[...]
    ca_ref[0] = jnp.cos(ang_a) * msk_ref[...]
    sa_ref[0] = jnp.sin(ang_a) * sgn_ref[...]
    ang_r = pos * inv_r_ref[...]
    cr_ref[0] = jnp.cos(ang_r)
    sr_ref[0] = jnp.sin(ang_r)


def _rope_tables(positions):
    B, T = positions.shape
    half_a = MLA_ROPE // 2
    inv_a = 1.0 / (ROPE_BASE ** (jnp.arange(0, MLA_ROPE, 2, dtype=F32) / MLA_ROPE))
    inv_r = 1.0 / (ROPE_BASE ** (jnp.arange(0, RET_DK, 2, dtype=F32) / RET_DK))
    zeros = jnp.zeros((LANES - 2 * half_a,), F32)
    ones = jnp.ones((half_a,), F32)
    inv_a_p = jnp.concatenate([inv_a, inv_a, zeros])[None]
    msk = jnp.concatenate([ones, ones, zeros])[None]
    sgn = jnp.concatenate([-ones, ones, zeros])[None]
    tt = min(T, 1024)
    tab = jax.ShapeDtypeStruct((B, T, LANES), F32)
    blk = pl.BlockSpec((1, tt, LANES), lambda b, t: (b, t, 0))
    return pl.pallas_call(
        _rope_tab_kernel,
        grid=(B, T // tt),
        in_specs=[pl.BlockSpec((1, tt, 1), lambda b, t: (b, t, 0)),
                  _const_spec((1, LANES)), _const_spec((1, LANES)), _const_spec((1, LANES)),
                  _const_spec((1, LANES))],
        out_specs=[blk, blk, blk, blk],
        out_shape=[tab, tab, tab, tab],
        compiler_params=_params("parallel", "parallel"),
        name="rope_tables",
    )(positions.reshape(B, T, 1), inv_a_p, msk, sgn, inv_r[None])


def _rope_pad(v, c, s):
    q = LANES // 4
    return v * c + (pltpu.roll(v, 3 * q, 1) + pltpu.roll(v, q, 1)) * s


def _even_front_kernel(x_ref, g_ref, wlat_ref, gq_ref, wuq_ref, gkv_ref, wukv_ref, ca_ref, sa_ref,
                       q_ref, k_ref, v_ref):
    h = _rms(x_ref[0], g_ref[...]).astype(BF16)
    lat = _dot(h, wlat_ref[...])
    ca = ca_ref[0]
    sa = sa_ref[0]
    cq = _rms(lat[:, :MLA_Q_LORA], gq_ref[...]).astype(BF16)
    q = _dot(cq, wuq_ref[...])
    q_scale = (MLA_NOPE + MLA_ROPE) ** -0.5 * LOG2E
    ckv = _rms(lat[:, MLA_Q_LORA:MLA_Q_LORA + MLA_KV_LORA], gkv_ref[...]).astype(BF16)
    kv = _dot(ckv, wukv_ref[...])
    kr = _rope_pad(lat[:, MLA_Q_LORA + MLA_KV_LORA:], ca, sa).astype(BF16)
    for hd in range(MLA_HEADS):
        lo = hd * HEAD_PAD
        q_ref[0, :, lo:lo + LANES] = (q[:, lo:lo + LANES] * q_scale).astype(BF16)
        q_ref[0, :, lo + LANES:lo + HEAD_PAD] = (
            _rope_pad(q[:, lo + LANES:lo + HEAD_PAD], ca, sa) * q_scale).astype(BF16)
        k_ref[0, :, lo:lo + LANES] = kv[:, hd * MLA_NOPE:(hd + 1) * MLA_NOPE].astype(BF16)
        k_ref[0, :, lo + LANES:lo + HEAD_PAD] = kr
    v_ref[0] = kv[:, MIX_A:].astype(BF16)


def _even_front(x, g, wlat, gq, wuq, gkv, wukv, ca, sa):
    B, T, D = x.shape
    tm = min(T, ROW_TILE)
    row = lambda w: pl.BlockSpec((1, tm, w), lambda b, t: (b, t, 0))
    qk = jax.ShapeDtypeStruct((B, T, MLA_HEADS * HEAD_PAD), BF16)
    return pl.pallas_call(
        _even_front_kernel,
        grid=(B, T // tm),
        in_specs=[row(D), _const_spec(g.shape), _const_spec(wlat.shape), _const_spec(gq.shape),
                  _const_spec(wuq.shape), _const_spec(gkv.shape), _const_spec(wukv.shape),
                  row(LANES), row(LANES)],
        out_specs=[row(MLA_HEADS * HEAD_PAD), row(MLA_HEADS * HEAD_PAD), row(MIX_A)],
        out_shape=[qk, qk, jax.ShapeDtypeStruct((B, T, MIX_A), BF16)],
        compiler_params=_params("parallel", "parallel"),
        name="even_front",
    )(x, g, wlat, gq, wuq, gkv, wukv, ca, sa)


def _attn_kernel(q_ref, k_ref, v_ref, o_ref):
    s = _dot_nt(q_ref[0], k_ref[0])
    m = jnp.max(s, axis=-1, keepdims=True)
    p = jnp.exp2(s - m)
    l = jnp.sum(p, axis=-1, keepdims=True)
    o = _dot(p.astype(BF16), v_ref[0])
    o_ref[0] = (o / l).astype(BF16)


def _attention(q, k, v):
    B, T, _ = q.shape
    tq = min(T, ATTN_Q_TILE)
    return pl.pallas_call(
        _attn_kernel,
        grid=(B, MLA_HEADS, T // tq),
        in_specs=[pl.BlockSpec((1, tq, HEAD_PAD), lambda b, h, i: (b, i, h)),
                  pl.BlockSpec((1, T, HEAD_PAD), lambda b, h, i: (b, 0, h)),
                  pl.BlockSpec((1, T, MLA_V), lambda b, h, i: (b, 0, h))],
        out_specs=pl.BlockSpec((1, tq, MLA_V), lambda b, h, i: (b, i, h)),
        out_shape=jax.ShapeDtypeStruct((B, T, MIX_A), BF16),
        compiler_params=_params("parallel", "parallel", "parallel"),
        name="mla_attention",
    )(q, k, v)


def _even_back_kernel(x_ref, xp_ref, xn_ref, g_ref, wu_ref, wg_ref, a_ref, pw_ref, ps_ref, wo_ref,
                      o_ref, *, seq_len):
    tm = x_ref.shape[1]
    t0 = pl.program_id(1) * tm
    x = x_ref[0]
    g = g_ref[...]
    h = _rms(x, g).astype(BF16)
    hp = _rms(xp_ref[0], g).astype(BF16)
    hn = _rms(xn_ref[0], g).astype(BF16)
    wu = wu_ref[...]
    up = jnp.where(t0 > 0, _dot(hp, wu), 0.0)
    un = jnp.where(t0 + tm < seq_len, _dot(hn, wu), 0.0)
    u = _dot(h, wu)
    ue = jnp.concatenate([up, u, un], axis=0)
    t = (t0 + lax.broadcasted_iota(jnp.int32, (tm, 1), 0))
    gate = _dot(h, wg_ref[...])
    sg = _silu(gate)
    ya = (a_ref[0].astype(F32) * sg[:, :MIX_A]).astype(BF16)
    y = _dot(ya, wo_ref[:MIX_A, :])
    for gi, w in enumerate(POOL_WINDOWS):
        left = w // 2
        right = w - 1 - left
        cols = slice(gi * POOL_DIM, (gi + 1) * POOL_DIM)
        acc = ue[POOL_HALO - left:POOL_HALO - left + tm, cols]
        for j in range(-left + 1, right + 1):
            acc = acc + ue[POOL_HALO + j:POOL_HALO + j + tm, cols]
        cnt = (jnp.minimum(t + right, seq_len - 1) - jnp.maximum(t - left, 0) + 1).astype(F32)
        d = (acc / cnt - u[:, cols]).astype(BF16)
        bg = _dot(d, pw_ref[gi]) * ps_ref[:, cols]
        yb = (bg * sg[:, MIX_A + gi * POOL_DIM:MIX_A + (gi + 1) * POOL_DIM]).astype(BF16)
        y = y + _dot(yb, wo_ref[MIX_A + gi * POOL_DIM:MIX_A + (gi + 1) * POOL_DIM, :])
    o_ref[0] = x + y


def _even_back(x, g, wu, wg, a, pw, ps, wo):
    B, T, D = x.shape
    tm = min(T, ROW_TILE)
    nb = tm // POOL_HALO
    last = T // POOL_HALO - 1
    row = lambda w: pl.BlockSpec((1, tm, w), lambda b, t: (b, t, 0))
    return pl.pallas_call(
        functools.partial(_even_back_kernel, seq_len=T),
        grid=(B, T // tm),
        in_specs=[row(D),
                  pl.BlockSpec((1, POOL_HALO, D), lambda b, t: (b, jnp.maximum(t * nb - 1, 0), 0)),
                  pl.BlockSpec((1, POOL_HALO, D), lambda b, t: (b, jnp.minimum((t + 1) * nb, last), 0)),
                  _const_spec(g.shape), _const_spec(wu.shape), _const_spec(wg.shape), row(MIX_A),
                  _const_spec(pw.shape), _const_spec(ps.shape), _const_spec(wo.shape)],
        out_specs=row(D),
        out_shape=jax.ShapeDtypeStruct((B, T, D), F32),
        compiler_params=_params("parallel", "parallel"),
        name="even_back",
    )(x, x, x, g, wu, wg, a, pw, ps, wo)


def _odd_front_kernel(x_ref, g_ref, w_ref, c_ref, s_ref, q_ref, k_ref, v_ref):
    h = _rms(x_ref[0], g_ref[...]).astype(BF16)
    qkv = _dot(h, w_ref[...])
    c = c_ref[0]
    s = s_ref[0]
    half = RET_DK // 2
    k_scale = RET_DK ** -0.5
    for hd in range(RET_HEADS):
        for base, ref, scale in ((0, q_ref, None), (RET_QK, k_ref, k_scale)):
            lo = base + hd * RET_DK
            x1 = qkv[:, lo:lo + half]
            x2 = qkv[:, lo + half:lo + RET_DK]
            o1 = x1 * c - x2 * s
            o2 = x2 * c + x1 * s
            if scale is not None:
                o1 = o1 * scale
                o2 = o2 * scale
            ref[0, :, hd * RET_DK:hd * RET_DK + half] = o1.astype(BF16)
            ref[0, :, hd * RET_DK + half:(hd + 1) * RET_DK] = o2.astype(BF16)
    v_ref[0] = qkv[:, 2 * RET_QK:].astype(BF16)


def _odd_front(x, g, w, c, s):
    B, T, D = x.shape
    tm = min(T, ROW_TILE)
    row = lambda wd: pl.BlockSpec((1, tm, wd), lambda b, t: (b, t, 0))
    qk = jax.ShapeDtypeStruct((B, T, RET_QK), BF16)
    return pl.pallas_call(
        _odd_front_kernel,
        grid=(B, T // tm),
        in_specs=[row(D), _const_spec(g.shape), _const_spec(w.shape), row(LANES), row(LANES)],
        out_specs=[row(RET_QK), row(RET_QK), row(RET_V)],
        out_shape=[qk, qk, jax.ShapeDtypeStruct((B, T, RET_V), BF16)],
        compiler_params=_params("parallel", "parallel"),
        name="odd_front",
    )(x, g, w, c, s)


def _log_sigmoid(x):
    return jnp.minimum(x, 0.0) - jnp.log1p(jnp.exp(-jnp.abs(x)))


def _retention_kernel(q_ref, k_ref, v_ref, df_ref, db_ref, gn_ref, o_ref, sf_ref, acc_ref):
    T = q_ref.shape[1]
    C = min(T, RET_CHUNK)
    n_chunks = T // C
    lf = _log_sigmoid(df_ref[0])
    lb = _log_sigmoid(db_ref[0])
    lf1 = lf[:, :1]
    lb1 = lb[:, :1]
    ri = lax.broadcasted_iota(jnp.int32, (C, 1), 0).astype(F32)
    diff = (lax.broadcasted_iota(jnp.int32, (C, C), 0)
            - lax.broadcasted_iota(jnp.int32, (C, C), 1)).astype(F32)
    dmat = jnp.exp(jnp.where(diff >= 0, diff * lf1, -diff * lb1))
    xi_f = jnp.exp((ri + 1.0) * lf1)
    xi_b = jnp.exp((C - ri) * lb1)
    zeta_f = jnp.exp((C - 1.0 - ri) * lf1)
    zeta_b = jnp.exp(ri * lb1)
    cd_f = jnp.exp(C * lf1)
    cd_b = jnp.exp(C * lb1)

    acc_ref[...] = jnp.zeros_like(acc_ref)

    def fwd(i, carry):
        rows = pl.ds(pl.multiple_of(i * C, C), C)
        sf_ref[i] = acc_ref[...].astype(BF16)
        kz = (k_ref[0, rows, :].astype(F32) * zeta_f).astype(BF16)
        acc_ref[...] = acc_ref[...] * cd_f + _dot_tn(kz, v_ref[0, rows, :])
        return carry

    lax.fori_loop(0, n_chunks, fwd, 0)
    acc_ref[...] = jnp.zeros_like(acc_ref)

    def bwd(j, carry):
        i = n_chunks - 1 - j
        rows = pl.ds(pl.multiple_of(i * C, C), C)
        q = q_ref[0, rows, :]
        k = k_ref[0, rows, :]
        v = v_ref[0, rows, :]
        s = (_dot_nt(q, k) * dmat).astype(BF16)
        o = _dot(s, v)
        o = o + _dot(q, sf_ref[i]) * xi_f
        o = o + _dot(q, acc_ref[...].astype(BF16)) * xi_b
        mu = jnp.mean(o, axis=-1, keepdims=True)
        oc = o - mu
        var = jnp.mean(oc * oc, axis=-1, keepdims=True)
        o_ref[0, rows, :] = (oc * lax.rsqrt(var + NORM_EPS) * gn_ref[...]).astype(BF16)
        kz = (k.astype(F32) * zeta_b).astype(BF16)
        acc_ref[...] = acc_ref[...] * cd_b + _dot_tn(kz, v)
        return carry

    lax.fori_loop(0, n_chunks, bwd, 0)


def _retention(q, k, v, dec_f, dec_b, gn_g):
    B, T, _ = q.shape
    C = min(T, RET_CHUNK)
    head = lambda w: pl.BlockSpec((1, T, w), lambda b, h: (b, 0, h))
    dec = pl.BlockSpec((1, 1, LANES), lambda b, h: (h, 0, 0))
    return pl.pallas_call(
        _retention_kernel,
        grid=(B, RET_HEADS),
        in_specs=[head(RET_DK), head(RET_DK), head(RET_DV), dec, dec,
                  pl.BlockSpec((1, RET_DV), lambda b, h: (0, h))],
        out_specs=head(RET_DV),
        out_shape=jax.ShapeDtypeStruct((B, T, RET_V), BF16),
        scratch_shapes=[pltpu.VMEM((T // C, RET_DK, RET_DV), BF16),
                        pltpu.VMEM((RET_DK, RET_DV), F32)],
        compiler_params=_params("parallel", "parallel"),
        name="retention",
    )(q, k, v, dec_f, dec_b, gn_g)


def _odd_back_kernel(x_ref, g_ref, wg_ref, a_ref, wo_ref, fg_ref, o_ref, *, final_norm):
    x = x_ref[0]
    h = _rms(x, g_ref[...]).astype(BF16)
    gate = _dot(h, wg_ref[...])
    y = (a_ref[0].astype(F32) * _silu(gate)).astype(BF16)
    out = x + _dot(y, wo_ref[...])
    if final_norm:
        out = _rms(out, fg_ref[...])
    o_ref[0] = out


def _odd_back(x, g, wg, a, wo, fg, final_norm):
    B, T, D = x.shape
    tm = min(T, ROW_TILE)
    row = lambda w: pl.BlockSpec((1, tm, w), lambda b, t: (b, t, 0))
    return pl.pallas_call(
        functools.partial(_odd_back_kernel, final_norm=final_norm),
        grid=(B, T // tm),
        in_specs=[row(D), _const_spec(g.shape), _const_spec(wg.shape), row(RET_V),
                  _const_spec(wo.shape), _const_spec(fg.shape)],
        out_specs=row(D),
        out_shape=jax.ShapeDtypeStruct((B, T, D), F32),
        compiler_params=_params("parallel", "parallel"),
        name="odd_back",
    )(x, g, wg, a, wo, fg)


def _even_weights(w_in, w_uq, w_ukv):
    n_lat = MLA_Q_LORA + MLA_KV_LORA
    wlat = jnp.concatenate(
        [w_in[:, :n_lat + MLA_ROPE], jnp.zeros((D_MODEL, LANES - MLA_ROPE), w_in.dtype)], axis=1)
    wu = w_in[:, n_lat + MLA_ROPE:n_lat + MLA_ROPE + MIX_B]
    wg = w_in[:, n_lat + MLA_ROPE + MIX_B:]
    uq = w_uq.reshape(MLA_Q_LORA, MLA_HEADS, MLA_NOPE + MLA_ROPE)
    uq = jnp.pad(uq, ((0, 0), (0, 0), (0, HEAD_PAD - MLA_NOPE - MLA_ROPE)))
    uq = uq.reshape(MLA_Q_LORA, MLA_HEADS * HEAD_PAD)
    ukv = w_ukv.reshape(MLA_KV_LORA, MLA_HEADS, MLA_NOPE + MLA_V)
    ukv = jnp.concatenate([ukv[:, :, :MLA_NOPE].reshape(MLA_KV_LORA, MIX_A),
                           ukv[:, :, MLA_NOPE:].reshape(MLA_KV_LORA, MIX_A)], axis=1)
    return wlat.astype(BF16), wu.astype(BF16), wg.astype(BF16), uq.astype(BF16), ukv.astype(BF16)


def kernel(x, positions, a_norm_g, a_w_in, a_q_norm_g, a_w_uq, a_kv_norm_g, a_w_ukv, a_pool_w,
           a_pool_scale, a_w_out, r_norm_g, r_w_in, r_decay_fwd, r_decay_bwd, r_gn_g, r_w_out,
           final_norm_g):
    depth = a_norm_g.shape[0] + r_norm_g.shape[0]
    ca, sa, cr, sr = _rope_tables(positions)
    fg = final_norm_g[None]
    for layer in range(depth):
        i = layer // 2
        if layer % 2 == 0:
            wlat, wu, wg, wuq, wukv = _even_weights(a_w_in[i], a_w_uq[i], a_w_ukv[i])
            g = a_norm_g[i][None]
            q, k, v = _even_front(x, g, wlat, a_q_norm_g[i][None], wuq, a_kv_norm_g[i][None], wukv,
                                  ca, sa)
            a = _attention(q, k, v)
            x = _even_back(x, g, wu, wg, a, a_pool_w[i].astype(BF16), a_pool_scale[i][None],
                           a_w_out[i].astype(BF16))
        else:
            w = r_w_in[i]
            g = r_norm_g[i][None]
            q, k, v = _odd_front(x, g, w[:, :2 * RET_QK + RET_V].astype(BF16), cr, sr)
            dec_f = jnp.broadcast_to(r_decay_fwd[i][:, None, None], (RET_HEADS, 1, LANES))
            dec_b = jnp.broadcast_to(r_decay_bwd[i][:, None, None], (RET_HEADS, 1, LANES))
            o = _retention(q, k, v, dec_f, dec_b, r_gn_g[i][None])
            x = _odd_back(x, g, w[:, 2 * RET_QK + RET_V:].astype(BF16), o, r_w_out[i].astype(BF16),
                          fg, final_norm=(layer == depth - 1))
    if depth % 2 == 1:
        raise NotImplementedError("final norm is fused into the last odd layer")
    return x
```

```python
import functools

import jax
import jax.numpy as jnp
from jax import lax
from jax.experimental import pallas as pl
from jax.experimental.pallas import tpu as pltpu

ROPE_BASE = 10000.0
NORM_EPS = 1e-6
LOG2E = 1.4426950408889634

D_MODEL = 1024
MLA_HEADS = 8
MLA_NOPE = 128
MLA_ROPE = 64
MLA_V = 128
MLA_Q_LORA = 384
MLA_KV_LORA = 128
POOL_WINDOWS = (2, 4, 8, 16)
POOL_DIM = 256
MIX_A = MLA_HEADS * MLA_V
MIX_B = len(POOL_WINDOWS) * POOL_DIM
RET_HEADS = 4
RET_DK = 256
RET_DV = 512
RET_QK = RET_HEADS * RET_DK
RET_V = RET_HEADS * RET_DV

LANES = 128
SUBLANES = 8
HEAD_PAD = 2 * LANES
POOL_HALO = SUBLANES
VMEM_LIMIT = 56 * 1024 * 1024

ROW_TILE = 512
ATTN_KV_CHUNK = 256
ATTN_HEADS_PER_STEP = 2
RET_CHUNK = 256

BF16 = jnp.bfloat16
F32 = jnp.float32


def _params(*sem):
    return pltpu.CompilerParams(dimension_semantics=sem, vmem_limit_bytes=VMEM_LIMIT)


def _const_spec(shape):
    nd = len(shape)
    return pl.BlockSpec(shape, lambda *_: (0,) * nd, pipeline_mode=pl.Buffered(1))


def _rms(x, g):
    return x * lax.rsqrt(jnp.mean(x * x, axis=-1, keepdims=True) + NORM_EPS) * g


def _silu(x):
    return x / (1.0 + jnp.exp(-x))


def _dot(a, b):
    return jnp.dot(a, b, preferred_element_type=F32)


def _dot_nt(a, b):
    return lax.dot_general(a, b, (((1,), (1,)), ((), ())), preferred_element_type=F32)


def _dot_tn(a, b):
    return lax.dot_general(a, b, (((0,), (0,)), ((), ())), preferred_element_type=F32)


def _rope_tab_kernel(pos_ref, inv_a_ref, msk_ref, sgn_ref, inv_r_ref,
                     ca_ref, sa_ref, cat_ref, sat_ref, cr_ref, sr_ref):
    half_a = MLA_ROPE // 2
    pos = pos_ref[0].astype(F32)
    ang_a = pos * inv_a_ref[...]
    cos_a = jnp.cos(ang_a)
    sin_a = jnp.sin(ang_a)
    ca_ref[0] = cos_a * msk_ref[...]
    sa_ref[0] = sin_a * sgn_ref[...]
    cat_ref[0] = cos_a.T[:half_a]
    sat_ref[0] = sin_a.T[:half_a]
    ang_r = pos * inv_r_ref[...]
    cr_ref[0] = jnp.cos(ang_r)
    sr_ref[0] = jnp.sin(ang_r)


def _rope_tables(positions):
    B, T = positions.shape
    half_a = MLA_ROPE // 2
    inv_a = 1.0 / (ROPE_BASE ** (jnp.arange(0, MLA_ROPE, 2, dtype=F32) / MLA_ROPE))
    inv_r = 1.0 / (ROPE_BASE ** (jnp.arange(0, RET_DK, 2, dtype=F32) / RET_DK))
    zeros = jnp.zeros((LANES - 2 * half_a,), F32)
    ones = jnp.ones((half_a,), F32)
    inv_a_p = jnp.concatenate([inv_a, inv_a, zeros])[None]
    msk = jnp.concatenate([ones, ones, zeros])[None]
    sgn = jnp.concatenate([-ones, ones, zeros])[None]
    tt = min(T, 1024)
    tab = jax.ShapeDtypeStruct((B, T, LANES), F32)
    tab_t = jax.ShapeDtypeStruct((B, half_a, T), F32)
    blk = pl.BlockSpec((1, tt, LANES), lambda b, t: (b, t, 0))
    blk_t = pl.BlockSpec((1, half_a, tt), lambda b, t: (b, 0, t))
    return pl.pallas_call(
        _rope_tab_kernel,
        grid=(B, T // tt),
        in_specs=[pl.BlockSpec((1, tt, 1), lambda b, t: (b, t, 0)),
                  _const_spec((1, LANES)), _const_spec((1, LANES)), _const_spec((1, LANES)),
                  _const_spec((1, LANES))],
        out_specs=[blk, blk, blk_t, blk_t, blk, blk],
        out_shape=[tab, tab, tab_t, tab_t, tab, tab],
        compiler_params=_params("parallel", "parallel"),
        name="rope_tables",
    )(positions.reshape(B, T, 1), inv_a_p, msk, sgn, inv_r[None])


def _rope_pad(v, c, s):
    q = LANES // 4
    return v * c + (pltpu.roll(v, 3 * q, 1) + pltpu.roll(v, q, 1)) * s


def _even_front_kernel(x_ref, g_ref, wlat_ref, gq_ref, wuqt_ref, gkv_ref, wuk_ref, wuvt_ref,
                       ca_ref, sa_ref, cat_ref, sat_ref, qt_ref, k_ref, vt_ref):
    h = _rms(x_ref[0], g_ref[...]).astype(BF16)
    lat = _dot(h, wlat_ref[...])
    cq = _rms(lat[:, :MLA_Q_LORA], gq_ref[...]).astype(BF16)
    qt = _dot_nt(wuqt_ref[...], cq)
    q_scale = (MLA_NOPE + MLA_ROPE) ** -0.5 * LOG2E
    ckv = _rms(lat[:, MLA_Q_LORA:MLA_Q_LORA + MLA_KV_LORA], gkv_ref[...]).astype(BF16)
    kn = _dot(ckv, wuk_ref[...])
    vt_ref[0] = _dot_nt(wuvt_ref[...], ckv).astype(BF16)
    kr = _rope_pad(lat[:, MLA_Q_LORA + MLA_KV_LORA:], ca_ref[0], sa_ref[0]).astype(BF16)
    ct = cat_ref[0]
    st = sat_ref[0]
    half = MLA_ROPE // 2
    for hd in range(MLA_HEADS):
        lo = hd * HEAD_PAD
        r1 = lo + MLA_NOPE
        r2 = r1 + half
        r3 = r2 + half
        x1 = qt[r1:r2]
        x2 = qt[r2:r3]
        qt_ref[0, 0, lo:r1, :] = (qt[lo:r1] * q_scale).astype(BF16)
        qt_ref[0, 0, r1:r2, :] = ((x1 * ct - x2 * st) * q_scale).astype(BF16)
        qt_ref[0, 0, r2:r3, :] = ((x2 * ct + x1 * st) * q_scale).astype(BF16)
        qt_ref[0, 0, r3:lo + HEAD_PAD, :] = jnp.zeros((lo + HEAD_PAD - r3, qt.shape[1]), BF16)
        k_ref[0, :, lo:lo + LANES] = kn[:, hd * MLA_NOPE:(hd + 1) * MLA_NOPE].astype(BF16)
        k_ref[0, :, lo + LANES:lo + HEAD_PAD] = kr


def _even_front(x, g, wlat, gq, wuqt, gkv, wuk, wuvt, ca, sa, cat, sat):
    B, T, D = x.shape
    tm = min(T, ROW_TILE)
    row = lambda w: pl.BlockSpec((1, tm, w), lambda b, t: (b, t, 0))
    col = lambda r: pl.BlockSpec((1, r, tm), lambda b, t: (b, 0, t))
    consts = [g, wlat, gq, wuqt, gkv, wuk, wuvt]
    return pl.pallas_call(
        _even_front_kernel,
        grid=(B, T // tm),
        in_specs=[row(D)] + [_const_spec(c.shape) for c in consts]
        + [row(LANES), row(LANES), col(MLA_ROPE // 2), col(MLA_ROPE // 2)],
        out_specs=[pl.BlockSpec((1, 1, MLA_HEADS * HEAD_PAD, tm), lambda b, t: (b, t, 0, 0)),
                   row(MLA_HEADS * HEAD_PAD), col(MIX_A)],
        out_shape=[jax.ShapeDtypeStruct((B, T // tm, MLA_HEADS * HEAD_PAD, tm), BF16),
                   jax.ShapeDtypeStruct((B, T, MLA_HEADS * HEAD_PAD), BF16),
                   jax.ShapeDtypeStruct((B, MIX_A, T), BF16)],
        compiler_params=_params("parallel", "parallel"),
        name="even_front",
    )(x, *consts, ca, sa, cat, sat)


def _attn_kernel(qt_ref, k_ref, vt_ref, o_ref, sa_ref, sb_ref, m_ref):
    n_q, tq = qt_ref.shape[1], qt_ref.shape[3]
    n_kc, kc = sa_ref.shape[0], sa_ref.shape[1]
    n_blk = n_q * (qt_ref.shape[2] // HEAD_PAD)
    grp = (kc // SUBLANES, SUBLANES, tq)
    s_bufs = (sa_ref, sb_ref)

    def score_chunk(b, c, m8):
        hd, qb = divmod(b, n_q)
        qt = qt_ref[0, qb, hd * HEAD_PAD:(hd + 1) * HEAD_PAD, :]
        s = _dot(k_ref[0, c * kc:(c + 1) * kc, hd * HEAD_PAD:(hd + 1) * HEAD_PAD], qt)
        s_bufs[b % 2][c] = s
        cm = jnp.max(s.reshape(grp), axis=0)
        return cm if m8 is None else jnp.maximum(m8, cm)

    def value_chunk(b, c, m, l8, acc):
        hd = b // n_q
        p = jnp.exp2(s_bufs[b % 2][c] - m)
        ps = jnp.sum(p.reshape(grp), axis=0)
        vt = vt_ref[0, hd * MLA_V:(hd + 1) * MLA_V, c * kc:(c + 1) * kc]
        pv = _dot(vt, p.astype(BF16))
        return (ps if l8 is None else l8 + ps), (pv if acc is None else acc + pv)

    def stage(k):
        run_v = k >= 1
        run_s = k < n_blk
        if run_v:
            m = jnp.max(m_ref[(k - 1) % 2], axis=0, keepdims=True)
        m8 = l8 = acc = None
        for c in range(n_kc):
            if run_v:
                l8, acc = value_chunk(k - 1, c, m, l8, acc)
            if run_s:
                m8 = score_chunk(k, c, m8)
        if run_s:
            m_ref[k % 2] = m8
        if run_v:
            hd, qb = divmod(k - 1, n_q)
            l = jnp.sum(l8, axis=0, keepdims=True)
            o_ref[0, qb * tq:(qb + 1) * tq, hd * MLA_V:(hd + 1) * MLA_V] = (acc / l).T.astype(BF16)

    one = jnp.minimum(pl.program_id(0) + 1, 1)
    for k in range(n_blk + 1):
        lax.fori_loop(0, one, lambda _, carry, k=k: (stage(k), carry)[1], 0)


def _attention(qt, k, vt):
    B, T, _ = k.shape
    n_q, tq = qt.shape[1], qt.shape[3]
    kc = min(T, ATTN_KV_CHUNK)
    hs = ATTN_HEADS_PER_STEP
    return pl.pallas_call(
        _attn_kernel,
        grid=(B, MLA_HEADS // hs),
        in_specs=[pl.BlockSpec((1, n_q, hs * HEAD_PAD, tq), lambda b, h: (b, 0, h, 0)),
                  pl.BlockSpec((1, T, hs * HEAD_PAD), lambda b, h: (b, 0, h)),
                  pl.BlockSpec((1, hs * MLA_V, T), lambda b, h: (b, h, 0))],
        out_specs=pl.BlockSpec((1, T, hs * MLA_V), lambda b, h: (b, 0, h)),
        out_shape=jax.ShapeDtypeStruct((B, T, MIX_A), BF16),
        scratch_shapes=[pltpu.VMEM((T // kc, kc, tq), F32), pltpu.VMEM((T // kc, kc, tq), F32),
                        pltpu.VMEM((2, SUBLANES, tq), F32)],
        compiler_params=_params("parallel", "parallel"),
        name="mla_attention",
    )(qt, k, vt)


def _even_back_kernel(x_ref, xp_ref, xn_ref, g_ref, wu_ref, wg_ref, a_ref, pw_ref, ps_ref, wo_ref,
                      o_ref, *, seq_len):
    tm = x_ref.shape[1]
    t0 = pl.program_id(1) * tm
    x = x_ref[0]
    g = g_ref[...]
    h = _rms(x, g).astype(BF16)
    hp = _rms(xp_ref[0], g).astype(BF16)
    hn = _rms(xn_ref[0], g).astype(BF16)
    wu = wu_ref[...]
    up = jnp.where(t0 > 0, _dot(hp, wu), 0.0)
    un = jnp.where(t0 + tm < seq_len, _dot(hn, wu), 0.0)
    u = _dot(h, wu)
    ue = jnp.concatenate([up, u, un], axis=0)
    t = (t0 + lax.broadcasted_iota(jnp.int32, (tm, 1), 0))
    gate = _dot(h, wg_ref[...])
    sg = _silu(gate)
    ya = (a_ref[0].astype(F32) * sg[:, :MIX_A]).astype(BF16)
    y = _dot(ya, wo_ref[:MIX_A, :])
    for gi, w in enumerate(POOL_WINDOWS):
        left = w // 2
        right = w - 1 - left
        cols = slice(gi * POOL_DIM, (gi + 1) * POOL_DIM)
        acc = ue[POOL_HALO - left:POOL_HALO - left + tm, cols]
        for j in range(-left + 1, right + 1):
            acc = acc + ue[POOL_HALO + j:POOL_HALO + j + tm, cols]
        cnt = (jnp.minimum(t + right, seq_len - 1) - jnp.maximum(t - left, 0) + 1).astype(F32)
        d = (acc / cnt - u[:, cols]).astype(BF16)
        bg = _dot(d, pw_ref[gi]) * ps_ref[:, cols]
        yb = (bg * sg[:, MIX_A + gi * POOL_DIM:MIX_A + (gi + 1) * POOL_DIM]).astype(BF16)
        y = y + _dot(yb, wo_ref[MIX_A + gi * POOL_DIM:MIX_A + (gi + 1) * POOL_DIM, :])
    o_ref[0] = x + y


def _even_back(x, g, wu, wg, a, pw, ps, wo):
    B, T, D = x.shape
    tm = min(T, ROW_TILE)
    nb = tm // POOL_HALO
    last = T // POOL_HALO - 1
    row = lambda w: pl.BlockSpec((1, tm, w), lambda b, t: (b, t, 0))
    return pl.pallas_call(
        functools.partial(_even_back_kernel, seq_len=T),
        grid=(B, T // tm),
        in_specs=[row(D),
                  pl.BlockSpec((1, POOL_HALO, D), lambda b, t: (b, jnp.maximum(t * nb - 1, 0), 0)),
                  pl.BlockSpec((1, POOL_HALO, D), lambda b, t: (b, jnp.minimum((t + 1) * nb, last), 0)),
                  _const_spec(g.shape), _const_spec(wu.shape), _const_spec(wg.shape), row(MIX_A),
                  _const_spec(pw.shape), _const_spec(ps.shape), _const_spec(wo.shape)],
        out_specs=row(D),
        out_shape=jax.ShapeDtypeStruct((B, T, D), F32),
        compiler_params=_params("parallel", "parallel"),
        name="even_back",
    )(x, x, x, g, wu, wg, a, pw, ps, wo)


def _odd_front_kernel(x_ref, g_ref, w_ref, c_ref, s_ref, q_ref, k_ref, v_ref):
    h = _rms(x_ref[0], g_ref[...]).astype(BF16)
    qkv = _dot(h, w_ref[...])
    c = c_ref[0]
    s = s_ref[0]
    half = RET_DK // 2
    k_scale = RET_DK ** -0.5
    for hd in range(RET_HEADS):
        for base, ref, scale in ((0, q_ref, None), (RET_QK, k_ref, k_scale)):
            lo = base + hd * RET_DK
            x1 = qkv[:, lo:lo + half]
            x2 = qkv[:, lo + half:lo + RET_DK]
            o1 = x1 * c - x2 * s
            o2 = x2 * c + x1 * s
            if scale is not None:
                o1 = o1 * scale
                o2 = o2 * scale
            ref[0, :, hd * RET_DK:hd * RET_DK + half] = o1.astype(BF16)
            ref[0, :, hd * RET_DK + half:(hd + 1) * RET_DK] = o2.astype(BF16)
    v_ref[0] = qkv[:, 2 * RET_QK:].astype(BF16)


def _odd_front(x, g, w, c, s):
    B, T, D = x.shape
    tm = min(T, ROW_TILE)
    row = lambda wd: pl.BlockSpec((1, tm, wd), lambda b, t: (b, t, 0))
    qk = jax.ShapeDtypeStruct((B, T, RET_QK), BF16)
    return pl.pallas_call(
        _odd_front_kernel,
        grid=(B, T // tm),
        in_specs=[row(D), _const_spec(g.shape), _const_spec(w.shape), row(LANES), row(LANES)],
        out_specs=[row(RET_QK), row(RET_QK), row(RET_V)],
        out_shape=[qk, qk, jax.ShapeDtypeStruct((B, T, RET_V), BF16)],
        compiler_params=_params("parallel", "parallel"),
        name="odd_front",
    )(x, g, w, c, s)


def _log_sigmoid(x):
    return jnp.minimum(x, 0.0) - jnp.log1p(jnp.exp(-jnp.abs(x)))


def _retention_kernel(q_ref, k_ref, v_ref, df_ref, db_ref, gn_ref, o_ref, sf_ref, acc_ref):
    T = q_ref.shape[1]
    C = min(T, RET_CHUNK)
    n_chunks = T // C
    lf = _log_sigmoid(df_ref[0])
    lb = _log_sigmoid(db_ref[0])
    lf1 = lf[:, :1]
    lb1 = lb[:, :1]
    ri = lax.broadcasted_iota(jnp.int32, (C, 1), 0).astype(F32)
    diff = (lax.broadcasted_iota(jnp.int32, (C, C), 0)
            - lax.broadcasted_iota(jnp.int32, (C, C), 1)).astype(F32)
    dmat = jnp.exp(jnp.where(diff >= 0, diff * lf1, -diff * lb1))
    xi_f = jnp.exp((ri + 1.0) * lf1)
    xi_b = jnp.exp((C - ri) * lb1)
    zeta_f = jnp.exp((C - 1.0 - ri) * lf1)
    zeta_b = jnp.exp(ri * lb1)
    cd_f = jnp.exp(C * lf1)
    cd_b = jnp.exp(C * lb1)

    acc_ref[...] = jnp.zeros_like(acc_ref)

    def fwd(i, carry):
        rows = pl.ds(pl.multiple_of(i * C, C), C)
        sf_ref[i] = acc_ref[...].astype(BF16)
        kz = (k_ref[0, rows, :].astype(F32) * zeta_f).astype(BF16)
        acc_ref[...] = acc_ref[...] * cd_f + _dot_tn(kz, v_ref[0, rows, :])
        return carry

    lax.fori_loop(0, n_chunks, fwd, 0)
    acc_ref[...] = jnp.zeros_like(acc_ref)

    def bwd(j, carry):
        i = n_chunks - 1 - j
        rows = pl.ds(pl.multiple_of(i * C, C), C)
        q = q_ref[0, rows, :]
        k = k_ref[0, rows, :]
        v = v_ref[0, rows, :]
        s = (_dot_nt(q, k) * dmat).astype(BF16)
        o = _dot(s, v)
        o = o + _dot(q, sf_ref[i]) * xi_f
        o = o + _dot(q, acc_ref[...].astype(BF16)) * xi_b
        mu = jnp.mean(o, axis=-1, keepdims=True)
        oc = o - mu
        var = jnp.mean(oc * oc, axis=-1, keepdims=True)
        o_ref[0, rows, :] = (oc * lax.rsqrt(var + NORM_EPS) * gn_ref[...]).astype(BF16)
        kz = (k.astype(F32) * zeta_b).astype(BF16)
        acc_ref[...] = acc_ref[...] * cd_b + _dot_tn(kz, v)
        return carry

    lax.fori_loop(0, n_chunks, bwd, 0)


def _retention(q, k, v, dec_f, dec_b, gn_g):
    B, T, _ = q.shape
    C = min(T, RET_CHUNK)
    head = lambda w: pl.BlockSpec((1, T, w), lambda b, h: (b, 0, h))
    dec = pl.BlockSpec((1, 1, LANES), lambda b, h: (h, 0, 0))
    return pl.pallas_call(
        _retention_kernel,
        grid=(B, RET_HEADS),
        in_specs=[head(RET_DK), head(RET_DK), head(RET_DV), dec, dec,
                  pl.BlockSpec((1, RET_DV), lambda b, h: (0, h))],
        out_specs=head(RET_DV),
        out_shape=jax.ShapeDtypeStruct((B, T, RET_V), BF16),
        scratch_shapes=[pltpu.VMEM((T // C, RET_DK, RET_DV), BF16),
                        pltpu.VMEM((RET_DK, RET_DV), F32)],
        compiler_params=_params("parallel", "parallel"),
        name="retention",
    )(q, k, v, dec_f, dec_b, gn_g)


def _odd_back_kernel(x_ref, g_ref, wg_ref, a_ref, wo_ref, fg_ref, o_ref, *, final_norm):
    x = x_ref[0]
    h = _rms(x, g_ref[...]).astype(BF16)
    gate = _dot(h, wg_ref[...])
    y = (a_ref[0].astype(F32) * _silu(gate)).astype(BF16)
    out = x + _dot(y, wo_ref[...])
    if final_norm:
        out = _rms(out, fg_ref[...])
    o_ref[0] = out


def _odd_back(x, g, wg, a, wo, fg, final_norm):
    B, T, D = x.shape
    tm = min(T, ROW_TILE)
    row = lambda w: pl.BlockSpec((1, tm, w), lambda b, t: (b, t, 0))
    return pl.pallas_call(
        functools.partial(_odd_back_kernel, final_norm=final_norm),
        grid=(B, T // tm),
        in_specs=[row(D), _const_spec(g.shape), _const_spec(wg.shape), row(RET_V),
                  _const_spec(wo.shape), _const_spec(fg.shape)],
        out_specs=row(D),
        out_shape=jax.ShapeDtypeStruct((B, T, D), F32),
        compiler_params=_params("parallel", "parallel"),
        name="odd_back",
    )(x, g, wg, a, wo, fg)


def _even_weights(w_in, w_uq, w_ukv):
    n_lat = MLA_Q_LORA + MLA_KV_LORA
    wlat = jnp.concatenate(
        [w_in[:, :n_lat + MLA_ROPE], jnp.zeros((D_MODEL, LANES - MLA_ROPE), w_in.dtype)], axis=1)
    wu = w_in[:, n_lat + MLA_ROPE:n_lat + MLA_ROPE + MIX_B]
    wg = w_in[:, n_lat + MLA_ROPE + MIX_B:]
    uq = w_uq.reshape(MLA_Q_LORA, MLA_HEADS, MLA_NOPE + MLA_ROPE)
    uq = jnp.pad(uq, ((0, 0), (0, 0), (0, HEAD_PAD - MLA_NOPE - MLA_ROPE)))
    uqt = uq.reshape(MLA_Q_LORA, MLA_HEADS * HEAD_PAD).T
    ukv = w_ukv.reshape(MLA_KV_LORA, MLA_HEADS, MLA_NOPE + MLA_V)
    uk = ukv[:, :, :MLA_NOPE].reshape(MLA_KV_LORA, MIX_A)
    uvt = ukv[:, :, MLA_NOPE:].reshape(MLA_KV_LORA, MIX_A).T
    return tuple(w.astype(BF16) for w in (wlat, wu, wg, uqt, uk, uvt))


def kernel(x, positions, a_norm_g, a_w_in, a_q_norm_g, a_w_uq, a_kv_norm_g, a_w_ukv, a_pool_w,
           a_pool_scale, a_w_out, r_norm_g, r_w_in, r_decay_fwd, r_decay_bwd, r_gn_g, r_w_out,
           final_norm_g):
    depth = a_norm_g.shape[0] + r_norm_g.shape[0]
    assert depth % 2 == 0, "the final norm is fused into the last (odd) layer's back kernel"
    ca, sa, cat, sat, cr, sr = _rope_tables(positions)
    fg = final_norm_g[None]
    for layer in range(depth):
        i = layer // 2
        if layer % 2 == 0:
            wlat, wu, wg, wuqt, wuk, wuvt = _even_weights(a_w_in[i], a_w_uq[i], a_w_ukv[i])
            g = a_norm_g[i][None]
            qt, k, vt = _even_front(x, g, wlat, a_q_norm_g[i][None], wuqt, a_kv_norm_g[i][None],
                                    wuk, wuvt, ca, sa, cat, sat)
            a = _attention(qt, k, vt)
            x = _even_back(x, g, wu, wg, a, a_pool_w[i].astype(BF16), a_pool_scale[i][None],
                           a_w_out[i].astype(BF16))
        else:
            w = r_w_in[i]
            g = r_norm_g[i][None]
            q, k, v = _odd_front(x, g, w[:, :2 * RET_QK + RET_V].astype(BF16), cr, sr)
            dec_f = jnp.broadcast_to(r_decay_fwd[i][:, None, None], (RET_HEADS, 1, LANES))
            dec_b = jnp.broadcast_to(r_decay_bwd[i][:, None, None], (RET_HEADS, 1, LANES))
            o = _retention(q, k, v, dec_f, dec_b, r_gn_g[i][None])
            x = _odd_back(x, g, w[:, 2 * RET_QK + RET_V:].astype(BF16), o, r_w_out[i].astype(BF16),
                          fg, final_norm=(layer == depth - 1))
    return x
```

```python
import functools

import jax
import jax.numpy as jnp
from jax import lax
from jax.experimental import pallas as pl
from jax.experimental.pallas import tpu as pltpu

ROPE_BASE = 10000.0
NORM_EPS = 1e-6
LOG2E = 1.4426950408889634

D_MODEL = 1024
MLA_HEADS = 8
MLA_NOPE = 128
MLA_ROPE = 64
MLA_V = 128
MLA_Q_LORA = 384
MLA_KV_LORA = 128
POOL_WINDOWS = (2, 4, 8, 16)
POOL_DIM = 256
MIX_A = MLA_HEADS * MLA_V
MIX_B = len(POOL_WINDOWS) * POOL_DIM
RET_HEADS = 4
RET_DK = 256
RET_DV = 512
RET_QK = RET_HEADS * RET_DK
RET_V = RET_HEADS * RET_DV

LANES = 128
SUBLANES = 8
HEAD_PAD = 2 * LANES
POOL_HALO = SUBLANES
VMEM_LIMIT = 56 * 1024 * 1024

ROW_TILE = 512
ATTN_KV_CHUNK = 256
ATTN_HEADS_PER_STEP = 2
RET_CHUNK = 256

BF16 = jnp.bfloat16
F32 = jnp.float32


def _params(*sem):
    return pltpu.CompilerParams(dimension_semantics=sem, vmem_limit_bytes=VMEM_LIMIT)


def _const_spec(shape):
    nd = len(shape)
    return pl.BlockSpec(shape, lambda *_: (0,) * nd, pipeline_mode=pl.Buffered(1))


def _rms(x, g):
    return x * lax.rsqrt(jnp.mean(x * x, axis=-1, keepdims=True) + NORM_EPS) * g


def _silu(x):
    return x / (1.0 + jnp.exp(-x))


def _dot(a, b):
    return jnp.dot(a, b, preferred_element_type=F32)


def _dot_nt(a, b):
    return lax.dot_general(a, b, (((1,), (1,)), ((), ())), preferred_element_type=F32)


def _dot_tn(a, b):
    return lax.dot_general(a, b, (((0,), (0,)), ((), ())), preferred_element_type=F32)


def _rope_tab_kernel(pos_ref, inv_a_ref, msk_ref, sgn_ref, inv_r_ref,
                     ca_ref, sa_ref, cat_ref, sat_ref, cr_ref, sr_ref):
    half_a = MLA_ROPE // 2
    pos = pos_ref[0].astype(F32)
    ang_a = pos * inv_a_ref[...]
    cos_a = jnp.cos(ang_a)
    sin_a = jnp.sin(ang_a)
    ca_ref[0] = cos_a * msk_ref[...]
    sa_ref[0] = sin_a * sgn_ref[...]
    cat_ref[0] = cos_a.T[:half_a]
    sat_ref[0] = sin_a.T[:half_a]
    ang_r = pos * inv_r_ref[...]
    cr_ref[0] = jnp.cos(ang_r)
    sr_ref[0] = jnp.sin(ang_r)


def _rope_tables(positions):
    B, T = positions.shape
    half_a = MLA_ROPE // 2
    inv_a = 1.0 / (ROPE_BASE ** (jnp.arange(0, MLA_ROPE, 2, dtype=F32) / MLA_ROPE))
    inv_r = 1.0 / (ROPE_BASE ** (jnp.arange(0, RET_DK, 2, dtype=F32) / RET_DK))
    zeros = jnp.zeros((LANES - 2 * half_a,), F32)
    ones = jnp.ones((half_a,), F32)
    inv_a_p = jnp.concatenate([inv_a, inv_a, zeros])[None]
    msk = jnp.concatenate([ones, ones, zeros])[None]
    sgn = jnp.concatenate([-ones, ones, zeros])[None]
    tt = min(T, 1024)
    tab = jax.ShapeDtypeStruct((B, T, LANES), F32)
    tab_t = jax.ShapeDtypeStruct((B, half_a, T), F32)
    blk = pl.BlockSpec((1, tt, LANES), lambda b, t: (b, t, 0))
    blk_t = pl.BlockSpec((1, half_a, tt), lambda b, t: (b, 0, t))
    return pl.pallas_call(
        _rope_tab_kernel,
        grid=(B, T // tt),
        in_specs=[pl.BlockSpec((1, tt, 1), lambda b, t: (b, t, 0)),
                  _const_spec((1, LANES)), _const_spec((1, LANES)), _const_spec((1, LANES)),
                  _const_spec((1, LANES))],
        out_specs=[blk, blk, blk_t, blk_t, blk, blk],
        out_shape=[tab, tab, tab_t, tab_t, tab, tab],
        compiler_params=_params("parallel", "parallel"),
        name="rope_tables",
    )(positions.reshape(B, T, 1), inv_a_p, msk, sgn, inv_r[None])


def _rope_pad(v, c, s):
    q = LANES // 4
    return v * c + (pltpu.roll(v, 3 * q, 1) + pltpu.roll(v, q, 1)) * s


def _even_front_kernel(x_ref, g_ref, wlat_ref, gq_ref, wuqt_ref, gkv_ref, wuk_ref, wuvt_ref,
                       ca_ref, sa_ref, cat_ref, sat_ref, qt_ref, k_ref, vt_ref):
    h = _rms(x_ref[0], g_ref[...]).astype(BF16)
    lat = _dot(h, wlat_ref[...])
    cq = _rms(lat[:, :MLA_Q_LORA], gq_ref[...]).astype(BF16)
    qt = _dot_nt(wuqt_ref[...], cq)
    q_scale = (MLA_NOPE + MLA_ROPE) ** -0.5 * LOG2E
    ckv = _rms(lat[:, MLA_Q_LORA:MLA_Q_LORA + MLA_KV_LORA], gkv_ref[...]).astype(BF16)
    kn = _dot(ckv, wuk_ref[...])
    vt_ref[0] = _dot_nt(wuvt_ref[...], ckv).astype(BF16)
    kr = _rope_pad(lat[:, MLA_Q_LORA + MLA_KV_LORA:], ca_ref[0], sa_ref[0]).astype(BF16)
    ct = cat_ref[0]
    st = sat_ref[0]
    half = MLA_ROPE // 2
    for hd in range(MLA_HEADS):
        lo = hd * HEAD_PAD
        r1 = lo + MLA_NOPE
        r2 = r1 + half
        r3 = r2 + half
        x1 = qt[r1:r2]
        x2 = qt[r2:r3]
        qt_ref[0, 0, lo:r1, :] = (qt[lo:r1] * q_scale).astype(BF16)
        qt_ref[0, 0, r1:r2, :] = ((x1 * ct - x2 * st) * q_scale).astype(BF16)
        qt_ref[0, 0, r2:r3, :] = ((x2 * ct + x1 * st) * q_scale).astype(BF16)
        qt_ref[0, 0, r3:lo + HEAD_PAD, :] = jnp.zeros((lo + HEAD_PAD - r3, qt.shape[1]), BF16)
        k_ref[0, :, lo:lo + LANES] = kn[:, hd * MLA_NOPE:(hd + 1) * MLA_NOPE].astype(BF16)
        k_ref[0, :, lo + LANES:lo + HEAD_PAD] = kr


def _even_front(x, g, wlat, gq, wuqt, gkv, wuk, wuvt, ca, sa, cat, sat):
    B, T, D = x.shape
    tm = min(T, ROW_TILE)
    row = lambda w: pl.BlockSpec((1, tm, w), lambda b, t: (b, t, 0))
    col = lambda r: pl.BlockSpec((1, r, tm), lambda b, t: (b, 0, t))
    consts = [g, wlat, gq, wuqt, gkv, wuk, wuvt]
    return pl.pallas_call(
        _even_front_kernel,
        grid=(B, T // tm),
        in_specs=[row(D)] + [_const_spec(c.shape) for c in consts]
        + [row(LANES), row(LANES), col(MLA_ROPE // 2), col(MLA_ROPE // 2)],
        out_specs=[pl.BlockSpec((1, 1, MLA_HEADS * HEAD_PAD, tm), lambda b, t: (b, t, 0, 0)),
                   row(MLA_HEADS * HEAD_PAD), col(MIX_A)],
        out_shape=[jax.ShapeDtypeStruct((B, T // tm, MLA_HEADS * HEAD_PAD, tm), BF16),
                   jax.ShapeDtypeStruct((B, T, MLA_HEADS * HEAD_PAD), BF16),
                   jax.ShapeDtypeStruct((B, MIX_A, T), BF16)],
        compiler_params=_params("parallel", "parallel"),
        name="even_front",
    )(x, *consts, ca, sa, cat, sat)


def _attn_kernel(qt_ref, k_ref, vt_ref, o_ref, sa_ref, sb_ref, m_ref):
    n_q, tq = qt_ref.shape[1], qt_ref.shape[3]
    n_kc, kc = sa_ref.shape[0], sa_ref.shape[1]
    n_blk = n_q * (qt_ref.shape[2] // HEAD_PAD)
    grp = (kc // SUBLANES, SUBLANES, tq)
    s_bufs = (sa_ref, sb_ref)

    def score_chunk(b, c, m8):
        hd, qb = divmod(b, n_q)
        qt = qt_ref[0, qb, hd * HEAD_PAD:(hd + 1) * HEAD_PAD, :]
        s = _dot(k_ref[0, c * kc:(c + 1) * kc, hd * HEAD_PAD:(hd + 1) * HEAD_PAD], qt)
        s_bufs[b % 2][c] = s
        cm = jnp.max(s.reshape(grp), axis=0)
        return cm if m8 is None else jnp.maximum(m8, cm)

    def value_chunk(b, c, m, l8, acc):
        hd = b // n_q
        p = jnp.exp2(s_bufs[b % 2][c] - m)
        ps = jnp.sum(p.reshape(grp), axis=0)
        vt = vt_ref[0, hd * MLA_V:(hd + 1) * MLA_V, c * kc:(c + 1) * kc]
        pv = _dot(vt, p.astype(BF16))
        return (ps if l8 is None else l8 + ps), (pv if acc is None else acc + pv)

    def stage(k):
        run_v = k >= 1
        run_s = k < n_blk
        if run_v:
            m = jnp.max(m_ref[(k - 1) % 2], axis=0, keepdims=True)
        m8 = l8 = acc = None
        for c in range(n_kc):
            if run_v:
                l8, acc = value_chunk(k - 1, c, m, l8, acc)
            if run_s:
                m8 = score_chunk(k, c, m8)
        if run_s:
            m_ref[k % 2] = m8
        if run_v:
            hd, qb = divmod(k - 1, n_q)
            l = jnp.sum(l8, axis=0, keepdims=True)
            o_ref[0, qb * tq:(qb + 1) * tq, hd * MLA_V:(hd + 1) * MLA_V] = (acc / l).T.astype(BF16)

    one = jnp.minimum(pl.program_id(0) + 1, 1)
    for k in range(n_blk + 1):
        lax.fori_loop(0, one, lambda _, carry, k=k: (stage(k), carry)[1], 0)


def _attention(qt, k, vt):
    B, T, _ = k.shape
    n_q, tq = qt.shape[1], qt.shape[3]
    kc = min(T, ATTN_KV_CHUNK)
    hs = ATTN_HEADS_PER_STEP
    return pl.pallas_call(
        _attn_kernel,
        grid=(B, MLA_HEADS // hs),
        in_specs=[pl.BlockSpec((1, n_q, hs * HEAD_PAD, tq), lambda b, h: (b, 0, h, 0)),
                  pl.BlockSpec((1, T, hs * HEAD_PAD), lambda b, h: (b, 0, h)),
                  pl.BlockSpec((1, hs * MLA_V, T), lambda b, h: (b, h, 0))],
        out_specs=pl.BlockSpec((1, T, hs * MLA_V), lambda b, h: (b, 0, h)),
        out_shape=jax.ShapeDtypeStruct((B, T, MIX_A), BF16),
        scratch_shapes=[pltpu.VMEM((T // kc, kc, tq), F32), pltpu.VMEM((T // kc, kc, tq), F32),
                        pltpu.VMEM((2, SUBLANES, tq), F32)],
        compiler_params=_params("parallel", "parallel"),
        name="mla_attention",
    )(qt, k, vt)


def _even_back_kernel(x_ref, xp_ref, xn_ref, g_ref, wu_ref, wg_ref, a_ref, pw_ref, ps_ref, wo_ref,
                      o_ref, *, seq_len):
    tm = x_ref.shape[1]
    t0 = pl.program_id(1) * tm
    x = x_ref[0]
    g = g_ref[...]
    h = _rms(x, g).astype(BF16)
    hp = _rms(xp_ref[0], g).astype(BF16)
    hn = _rms(xn_ref[0], g).astype(BF16)
    wu = wu_ref[...]
    up = jnp.where(t0 > 0, _dot(hp, wu), 0.0)
    un = jnp.where(t0 + tm < seq_len, _dot(hn, wu), 0.0)
    u = _dot(h, wu)
    ue = jnp.concatenate([up, u, un], axis=0)
    t = (t0 + lax.broadcasted_iota(jnp.int32, (tm, 1), 0))
    gate = _dot(h, wg_ref[...])
    sg = _silu(gate)
    ya = (a_ref[0].astype(F32) * sg[:, :MIX_A]).astype(BF16)
    y = _dot(ya, wo_ref[:MIX_A, :])
    for gi, w in enumerate(POOL_WINDOWS):
        left = w // 2
        right = w - 1 - left
        cols = slice(gi * POOL_DIM, (gi + 1) * POOL_DIM)
        acc = ue[POOL_HALO - left:POOL_HALO - left + tm, cols]
        for j in range(-left + 1, right + 1):
            acc = acc + ue[POOL_HALO + j:POOL_HALO + j + tm, cols]
        cnt = (jnp.minimum(t + right, seq_len - 1) - jnp.maximum(t - left, 0) + 1).astype(F32)
        d = (acc / cnt - u[:, cols]).astype(BF16)
        bg = _dot(d, pw_ref[gi]) * ps_ref[:, cols]
        yb = (bg * sg[:, MIX_A + gi * POOL_DIM:MIX_A + (gi + 1) * POOL_DIM]).astype(BF16)
        y = y + _dot(yb, wo_ref[MIX_A + gi * POOL_DIM:MIX_A + (gi + 1) * POOL_DIM, :])
    o_ref[0] = x + y


def _even_back(x, g, wu, wg, a, pw, ps, wo):
    B, T, D = x.shape
    tm = min(T, ROW_TILE)
    nb = tm // POOL_HALO
    last = T // POOL_HALO - 1
    row = lambda w: pl.BlockSpec((1, tm, w), lambda b, t: (b, t, 0))
    return pl.pallas_call(
        functools.partial(_even_back_kernel, seq_len=T),
        grid=(B, T // tm),
        in_specs=[row(D),
                  pl.BlockSpec((1, POOL_HALO, D), lambda b, t: (b, jnp.maximum(t * nb - 1, 0), 0)),
                  pl.BlockSpec((1, POOL_HALO, D), lambda b, t: (b, jnp.minimum((t + 1) * nb, last), 0)),
                  _const_spec(g.shape), _const_spec(wu.shape), _const_spec(wg.shape), row(MIX_A),
                  _const_spec(pw.shape), _const_spec(ps.shape), _const_spec(wo.shape)],
        out_specs=row(D),
        out_shape=jax.ShapeDtypeStruct((B, T, D), F32),
        compiler_params=_params("parallel", "parallel"),
        name="even_back",
    )(x, x, x, g, wu, wg, a, pw, ps, wo)


def _odd_front_kernel(x_ref, g_ref, w_ref, c_ref, s_ref, q_ref, k_ref, v_ref):
    h = _rms(x_ref[0], g_ref[...]).astype(BF16)
    qkv = _dot(h, w_ref[...])
    c = c_ref[0]
    s = s_ref[0]
    half = RET_DK // 2
    k_scale = RET_DK ** -0.5
    for hd in range(RET_HEADS):
        for base, ref, scale in ((0, q_ref, None), (RET_QK, k_ref, k_scale)):
            lo = base + hd * RET_DK
            x1 = qkv[:, lo:lo + half]
            x2 = qkv[:, lo + half:lo + RET_DK]
            o1 = x1 * c - x2 * s
            o2 = x2 * c + x1 * s
            if scale is not None:
                o1 = o1 * scale
                o2 = o2 * scale
            ref[0, :, hd * RET_DK:hd * RET_DK + half] = o1.astype(BF16)
            ref[0, :, hd * RET_DK + half:(hd + 1) * RET_DK] = o2.astype(BF16)
    v_ref[0] = qkv[:, 2 * RET_QK:].astype(BF16)


def _odd_front(x, g, w, c, s):
    B, T, D = x.shape
    tm = min(T, ROW_TILE)
    row = lambda wd: pl.BlockSpec((1, tm, wd), lambda b, t: (b, t, 0))
    qk = jax.ShapeDtypeStruct((B, T, RET_QK), BF16)
    return pl.pallas_call(
        _odd_front_kernel,
        grid=(B, T // tm),
        in_specs=[row(D), _const_spec(g.shape), _const_spec(w.shape), row(LANES), row(LANES)],
        out_specs=[row(RET_QK), row(RET_QK), row(RET_V)],
        out_shape=[qk, qk, jax.ShapeDtypeStruct((B, T, RET_V), BF16)],
        compiler_params=_params("parallel", "parallel"),
        name="odd_front",
    )(x, g, w, c, s)


def _log_sigmoid(x):
    return jnp.minimum(x, 0.0) - jnp.log1p(jnp.exp(-jnp.abs(x)))


def _retention_kernel(q_ref, k_ref, v_ref, df_ref, db_ref, gn_ref, o_ref, sf_ref, acc_ref):
    T = q_ref.shape[1]
    C = min(T, RET_CHUNK)
    n_chunks = T // C
    lf = _log_sigmoid(df_ref[0])
    lb = _log_sigmoid(db_ref[0])
    lf1 = lf[:, :1]
    lb1 = lb[:, :1]
    ri = lax.broadcasted_iota(jnp.int32, (C, 1), 0).astype(F32)
    diff = (lax.broadcasted_iota(jnp.int32, (C, C), 0)
            - lax.broadcasted_iota(jnp.int32, (C, C), 1)).astype(F32)
    dmat = jnp.exp(jnp.where(diff >= 0, diff * lf1, -diff * lb1))
    xi_f = jnp.exp((ri + 1.0) * lf1)
    xi_b = jnp.exp((C - ri) * lb1)
    zeta_f = jnp.exp((C - 1.0 - ri) * lf1)
    zeta_b = jnp.exp(ri * lb1)
    cd_f = jnp.exp(C * lf1)
    cd_b = jnp.exp(C * lb1)

    def state_update(i, zeta, cd, first):
        rows = slice(i * C, (i + 1) * C)
        kz = (k_ref[0, rows, :].astype(F32) * zeta).astype(BF16)
        upd = _dot_tn(kz, v_ref[0, rows, :])
        acc_ref[...] = upd if first else acc_ref[...] * cd + upd

    for i in range(n_chunks - 1):
        state_update(i, zeta_f, cd_f, first=(i == 0))
        sf_ref[i + 1] = acc_ref[...].astype(BF16)

    for i in reversed(range(n_chunks)):
        rows = slice(i * C, (i + 1) * C)
        q = q_ref[0, rows, :]
        qf = q.astype(F32)
        s = (_dot_nt(q, k_ref[0, rows, :]) * dmat).astype(BF16)
        o = _dot(s, v_ref[0, rows, :])
        if i > 0:
            o = o + _dot((qf * xi_f).astype(BF16), sf_ref[i])
        if i < n_chunks - 1:
            o = o + _dot((qf * xi_b).astype(BF16), acc_ref[...].astype(BF16))
        mu = jnp.mean(o, axis=-1, keepdims=True)
        oc = o - mu
        var = jnp.mean(oc * oc, axis=-1, keepdims=True)
        o_ref[0, rows, :] = (oc * lax.rsqrt(var + NORM_EPS) * gn_ref[...]).astype(BF16)
        if i > 0:
            state_update(i, zeta_b, cd_b, first=(i == n_chunks - 1))


def _retention(q, k, v, dec_f, dec_b, gn_g):
    B, T, _ = q.shape
    C = min(T, RET_CHUNK)
    head = lambda w: pl.BlockSpec((1, T, w), lambda b, h: (b, 0, h))
    dec = pl.BlockSpec((1, 1, LANES), lambda b, h: (h, 0, 0))
    return pl.pallas_call(
        _retention_kernel,
        grid=(B, RET_HEADS),
        in_specs=[head(RET_DK), head(RET_DK), head(RET_DV), dec, dec,
                  pl.BlockSpec((1, RET_DV), lambda b, h: (0, h))],
        out_specs=head(RET_DV),
        out_shape=jax.ShapeDtypeStruct((B, T, RET_V), BF16),
        scratch_shapes=[pltpu.VMEM((T // C, RET_DK, RET_DV), BF16),
                        pltpu.VMEM((RET_DK, RET_DV), F32)],
        compiler_params=_params("parallel", "parallel"),
        name="retention",
    )(q, k, v, dec_f, dec_b, gn_g)


def _odd_back_kernel(x_ref, g_ref, wg_ref, a_ref, wo_ref, fg_ref, o_ref, *, final_norm):
    x = x_ref[0]
    h = _rms(x, g_ref[...]).astype(BF16)
    gate = _dot(h, wg_ref[...])
    y = (a_ref[0].astype(F32) * _silu(gate)).astype(BF16)
    out = x + _dot(y, wo_ref[...])
    if final_norm:
        out = _rms(out, fg_ref[...])
    o_ref[0] = out


def _odd_back(x, g, wg, a, wo, fg, final_norm):
    B, T, D = x.shape
    tm = min(T, ROW_TILE)
    row = lambda w: pl.BlockSpec((1, tm, w), lambda b, t: (b, t, 0))
    return pl.pallas_call(
        functools.partial(_odd_back_kernel, final_norm=final_norm),
        grid=(B, T // tm),
        in_specs=[row(D), _const_spec(g.shape), _const_spec(wg.shape), row(RET_V),
                  _const_spec(wo.shape), _const_spec(fg.shape)],
        out_specs=row(D),
        out_shape=jax.ShapeDtypeStruct((B, T, D), F32),
        compiler_params=_params("parallel", "parallel"),
        name="odd_back",
    )(x, g, wg, a, wo, fg)


def _even_weights(w_in, w_uq, w_ukv):
    n_lat = MLA_Q_LORA + MLA_KV_LORA
    wlat = jnp.concatenate(
        [w_in[:, :n_lat + MLA_ROPE], jnp.zeros((D_MODEL, LANES - MLA_ROPE), w_in.dtype)], axis=1)
    wu = w_in[:, n_lat + MLA_ROPE:n_lat + MLA_ROPE + MIX_B]
    wg = w_in[:, n_lat + MLA_ROPE + MIX_B:]
    uq = w_uq.reshape(MLA_Q_LORA, MLA_HEADS, MLA_NOPE + MLA_ROPE)
    uq = jnp.pad(uq, ((0, 0), (0, 0), (0, HEAD_PAD - MLA_NOPE - MLA_ROPE)))
    uqt = uq.reshape(MLA_Q_LORA, MLA_HEADS * HEAD_PAD).T
    ukv = w_ukv.reshape(MLA_KV_LORA, MLA_HEADS, MLA_NOPE + MLA_V)
    uk = ukv[:, :, :MLA_NOPE].reshape(MLA_KV_LORA, MIX_A)
    uvt = ukv[:, :, MLA_NOPE:].reshape(MLA_KV_LORA, MIX_A).T
    return tuple(w.astype(BF16) for w in (wlat, wu, wg, uqt, uk, uvt))


def kernel(x, positions, a_norm_g, a_w_in, a_q_norm_g, a_w_uq, a_kv_norm_g, a_w_ukv, a_pool_w,
           a_pool_scale, a_w_out, r_norm_g, r_w_in, r_decay_fwd, r_decay_bwd, r_gn_g, r_w_out,
           final_norm_g):
    depth = a_norm_g.shape[0] + r_norm_g.shape[0]
    assert depth % 2 == 0, "the final norm is fused into the last (odd) layer's back kernel"
    ca, sa, cat, sat, cr, sr = _rope_tables(positions)
    fg = final_norm_g[None]
    for layer in range(depth):
        i = layer // 2
        if layer % 2 == 0:
            wlat, wu, wg, wuqt, wuk, wuvt = _even_weights(a_w_in[i], a_w_uq[i], a_w_ukv[i])
            g = a_norm_g[i][None]
            qt, k, vt = _even_front(x, g, wlat, a_q_norm_g[i][None], wuqt, a_kv_norm_g[i][None],
                                    wuk, wuvt, ca, sa, cat, sat)
            a = _attention(qt, k, vt)
            x = _even_back(x, g, wu, wg, a, a_pool_w[i].astype(BF16), a_pool_scale[i][None],
                           a_w_out[i].astype(BF16))
        else:
            w = r_w_in[i]
            g = r_norm_g[i][None]
            q, k, v = _odd_front(x, g, w[:, :2 * RET_QK + RET_V].astype(BF16), cr, sr)
            dec_f = jnp.broadcast_to(r_decay_fwd[i][:, None, None], (RET_HEADS, 1, LANES))
            dec_b = jnp.broadcast_to(r_decay_bwd[i][:, None, None], (RET_HEADS, 1, LANES))
            o = _retention(q, k, v, dec_f, dec_b, r_gn_g[i][None])
            x = _odd_back(x, g, w[:, 2 * RET_QK + RET_V:].astype(BF16), o, r_w_out[i].astype(BF16),
                          fg, final_norm=(layer == depth - 1))
    return x
```

```python
import functools

import jax
import jax.numpy as jnp
from jax import lax
from jax.experimental import pallas as pl
from jax.experimental.pallas import tpu as pltpu

ROPE_BASE = 10000.0
NORM_EPS = 1e-6
LOG2E = 1.4426950408889634

D_MODEL = 1024
MLA_HEADS = 8
MLA_NOPE = 128
MLA_ROPE = 64
MLA_V = 128
MLA_Q_LORA = 384
MLA_KV_LORA = 128
POOL_WINDOWS = (2, 4, 8, 16)
POOL_DIM = 256
MIX_A = MLA_HEADS * MLA_V
MIX_B = len(POOL_WINDOWS) * POOL_DIM
RET_HEADS = 4
RET_DK = 256
RET_DV = 512
RET_QK = RET_HEADS * RET_DK
RET_V = RET_HEADS * RET_DV

LANES = 128
SUBLANES = 8
HEAD_PAD = 2 * LANES
POOL_HALO = SUBLANES
VMEM_LIMIT = 56 * 1024 * 1024

ROW_TILE = 512
ATTN_KV_CHUNK = 256
ATTN_HEADS_PER_STEP = 4
ATTN_GROUP = 2
RET_CHUNK = 256

BF16 = jnp.bfloat16
F32 = jnp.float32


def _params(*sem):
    return pltpu.CompilerParams(dimension_semantics=sem, vmem_limit_bytes=VMEM_LIMIT)


def _const_spec(shape):
    nd = len(shape)
    return pl.BlockSpec(shape, lambda *_: (0,) * nd, pipeline_mode=pl.Buffered(1))


def _rms(x, g):
    return x * lax.rsqrt(jnp.mean(x * x, axis=-1, keepdims=True) + NORM_EPS) * g


def _silu(x):
    return x / (1.0 + jnp.exp(-x))


def _dot(a, b):
    return jnp.dot(a, b, preferred_element_type=F32)


def _dot_nt(a, b):
    return lax.dot_general(a, b, (((1,), (1,)), ((), ())), preferred_element_type=F32)


def _dot_tn(a, b):
    return lax.dot_general(a, b, (((0,), (0,)), ((), ())), preferred_element_type=F32)


def _split3(a):
    hi = a.astype(BF16)
    r1 = a - hi.astype(F32)
    mid = r1.astype(BF16)
    lo = (r1 - mid.astype(F32)).astype(BF16)
    return jnp.concatenate([hi, mid, lo], axis=1)


def _rope_tab_kernel(pos_ref, inv_r_ref, selc_ref, sels_ref, selt_ref,
                     ca_ref, sa_ref, cat_ref, sat_ref, cr_ref, sr_ref):
    tt = pos_ref.shape[1]
    sub = min(tt, 256)
    for r in range(0, tt, sub):
        rows = slice(r, r + sub)
        ang = pos_ref[0, rows, :].astype(F32) * inv_r_ref[...]
        c = jnp.cos(ang)
        s = jnp.sin(ang)
        cr_ref[0, rows, :] = c
        sr_ref[0, rows, :] = s
        c3 = _split3(c)
        s3 = _split3(s)
        ca_ref[0, rows, :] = _dot(c3, selc_ref[...])
        sa_ref[0, rows, :] = _dot(s3, sels_ref[...])
        cat_ref[0, :, rows] = _dot_nt(selt_ref[...], c3)
        sat_ref[0, :, rows] = _dot_nt(selt_ref[...], s3)


def _rope_tables(positions):
    B, T = positions.shape
    half_a = MLA_ROPE // 2
    stride = RET_DK // MLA_ROPE
    inv_r = 1.0 / (ROPE_BASE ** (jnp.arange(0, RET_DK, 2, dtype=F32) / RET_DK))
    assert inv_r.shape[0] == LANES and stride * half_a == LANES
    pick = (jnp.arange(LANES)[:, None] == stride * jnp.arange(half_a)[None, :]).astype(BF16)
    zeros = jnp.zeros((LANES, LANES - 2 * half_a), BF16)
    sel_c = jnp.concatenate([pick, pick, zeros], axis=1)
    sel_s = jnp.concatenate([-pick, pick, zeros], axis=1)
    tile3 = lambda m: jnp.concatenate([m, m, m], axis=0)
    tt = min(T, 1024)
    tab = jax.ShapeDtypeStruct((B, T, LANES), F32)
    tab_t = jax.ShapeDtypeStruct((B, half_a, T), F32)
    blk = pl.BlockSpec((1, tt, LANES), lambda b, t: (b, t, 0))
    blk_t = pl.BlockSpec((1, half_a, tt), lambda b, t: (b, 0, t))
    return pl.pallas_call(
        _rope_tab_kernel,
        grid=(B, T // tt),
        in_specs=[pl.BlockSpec((1, tt, 1), lambda b, t: (b, t, 0)),
                  _const_spec((1, LANES)), _const_spec((3 * LANES, LANES)),
                  _const_spec((3 * LANES, LANES)), _const_spec((half_a, 3 * LANES))],
        out_specs=[blk, blk, blk_t, blk_t, blk, blk],
        out_shape=[tab, tab, tab_t, tab_t, tab, tab],
        compiler_params=_params("parallel", "parallel"),
        name="rope_tables",
    )(positions.reshape(B, T, 1), inv_r[None], tile3(sel_c), tile3(sel_s), tile3(pick).T)


def _rope_pad(v, c, s):
    q = LANES // 4
    return v * c + (pltpu.roll(v, 3 * q, 1) + pltpu.roll(v, q, 1)) * s


def _even_front_kernel(x_ref, g_ref, wlat_ref, gq_ref, wuqt_ref, gkv_ref, wuk_ref, wuvt_ref,
                       ca_ref, sa_ref, cat_ref, sat_ref, qt_ref, k_ref, vt_ref):
    h = _rms(x_ref[0], g_ref[...]).astype(BF16)
    lat = _dot(h, wlat_ref[...])
    cq = _rms(lat[:, :MLA_Q_LORA], gq_ref[...]).astype(BF16)
    qt = _dot_nt(wuqt_ref[...], cq)
    q_scale = (MLA_NOPE + MLA_ROPE) ** -0.5 * LOG2E
    ckv = _rms(lat[:, MLA_Q_LORA:MLA_Q_LORA + MLA_KV_LORA], gkv_ref[...]).astype(BF16)
    kn = _dot(ckv, wuk_ref[...])
    vt_ref[0] = _dot_nt(wuvt_ref[...], ckv).astype(BF16)
    kr = _rope_pad(lat[:, MLA_Q_LORA + MLA_KV_LORA:], ca_ref[0], sa_ref[0]).astype(BF16)
    ct = cat_ref[0]
    st = sat_ref[0]
    half = MLA_ROPE // 2
    for hd in range(MLA_HEADS):
        lo = hd * HEAD_PAD
        r1 = lo + MLA_NOPE
        r2 = r1 + half
        r3 = r2 + half
        x1 = qt[r1:r2]
        x2 = qt[r2:r3]
        qt_ref[0, 0, lo:r1, :] = (qt[lo:r1] * q_scale).astype(BF16)
        qt_ref[0, 0, r1:r2, :] = ((x1 * ct - x2 * st) * q_scale).astype(BF16)
        qt_ref[0, 0, r2:r3, :] = ((x2 * ct + x1 * st) * q_scale).astype(BF16)
        qt_ref[0, 0, r3:lo + HEAD_PAD, :] = jnp.zeros((lo + HEAD_PAD - r3, qt.shape[1]), BF16)
        k_ref[0, :, lo:lo + LANES] = kn[:, hd * MLA_NOPE:(hd + 1) * MLA_NOPE].astype(BF16)
        k_ref[0, :, lo + LANES:lo + HEAD_PAD] = kr


def _even_front(x, g, wlat, gq, wuqt, gkv, wuk, wuvt, ca, sa, cat, sat):
    B, T, D = x.shape
    tm = min(T, ROW_TILE)
    row = lambda w: pl.BlockSpec((1, tm, w), lambda b, t: (b, t, 0))
    col = lambda r: pl.BlockSpec((1, r, tm), lambda b, t: (b, 0, t))
    consts = [g, wlat, gq, wuqt, gkv, wuk, wuvt]
    return pl.pallas_call(
        _even_front_kernel,
        grid=(B, T // tm),
        in_specs=[row(D)] + [_const_spec(c.shape) for c in consts]
        + [row(LANES), row(LANES), col(MLA_ROPE // 2), col(MLA_ROPE // 2)],
        out_specs=[pl.BlockSpec((1, 1, MLA_HEADS * HEAD_PAD, tm), lambda b, t: (b, t, 0, 0)),
                   row(MLA_HEADS * HEAD_PAD), col(MIX_A)],
        out_shape=[jax.ShapeDtypeStruct((B, T // tm, MLA_HEADS * HEAD_PAD, tm), BF16),
                   jax.ShapeDtypeStruct((B, T, MLA_HEADS * HEAD_PAD), BF16),
                   jax.ShapeDtypeStruct((B, MIX_A, T), BF16)],
        compiler_params=_params("parallel", "parallel"),
        name="even_front",
    )(x, *consts, ca, sa, cat, sat)


def _attn_kernel(qt_ref, k_ref, vt_ref, o_ref, s_ref, m_ref):
    n_q, tq = qt_ref.shape[1], qt_ref.shape[3]
    grp_blocks, n_kc, kc = s_ref.shape[1], s_ref.shape[2], s_ref.shape[3]
    n_blk = n_q * (qt_ref.shape[2] // HEAD_PAD)
    n_grp = n_blk // grp_blocks
    sub = (kc // SUBLANES, SUBLANES, tq)

    def score_chunk(g, j, c, m8):
        hd, qb = divmod(g * grp_blocks + j, n_q)
        qt = qt_ref[0, qb, hd * HEAD_PAD:(hd + 1) * HEAD_PAD, :]
        s = _dot(k_ref[0, c * kc:(c + 1) * kc, hd * HEAD_PAD:(hd + 1) * HEAD_PAD], qt)
        s_ref[g % 2, j, c] = s
        cm = jnp.max(s.reshape(sub), axis=0)
        return cm if m8 is None else jnp.maximum(m8, cm)

    def value_chunk(g, j, c, m, l8, acc):
        hd = (g * grp_blocks + j) // n_q
        p = jnp.exp2(s_ref[g % 2, j, c] - m)
        ps = jnp.sum(p.reshape(sub), axis=0)
        vt = vt_ref[0, hd * MLA_V:(hd + 1) * MLA_V, c * kc:(c + 1) * kc]
        pv = _dot(vt, p.astype(BF16))
        return (ps if l8 is None else l8 + ps), (pv if acc is None else acc + pv)

    def stage(k):
        run_v = k >= 1
        run_s = k < n_grp
        blocks = range(grp_blocks)
        if run_v:
            m = [jnp.max(m_ref[(k - 1) % 2, j], axis=0, keepdims=True) for j in blocks]
        m8 = [None] * grp_blocks
        l8 = [None] * grp_blocks
        acc = [None] * grp_blocks
        for c in range(n_kc):
            for j in blocks:
                if run_v:
                    l8[j], acc[j] = value_chunk(k - 1, j, c, m[j], l8[j], acc[j])
                if run_s:
                    m8[j] = score_chunk(k, j, c, m8[j])
        for j in blocks:
            if run_s:
                m_ref[k % 2, j] = m8[j]
            if run_v:
                hd, qb = divmod((k - 1) * grp_blocks + j, n_q)
                l = jnp.sum(l8[j], axis=0, keepdims=True)
                o_ref[0, qb * tq:(qb + 1) * tq, hd * MLA_V:(hd + 1) * MLA_V] = (
                    (acc[j] / l).T.astype(BF16))

    one = jnp.minimum(pl.program_id(0) + 1, 1)
    for k in range(n_grp + 1):
        lax.fori_loop(0, one, lambda _, carry, k=k: (stage(k), carry)[1], 0)


def _attention(qt, k, vt):
    B, T, _ = k.shape
    n_q, tq = qt.shape[1], qt.shape[3]
    kc = min(T, ATTN_KV_CHUNK)
    hs = ATTN_HEADS_PER_STEP
    return pl.pallas_call(
        _attn_kernel,
        grid=(B, MLA_HEADS // hs),
        in_specs=[pl.BlockSpec((1, n_q, hs * HEAD_PAD, tq), lambda b, h: (b, 0, h, 0)),
                  pl.BlockSpec((1, T, hs * HEAD_PAD), lambda b, h: (b, 0, h)),
                  pl.BlockSpec((1, hs * MLA_V, T), lambda b, h: (b, h, 0))],
        out_specs=pl.BlockSpec((1, T, hs * MLA_V), lambda b, h: (b, 0, h)),
        out_shape=jax.ShapeDtypeStruct((B, T, MIX_A), BF16),
        scratch_shapes=[pltpu.VMEM((2, ATTN_GROUP, T // kc, kc, tq), F32),
                        pltpu.VMEM((2, ATTN_GROUP, SUBLANES, tq), F32)],
        compiler_params=_params("parallel", "parallel"),
        name="mla_attention",
    )(qt, k, vt)


def _even_back_kernel(x_ref, xp_ref, xn_ref, g_ref, wu_ref, wg_ref, a_ref, pw_ref, ps_ref, wo_ref,
                      o_ref, *, seq_len):
    tm = x_ref.shape[1]
    t0 = pl.program_id(1) * tm
    x = x_ref[0]
    g = g_ref[...]
    h = _rms(x, g).astype(BF16)
    hp = _rms(xp_ref[0], g).astype(BF16)
    hn = _rms(xn_ref[0], g).astype(BF16)
    wu = wu_ref[...]
    up = jnp.where(t0 > 0, _dot(hp, wu), 0.0)
    un = jnp.where(t0 + tm < seq_len, _dot(hn, wu), 0.0)
    u = _dot(h, wu)
    ue = jnp.concatenate([up, u, un], axis=0)
    t = (t0 + lax.broadcasted_iota(jnp.int32, (tm, 1), 0))
    gate = _dot(h, wg_ref[...])
    sg = _silu(gate)
    ya = (a_ref[0].astype(F32) * sg[:, :MIX_A]).astype(BF16)
    y = _dot(ya, wo_ref[:MIX_A, :])
    for gi, w in enumerate(POOL_WINDOWS):
        left = w // 2
        right = w - 1 - left
        cols = slice(gi * POOL_DIM, (gi + 1) * POOL_DIM)
        acc = ue[POOL_HALO - left:POOL_HALO - left + tm, cols]
        for j in range(-left + 1, right + 1):
            acc = acc + ue[POOL_HALO + j:POOL_HALO + j + tm, cols]
        cnt = (jnp.minimum(t + right, seq_len - 1) - jnp.maximum(t - left, 0) + 1).astype(F32)
        d = (acc / cnt - u[:, cols]).astype(BF16)
        bg = _dot(d, pw_ref[gi]) * ps_ref[:, cols]
        yb = (bg * sg[:, MIX_A + gi * POOL_DIM:MIX_A + (gi + 1) * POOL_DIM]).astype(BF16)
        y = y + _dot(yb, wo_ref[MIX_A + gi * POOL_DIM:MIX_A + (gi + 1) * POOL_DIM, :])
    o_ref[0] = x + y


def _even_back(x, g, wu, wg, a, pw, ps, wo):
    B, T, D = x.shape
    tm = min(T, ROW_TILE)
    nb = tm // POOL_HALO
    last = T // POOL_HALO - 1
    row = lambda w: pl.BlockSpec((1, tm, w), lambda b, t: (b, t, 0))
    return pl.pallas_call(
        functools.partial(_even_back_kernel, seq_len=T),
        grid=(B, T // tm),
        in_specs=[row(D),
                  pl.BlockSpec((1, POOL_HALO, D), lambda b, t: (b, jnp.maximum(t * nb - 1, 0), 0)),
                  pl.BlockSpec((1, POOL_HALO, D), lambda b, t: (b, jnp.minimum((t + 1) * nb, last), 0)),
                  _const_spec(g.shape), _const_spec(wu.shape), _const_spec(wg.shape), row(MIX_A),
                  _const_spec(pw.shape), _const_spec(ps.shape), _const_spec(wo.shape)],
        out_specs=row(D),
        out_shape=jax.ShapeDtypeStruct((B, T, D), F32),
        compiler_params=_params("parallel", "parallel"),
        name="even_back",
    )(x, x, x, g, wu, wg, a, pw, ps, wo)


def _odd_front_kernel(x_ref, g_ref, w_ref, c_ref, s_ref, q_ref, k_ref, v_ref):
    h = _rms(x_ref[0], g_ref[...]).astype(BF16)
    qkv = _dot(h, w_ref[...])
    c = c_ref[0]
    s = s_ref[0]
    half = RET_DK // 2
    k_scale = RET_DK ** -0.5
    for hd in range(RET_HEADS):
        for base, ref, scale in ((0, q_ref, None), (RET_QK, k_ref, k_scale)):
            lo = base + hd * RET_DK
            x1 = qkv[:, lo:lo + half]
            x2 = qkv[:, lo + half:lo + RET_DK]
            o1 = x1 * c - x2 * s
            o2 = x2 * c + x1 * s
            if scale is not None:
                o1 = o1 * scale
                o2 = o2 * scale
            ref[0, :, hd * RET_DK:hd * RET_DK + half] = o1.astype(BF16)
            ref[0, :, hd * RET_DK + half:(hd + 1) * RET_DK] = o2.astype(BF16)
    v_ref[0] = qkv[:, 2 * RET_QK:].astype(BF16)


def _odd_front(x, g, w, c, s):
    B, T, D = x.shape
    tm = min(T, ROW_TILE)
    row = lambda wd: pl.BlockSpec((1, tm, wd), lambda b, t: (b, t, 0))
    qk = jax.ShapeDtypeStruct((B, T, RET_QK), BF16)
    return pl.pallas_call(
        _odd_front_kernel,
        grid=(B, T // tm),
        in_specs=[row(D), _const_spec(g.shape), _const_spec(w.shape), row(LANES), row(LANES)],
        out_specs=[row(RET_QK), row(RET_QK), row(RET_V)],
        out_shape=[qk, qk, jax.ShapeDtypeStruct((B, T, RET_V), BF16)],
        compiler_params=_params("parallel", "parallel"),
        name="odd_front",
    )(x, g, w, c, s)


def _log_sigmoid(x):
    return jnp.minimum(x, 0.0) - jnp.log1p(jnp.exp(-jnp.abs(x)))


def _retention_kernel(q_ref, k_ref, v_ref, df_ref, db_ref, gn_ref, o_ref, sf_ref, acc_ref):
    T = q_ref.shape[1]
    C = min(T, RET_CHUNK)
    n_chunks = T // C
    lf = _log_sigmoid(df_ref[0])
    lb = _log_sigmoid(db_ref[0])
    lf1 = lf[:, :1]
    lb1 = lb[:, :1]
    ri = lax.broadcasted_iota(jnp.int32, (C, 1), 0).astype(F32)
    diff = (lax.broadcasted_iota(jnp.int32, (C, C), 0)
            - lax.broadcasted_iota(jnp.int32, (C, C), 1)).astype(F32)
    dmat = jnp.exp(jnp.where(diff >= 0, diff * lf1, -diff * lb1))
    xi_f = jnp.exp((ri + 1.0) * lf1)
    xi_b = jnp.exp((C - ri) * lb1)
    zeta_f = jnp.exp((C - 1.0 - ri) * lf1)
    zeta_b = jnp.exp(ri * lb1)
    cd_f = jnp.exp(C * lf1)
    cd_b = jnp.exp(C * lb1)

    def state_update(i, zeta, cd, first):
        rows = slice(i * C, (i + 1) * C)
        kz = (k_ref[0, rows, :].astype(F32) * zeta).astype(BF16)
        upd = _dot_tn(kz, v_ref[0, rows, :])
        acc_ref[...] = upd if first else acc_ref[...] * cd + upd

    for i in range(n_chunks - 1):
        state_update(i, zeta_f, cd_f, first=(i == 0))
        sf_ref[i + 1] = acc_ref[...].astype(BF16)

    for i in reversed(range(n_chunks)):
        rows = slice(i * C, (i + 1) * C)
        q = q_ref[0, rows, :]
        qf = q.astype(F32)
        s = (_dot_nt(q, k_ref[0, rows, :]) * dmat).astype(BF16)
        o = _dot(s, v_ref[0, rows, :])
        if i > 0:
            o = o + _dot((qf * xi_f).astype(BF16), sf_ref[i])
        if i < n_chunks - 1:
            o = o + _dot((qf * xi_b).astype(BF16), acc_ref[...].astype(BF16))
        mu = jnp.mean(o, axis=-1, keepdims=True)
        oc = o - mu
        var = jnp.mean(oc * oc, axis=-1, keepdims=True)
        o_ref[0, rows, :] = (oc * lax.rsqrt(var + NORM_EPS) * gn_ref[...]).astype(BF16)
        if i > 0:
            state_update(i, zeta_b, cd_b, first=(i == n_chunks - 1))


def _retention(q, k, v, dec_f, dec_b, gn_g):
    B, T, _ = q.shape
    C = min(T, RET_CHUNK)
    head = lambda w: pl.BlockSpec((1, T, w), lambda b, h: (b, 0, h))
    dec = pl.BlockSpec((1, 1, LANES), lambda b, h: (h, 0, 0))
    return pl.pallas_call(
        _retention_kernel,
        grid=(B, RET_HEADS),
        in_specs=[head(RET_DK), head(RET_DK), head(RET_DV), dec, dec,
                  pl.BlockSpec((1, RET_DV), lambda b, h: (0, h))],
        out_specs=head(RET_DV),
        out_shape=jax.ShapeDtypeStruct((B, T, RET_V), BF16),
        scratch_shapes=[pltpu.VMEM((T // C, RET_DK, RET_DV), BF16),
                        pltpu.VMEM((RET_DK, RET_DV), F32)],
        compiler_params=_params("parallel", "parallel"),
        name="retention",
    )(q, k, v, dec_f, dec_b, gn_g)


def _odd_back_kernel(x_ref, g_ref, wg_ref, a_ref, wo_ref, fg_ref, o_ref, *, final_norm):
    x = x_ref[0]
    h = _rms(x, g_ref[...]).astype(BF16)
    gate = _dot(h, wg_ref[...])
    y = (a_ref[0].astype(F32) * _silu(gate)).astype(BF16)
    out = x + _dot(y, wo_ref[...])
    if final_norm:
        out = _rms(out, fg_ref[...])
    o_ref[0] = out


def _odd_back(x, g, wg, a, wo, fg, final_norm):
    B, T, D = x.shape
    tm = min(T, ROW_TILE)
    row = lambda w: pl.BlockSpec((1, tm, w), lambda b, t: (b, t, 0))
    return pl.pallas_call(
        functools.partial(_odd_back_kernel, final_norm=final_norm),
        grid=(B, T // tm),
        in_specs=[row(D), _const_spec(g.shape), _const_spec(wg.shape), row(RET_V),
                  _const_spec(wo.shape), _const_spec(fg.shape)],
        out_specs=row(D),
        out_shape=jax.ShapeDtypeStruct((B, T, D), F32),
        compiler_params=_params("parallel", "parallel"),
        name="odd_back",
    )(x, g, wg, a, wo, fg)


def _even_weights(w_in, w_uq, w_ukv):
    n_lat = MLA_Q_LORA + MLA_KV_LORA
    wlat = jnp.concatenate(
        [w_in[:, :n_lat + MLA_ROPE], jnp.zeros((D_MODEL, LANES - MLA_ROPE), w_in.dtype)], axis=1)
    wu = w_in[:, n_lat + MLA_ROPE:n_lat + MLA_ROPE + MIX_B]
    wg = w_in[:, n_lat + MLA_ROPE + MIX_B:]
    uq = w_uq.reshape(MLA_Q_LORA, MLA_HEADS, MLA_NOPE + MLA_ROPE)
    uq = jnp.pad(uq, ((0, 0), (0, 0), (0, HEAD_PAD - MLA_NOPE - MLA_ROPE)))
    uqt = uq.reshape(MLA_Q_LORA, MLA_HEADS * HEAD_PAD).T
    ukv = w_ukv.reshape(MLA_KV_LORA, MLA_HEADS, MLA_NOPE + MLA_V)
    uk = ukv[:, :, :MLA_NOPE].reshape(MLA_KV_LORA, MIX_A)
    uvt = ukv[:, :, MLA_NOPE:].reshape(MLA_KV_LORA, MIX_A).T
    return tuple(w.astype(BF16) for w in (wlat, wu, wg, uqt, uk, uvt))


def kernel(x, positions, a_norm_g, a_w_in, a_q_norm_g, a_w_uq, a_kv_norm_g, a_w_ukv, a_pool_w,
           a_pool_scale, a_w_out, r_norm_g, r_w_in, r_decay_fwd, r_decay_bwd, r_gn_g, r_w_out,
           final_norm_g):
    depth = a_norm_g.shape[0] + r_norm_g.shape[0]
    assert depth % 2 == 0, "the final norm is fused into the last (odd) layer's back kernel"
    ca, sa, cat, sat, cr, sr = _rope_tables(positions)
    fg = final_norm_g[None]
    for layer in range(depth):
        i = layer // 2
        if layer % 2 == 0:
            wlat, wu, wg, wuqt, wuk, wuvt = _even_weights(a_w_in[i], a_w_uq[i], a_w_ukv[i])
            g = a_norm_g[i][None]
            qt, k, vt = _even_front(x, g, wlat, a_q_norm_g[i][None], wuqt, a_kv_norm_g[i][None],
                                    wuk, wuvt, ca, sa, cat, sat)
            a = _attention(qt, k, vt)
            x = _even_back(x, g, wu, wg, a, a_pool_w[i].astype(BF16), a_pool_scale[i][None],
                           a_w_out[i].astype(BF16))
        else:
            w = r_w_in[i]
            g = r_norm_g[i][None]
            q, k, v = _odd_front(x, g, w[:, :2 * RET_QK + RET_V].astype(BF16), cr, sr)
            dec_f = jnp.broadcast_to(r_decay_fwd[i][:, None, None], (RET_HEADS, 1, LANES))
            dec_b = jnp.broadcast_to(r_decay_bwd[i][:, None, None], (RET_HEADS, 1, LANES))
            o = _retention(q, k, v, dec_f, dec_b, r_gn_g[i][None])
            x = _odd_back(x, g, w[:, 2 * RET_QK + RET_V:].astype(BF16), o, r_w_out[i].astype(BF16),
                          fg, final_norm=(layer == depth - 1))
    return x
```

```python
import functools

import jax
import jax.numpy as jnp
from jax import lax
from jax.experimental import pallas as pl
from jax.experimental.pallas import tpu as pltpu

ROPE_BASE = 10000.0
NORM_EPS = 1e-6
LOG2E = 1.4426950408889634

D_MODEL = 1024
MLA_HEADS = 8
MLA_NOPE = 128
MLA_ROPE = 64
MLA_V = 128
MLA_Q_LORA = 384
MLA_KV_LORA = 128
POOL_WINDOWS = (2, 4, 8, 16)
POOL_DIM = 256
MIX_A = MLA_HEADS * MLA_V
MIX_B = len(POOL_WINDOWS) * POOL_DIM
RET_HEADS = 4
RET_DK = 256
RET_DV = 512
RET_QK = RET_HEADS * RET_DK
RET_V = RET_HEADS * RET_DV

LANES = 128
SUBLANES = 8
HEAD_PAD = 2 * LANES
POOL_HALO = SUBLANES
VMEM_LIMIT = 56 * 1024 * 1024

ROW_TILE = 1024
SUB_TILE = 512
ATTN_KV_CHUNK = 256
ATTN_HEADS_PER_STEP = 4
ATTN_GROUP = 2
RET_CHUNK = 256
RET_HEADS_PER_STEP = 2

BF16 = jnp.bfloat16
F32 = jnp.float32


def _params(*sem):
    return pltpu.CompilerParams(dimension_semantics=sem, vmem_limit_bytes=VMEM_LIMIT)


def _const_spec(shape):
    nd = len(shape)
    return pl.BlockSpec(shape, lambda *_: (0,) * nd, pipeline_mode=pl.Buffered(1))


def _sub_tiles(rows):
    sub = min(rows, SUB_TILE)
    return [slice(r, r + sub) for r in range(0, rows, sub)]


def _rms(x, g):
    return x * lax.rsqrt(jnp.mean(x * x, axis=-1, keepdims=True) + NORM_EPS) * g


def _silu(x):
    return x / (1.0 + jnp.exp(-x))


def _dot(a, b):
    return jnp.dot(a, b, preferred_element_type=F32)


def _dot_nt(a, b):
    return lax.dot_general(a, b, (((1,), (1,)), ((), ())), preferred_element_type=F32)


def _dot_tn(a, b):
    return lax.dot_general(a, b, (((0,), (0,)), ((), ())), preferred_element_type=F32)


def _split3(a):
    hi = a.astype(BF16)
    r1 = a - hi.astype(F32)
    mid = r1.astype(BF16)
    lo = (r1 - mid.astype(F32)).astype(BF16)
    return jnp.concatenate([hi, mid, lo], axis=1)


def _rope_tab_kernel(pos_ref, inv_r_ref, selc_ref, sels_ref, selt_ref,
                     ca_ref, sa_ref, cat_ref, sat_ref, cr_ref, sr_ref):
    tt = pos_ref.shape[1]
    sub = min(tt, 256)
    for r in range(0, tt, sub):
        rows = slice(r, r + sub)
        ang = pos_ref[0, rows, :].astype(F32) * inv_r_ref[...]
        c = jnp.cos(ang)
        s = jnp.sin(ang)
        cr_ref[0, rows, :] = c
        sr_ref[0, rows, :] = s
        c3 = _split3(c)
        s3 = _split3(s)
        ca_ref[0, rows, :] = _dot(c3, selc_ref[...])
        sa_ref[0, rows, :] = _dot(s3, sels_ref[...])
        cat_ref[0, :, rows] = _dot_nt(selt_ref[...], c3)
        sat_ref[0, :, rows] = _dot_nt(selt_ref[...], s3)


def _rope_tables(positions):
    B, T = positions.shape
    half_a = MLA_ROPE // 2
    stride = RET_DK // MLA_ROPE
    inv_r = 1.0 / (ROPE_BASE ** (jnp.arange(0, RET_DK, 2, dtype=F32) / RET_DK))
    assert inv_r.shape[0] == LANES and stride * half_a == LANES
    pick = (jnp.arange(LANES)[:, None] == stride * jnp.arange(half_a)[None, :]).astype(BF16)
    zeros = jnp.zeros((LANES, LANES - 2 * half_a), BF16)
    sel_c = jnp.concatenate([pick, pick, zeros], axis=1)
    sel_s = jnp.concatenate([-pick, pick, zeros], axis=1)
    tile3 = lambda m: jnp.concatenate([m, m, m], axis=0)
    tt = min(T, 1024)
    tab = jax.ShapeDtypeStruct((B, T, LANES), F32)
    tab_t = jax.ShapeDtypeStruct((B, half_a, T), F32)
    blk = pl.BlockSpec((1, tt, LANES), lambda b, t: (b, t, 0))
    blk_t = pl.BlockSpec((1, half_a, tt), lambda b, t: (b, 0, t))
    return pl.pallas_call(
        _rope_tab_kernel,
        grid=(B, T // tt),
        in_specs=[pl.BlockSpec((1, tt, 1), lambda b, t: (b, t, 0)),
                  _const_spec((1, LANES)), _const_spec((3 * LANES, LANES)),
                  _const_spec((3 * LANES, LANES)), _const_spec((half_a, 3 * LANES))],
        out_specs=[blk, blk, blk_t, blk_t, blk, blk],
        out_shape=[tab, tab, tab_t, tab_t, tab, tab],
        compiler_params=_params("parallel", "parallel"),
        name="rope_tables",
    )(positions.reshape(B, T, 1), inv_r[None], tile3(sel_c), tile3(sel_s), tile3(pick).T)


def _rope_pad(v, c, s):
    q = LANES // 4
    return v * c + (pltpu.roll(v, 3 * q, 1) + pltpu.roll(v, q, 1)) * s


def _even_front_kernel(x_ref, g_ref, wlat_ref, gq_ref, wuqt_ref, gkv_ref, wuk_ref, wuvt_ref,
                       ca_ref, sa_ref, cat_ref, sat_ref, qt_ref, k_ref, vt_ref):
    q_scale = (MLA_NOPE + MLA_ROPE) ** -0.5 * LOG2E
    half = MLA_ROPE // 2
    for j, rows in enumerate(_sub_tiles(x_ref.shape[1])):
        h = _rms(x_ref[0, rows, :], g_ref[...]).astype(BF16)
        lat = _dot(h, wlat_ref[...])
        cq = _rms(lat[:, :MLA_Q_LORA], gq_ref[...]).astype(BF16)
        qt = _dot_nt(wuqt_ref[...], cq)
        ckv = _rms(lat[:, MLA_Q_LORA:MLA_Q_LORA + MLA_KV_LORA], gkv_ref[...]).astype(BF16)
        kn = _dot(ckv, wuk_ref[...])
        vt_ref[0, :, rows] = _dot_nt(wuvt_ref[...], ckv).astype(BF16)
        kr = _rope_pad(lat[:, MLA_Q_LORA + MLA_KV_LORA:], ca_ref[0, rows, :],
                       sa_ref[0, rows, :]).astype(BF16)
        ct = cat_ref[0, :, rows]
        st = sat_ref[0, :, rows]
        for hd in range(MLA_HEADS):
            lo = hd * HEAD_PAD
            r1 = lo + MLA_NOPE
            r2 = r1 + half
            r3 = r2 + half
            x1 = qt[r1:r2]
            x2 = qt[r2:r3]
            qt_ref[0, j, lo:r1, :] = (qt[lo:r1] * q_scale).astype(BF16)
            qt_ref[0, j, r1:r2, :] = ((x1 * ct - x2 * st) * q_scale).astype(BF16)
            qt_ref[0, j, r2:r3, :] = ((x2 * ct + x1 * st) * q_scale).astype(BF16)
            qt_ref[0, j, r3:lo + HEAD_PAD, :] = jnp.zeros((lo + HEAD_PAD - r3, qt.shape[1]), BF16)
            k_ref[0, rows, lo:lo + LANES] = kn[:, hd * MLA_NOPE:(hd + 1) * MLA_NOPE].astype(BF16)
            k_ref[0, rows, lo + LANES:lo + HEAD_PAD] = kr


def _even_front(x, g, wlat, gq, wuqt, gkv, wuk, wuvt, ca, sa, cat, sat):
    B, T, D = x.shape
    tm = min(T, ROW_TILE)
    sub = min(tm, SUB_TILE)
    row = lambda w: pl.BlockSpec((1, tm, w), lambda b, t: (b, t, 0))
    col = lambda r: pl.BlockSpec((1, r, tm), lambda b, t: (b, 0, t))
    consts = [g, wlat, gq, wuqt, gkv, wuk, wuvt]
    return pl.pallas_call(
        _even_front_kernel,
        grid=(B, T // tm),
        in_specs=[row(D)] + [_const_spec(c.shape) for c in consts]
        + [row(LANES), row(LANES), col(MLA_ROPE // 2), col(MLA_ROPE // 2)],
        out_specs=[pl.BlockSpec((1, tm // sub, MLA_HEADS * HEAD_PAD, sub), lambda b, t: (b, t, 0, 0)),
                   row(MLA_HEADS * HEAD_PAD), col(MIX_A)],
        out_shape=[jax.ShapeDtypeStruct((B, T // sub, MLA_HEADS * HEAD_PAD, sub), BF16),
                   jax.ShapeDtypeStruct((B, T, MLA_HEADS * HEAD_PAD), BF16),
                   jax.ShapeDtypeStruct((B, MIX_A, T), BF16)],
        compiler_params=_params("parallel", "parallel"),
        name="even_front",
    )(x, *consts, ca, sa, cat, sat)


def _attn_kernel(qt_ref, k_ref, vt_ref, o_ref, s_ref, m_ref):
    n_q, tq = qt_ref.shape[1], qt_ref.shape[3]
    grp_blocks, n_kc, kc = s_ref.shape[1], s_ref.shape[2], s_ref.shape[3]
    n_blk = n_q * (qt_ref.shape[2] // HEAD_PAD)
    n_grp = n_blk // grp_blocks
    sub = (kc // SUBLANES, SUBLANES, tq)

    def score_chunk(g, j, c, m8):
        hd, qb = divmod(g * grp_blocks + j, n_q)
        qt = qt_ref[0, qb, hd * HEAD_PAD:(hd + 1) * HEAD_PAD, :]
        s = _dot(k_ref[0, c * kc:(c + 1) * kc, hd * HEAD_PAD:(hd + 1) * HEAD_PAD], qt)
        s_ref[g % 2, j, c] = s
        cm = jnp.max(s.reshape(sub), axis=0)
        return cm if m8 is None else jnp.maximum(m8, cm)

    def value_chunk(g, j, c, m, l8, acc):
        hd = (g * grp_blocks + j) // n_q
        p = jnp.exp2(s_ref[g % 2, j, c] - m)
        ps = jnp.sum(p.reshape(sub), axis=0)
        vt = vt_ref[0, hd * MLA_V:(hd + 1) * MLA_V, c * kc:(c + 1) * kc]
        pv = _dot(vt, p.astype(BF16))
        return (ps if l8 is None else l8 + ps), (pv if acc is None else acc + pv)

    def stage(k):
        run_v = k >= 1
        run_s = k < n_grp
        blocks = range(grp_blocks)
        if run_v:
            m = [jnp.max(m_ref[(k - 1) % 2, j], axis=0, keepdims=True) for j in blocks]
        m8 = [None] * grp_blocks
        l8 = [None] * grp_blocks
        acc = [None] * grp_blocks
        for c in range(n_kc):
            for j in blocks:
                if run_v:
                    l8[j], acc[j] = value_chunk(k - 1, j, c, m[j], l8[j], acc[j])
                if run_s:
                    m8[j] = score_chunk(k, j, c, m8[j])
        for j in blocks:
            if run_s:
                m_ref[k % 2, j] = m8[j]
            if run_v:
                hd, qb = divmod((k - 1) * grp_blocks + j, n_q)
                l = jnp.sum(l8[j], axis=0, keepdims=True)
                o_ref[0, qb * tq:(qb + 1) * tq, hd * MLA_V:(hd + 1) * MLA_V] = (
                    (acc[j] / l).T.astype(BF16))

    one = jnp.minimum(pl.program_id(0) + 1, 1)
    for k in range(n_grp + 1):
        lax.fori_loop(0, one, lambda _, carry, k=k: (stage(k), carry)[1], 0)


def _attention(qt, k, vt):
    B, T, _ = k.shape
    n_q, tq = qt.shape[1], qt.shape[3]
    kc = min(T, ATTN_KV_CHUNK)
    hs = ATTN_HEADS_PER_STEP
    return pl.pallas_call(
        _attn_kernel,
        grid=(B, MLA_HEADS // hs),
        in_specs=[pl.BlockSpec((1, n_q, hs * HEAD_PAD, tq), lambda b, h: (b, 0, h, 0)),
                  pl.BlockSpec((1, T, hs * HEAD_PAD), lambda b, h: (b, 0, h)),
                  pl.BlockSpec((1, hs * MLA_V, T), lambda b, h: (b, h, 0))],
        out_specs=pl.BlockSpec((1, T, hs * MLA_V), lambda b, h: (b, 0, h)),
        out_shape=jax.ShapeDtypeStruct((B, T, MIX_A), BF16),
        scratch_shapes=[pltpu.VMEM((2, ATTN_GROUP, T // kc, kc, tq), F32),
                        pltpu.VMEM((2, ATTN_GROUP, SUBLANES, tq), F32)],
        compiler_params=_params("parallel", "parallel"),
        name="mla_attention",
    )(qt, k, vt)


def _even_back_kernel(x_ref, xp_ref, xn_ref, g_ref, wu_ref, wg_ref, a_ref, pw_ref, ps_ref, wo_ref,
                      o_ref, *, seq_len):
    tm = x_ref.shape[1]
    g = g_ref[...]
    wu = wu_ref[...]
    for rows in _sub_tiles(tm):
        sub = rows.stop - rows.start
        t0 = pl.program_id(1) * tm + rows.start
        x = x_ref[0, rows, :]
        h = _rms(x, g).astype(BF16)
        xp = xp_ref[0] if rows.start == 0 else x_ref[0, rows.start - POOL_HALO:rows.start, :]
        xn = xn_ref[0] if rows.stop == tm else x_ref[0, rows.stop:rows.stop + POOL_HALO, :]
        hp = _rms(xp, g).astype(BF16)
        hn = _rms(xn, g).astype(BF16)
        up = jnp.where(t0 > 0, _dot(hp, wu), 0.0)
        un = jnp.where(t0 + sub < seq_len, _dot(hn, wu), 0.0)
        u = _dot(h, wu)
        ue = jnp.concatenate([up, u, un], axis=0)
        t = (t0 + lax.broadcasted_iota(jnp.int32, (sub, 1), 0))
        gate = _dot(h, wg_ref[...])
        sg = _silu(gate)
        ya = (a_ref[0, rows, :].astype(F32) * sg[:, :MIX_A]).astype(BF16)
        y = _dot(ya, wo_ref[:MIX_A, :])
        for gi, w in enumerate(POOL_WINDOWS):
            left = w // 2
            right = w - 1 - left
            cols = slice(gi * POOL_DIM, (gi + 1) * POOL_DIM)
            acc = ue[POOL_HALO - left:POOL_HALO - left + sub, cols]
            for j in range(-left + 1, right + 1):
                acc = acc + ue[POOL_HALO + j:POOL_HALO + j + sub, cols]
            cnt = (jnp.minimum(t + right, seq_len - 1) - jnp.maximum(t - left, 0) + 1).astype(F32)
            d = (acc / cnt - u[:, cols]).astype(BF16)
            bg = _dot(d, pw_ref[gi]) * ps_ref[:, cols]
            yb = (bg * sg[:, MIX_A + gi * POOL_DIM:MIX_A + (gi + 1) * POOL_DIM]).astype(BF16)
            y = y + _dot(yb, wo_ref[MIX_A + gi * POOL_DIM:MIX_A + (gi + 1) * POOL_DIM, :])
        o_ref[0, rows, :] = x + y


def _even_back(x, g, wu, wg, a, pw, ps, wo):
    B, T, D = x.shape
    tm = min(T, ROW_TILE)
    nb = tm // POOL_HALO
    last = T // POOL_HALO - 1
    row = lambda w: pl.BlockSpec((1, tm, w), lambda b, t: (b, t, 0))
    return pl.pallas_call(
        functools.partial(_even_back_kernel, seq_len=T),
        grid=(B, T // tm),
        in_specs=[row(D),
                  pl.BlockSpec((1, POOL_HALO, D), lambda b, t: (b, jnp.maximum(t * nb - 1, 0), 0)),
                  pl.BlockSpec((1, POOL_HALO, D), lambda b, t: (b, jnp.minimum((t + 1) * nb, last), 0)),
                  _const_spec(g.shape), _const_spec(wu.shape), _const_spec(wg.shape), row(MIX_A),
                  _const_spec(pw.shape), _const_spec(ps.shape), _const_spec(wo.shape)],
        out_specs=row(D),
        out_shape=jax.ShapeDtypeStruct((B, T, D), F32),
        compiler_params=_params("parallel", "parallel"),
        name="even_back",
    )(x, x, x, g, wu, wg, a, pw, ps, wo)


def _odd_front_kernel(x_ref, g_ref, w_ref, c_ref, s_ref, q_ref, k_ref, v_ref):
    half = RET_DK // 2
    k_scale = RET_DK ** -0.5
    for rows in _sub_tiles(x_ref.shape[1]):
        h = _rms(x_ref[0, rows, :], g_ref[...]).astype(BF16)
        qkv = _dot(h, w_ref[...])
        c = c_ref[0, rows, :]
        s = s_ref[0, rows, :]
        for hd in range(RET_HEADS):
            for base, ref, scale in ((0, q_ref, None), (RET_QK, k_ref, k_scale)):
                lo = base + hd * RET_DK
                x1 = qkv[:, lo:lo + half]
                x2 = qkv[:, lo + half:lo + RET_DK]
                o1 = x1 * c - x2 * s
                o2 = x2 * c + x1 * s
                if scale is not None:
                    o1 = o1 * scale
                    o2 = o2 * scale
                ref[0, rows, hd * RET_DK:hd * RET_DK + half] = o1.astype(BF16)
                ref[0, rows, hd * RET_DK + half:(hd + 1) * RET_DK] = o2.astype(BF16)
        v_ref[0, rows, :] = qkv[:, 2 * RET_QK:].astype(BF16)


def _odd_front(x, g, w, c, s):
    B, T, D = x.shape
    tm = min(T, ROW_TILE)
    row = lambda wd: pl.BlockSpec((1, tm, wd), lambda b, t: (b, t, 0))
    qk = jax.ShapeDtypeStruct((B, T, RET_QK), BF16)
    return pl.pallas_call(
        _odd_front_kernel,
        grid=(B, T // tm),
        in_specs=[row(D), _const_spec(g.shape), _const_spec(w.shape), row(LANES), row(LANES)],
        out_specs=[row(RET_QK), row(RET_QK), row(RET_V)],
        out_shape=[qk, qk, jax.ShapeDtypeStruct((B, T, RET_V), BF16)],
        compiler_params=_params("parallel", "parallel"),
        name="odd_front",
    )(x, g, w, c, s)


def _log_sigmoid(x):
    return jnp.minimum(x, 0.0) - jnp.log1p(jnp.exp(-jnp.abs(x)))


def _retention_kernel(q_ref, k_ref, v_ref, df_ref, db_ref, gn_ref, o_ref, sf_ref, acc_ref):
    T = q_ref.shape[1]
    C = min(T, RET_CHUNK)
    n_chunks = T // C
    heads = range(q_ref.shape[2] // RET_DK)
    ri = lax.broadcasted_iota(jnp.int32, (C, 1), 0).astype(F32)
    diff = (lax.broadcasted_iota(jnp.int32, (C, C), 0)
            - lax.broadcasted_iota(jnp.int32, (C, C), 1)).astype(F32)

    def decays(hd):
        lf1 = _log_sigmoid(df_ref[hd])[:, :1]
        lb1 = _log_sigmoid(db_ref[hd])[:, :1]
        return dict(
            dmat=jnp.exp(jnp.where(diff >= 0, diff * lf1, -diff * lb1)),
            xi_f=jnp.exp((ri + 1.0) * lf1),
            xi_b=jnp.exp((C - ri) * lb1),
            zeta_f=jnp.exp((C - 1.0 - ri) * lf1),
            zeta_b=jnp.exp(ri * lb1),
            cd_f=jnp.exp(C * lf1), cd_b=jnp.exp(C * lb1))

    dec = [decays(hd) for hd in heads]

    def state_update(hd, i, zeta, cd, first):
        rows = slice(i * C, (i + 1) * C)
        kz = (k_ref[0, rows, hd * RET_DK:(hd + 1) * RET_DK].astype(F32) * zeta).astype(BF16)
        upd = _dot_tn(kz, v_ref[0, rows, hd * RET_DV:(hd + 1) * RET_DV])
        acc_ref[hd] = upd if first else acc_ref[hd] * cd + upd

    for i in range(n_chunks - 1):
        for hd in heads:
            state_update(hd, i, dec[hd]["zeta_f"], dec[hd]["cd_f"], first=(i == 0))
            sf_ref[hd, i + 1] = acc_ref[hd].astype(BF16)

    for i in reversed(range(n_chunks)):
        rows = slice(i * C, (i + 1) * C)
        for hd in heads:
            d = dec[hd]
            vcols = slice(hd * RET_DV, (hd + 1) * RET_DV)
            q = q_ref[0, rows, hd * RET_DK:(hd + 1) * RET_DK]
            qf = q.astype(F32)
            s = (_dot_nt(q, k_ref[0, rows, hd * RET_DK:(hd + 1) * RET_DK]) * d["dmat"]).astype(BF16)
            o = _dot(s, v_ref[0, rows, vcols])
            if i > 0:
                o = o + _dot((qf * d["xi_f"]).astype(BF16), sf_ref[hd, i])
            if i < n_chunks - 1:
                o = o + _dot((qf * d["xi_b"]).astype(BF16), acc_ref[hd].astype(BF16))
            mu = jnp.mean(o, axis=-1, keepdims=True)
            oc = o - mu
            var = jnp.mean(oc * oc, axis=-1, keepdims=True)
            o_ref[0, rows, vcols] = (oc * lax.rsqrt(var + NORM_EPS) * gn_ref[:, vcols]).astype(BF16)
            if i > 0:
                state_update(hd, i, d["zeta_b"], d["cd_b"], first=(i == n_chunks - 1))


def _retention(q, k, v, dec_f, dec_b, gn_g):
    B, T, _ = q.shape
    C = min(T, RET_CHUNK)
    hs = RET_HEADS_PER_STEP
    head = lambda w: pl.BlockSpec((1, T, hs * w), lambda b, h: (b, 0, h))
    dec = pl.BlockSpec((hs, 1, LANES), lambda b, h: (h, 0, 0))
    return pl.pallas_call(
        _retention_kernel,
        grid=(B, RET_HEADS // hs),
        in_specs=[head(RET_DK), head(RET_DK), head(RET_DV), dec, dec,
                  pl.BlockSpec((1, hs * RET_DV), lambda b, h: (0, h))],
        out_specs=head(RET_DV),
        out_shape=jax.ShapeDtypeStruct((B, T, RET_V), BF16),
        scratch_shapes=[pltpu.VMEM((hs, T // C, RET_DK, RET_DV), BF16),
                        pltpu.VMEM((hs, RET_DK, RET_DV), F32)],
        compiler_params=_params("parallel", "parallel"),
        name="retention",
    )(q, k, v, dec_f, dec_b, gn_g)


def _odd_back_kernel(x_ref, g_ref, wg_ref, a_ref, wo_ref, fg_ref, o_ref, *, final_norm):
    for rows in _sub_tiles(x_ref.shape[1]):
        x = x_ref[0, rows, :]
        h = _rms(x, g_ref[...]).astype(BF16)
        gate = _dot(h, wg_ref[...])
        y = (a_ref[0, rows, :].astype(F32) * _silu(gate)).astype(BF16)
        out = x + _dot(y, wo_ref[...])
        if final_norm:
            out = _rms(out, fg_ref[...])
        o_ref[0, rows, :] = out


def _odd_back(x, g, wg, a, wo, fg, final_norm):
    B, T, D = x.shape
    tm = min(T, ROW_TILE)
    row = lambda w: pl.BlockSpec((1, tm, w), lambda b, t: (b, t, 0))
    return pl.pallas_call(
        functools.partial(_odd_back_kernel, final_norm=final_norm),
        grid=(B, T // tm),
        in_specs=[row(D), _const_spec(g.shape), _const_spec(wg.shape), row(RET_V),
                  _const_spec(wo.shape), _const_spec(fg.shape)],
        out_specs=row(D),
        out_shape=jax.ShapeDtypeStruct((B, T, D), F32),
        compiler_params=_params("parallel", "parallel"),
        name="odd_back",
    )(x, g, wg, a, wo, fg)


def _even_weights(w_in, w_uq, w_ukv):
    n_lat = MLA_Q_LORA + MLA_KV_LORA
    wlat = jnp.concatenate(
        [w_in[:, :n_lat + MLA_ROPE], jnp.zeros((D_MODEL, LANES - MLA_ROPE), w_in.dtype)], axis=1)
    wu = w_in[:, n_lat + MLA_ROPE:n_lat + MLA_ROPE + MIX_B]
    wg = w_in[:, n_lat + MLA_ROPE + MIX_B:]
    uq = w_uq.reshape(MLA_Q_LORA, MLA_HEADS, MLA_NOPE + MLA_ROPE)
    uq = jnp.pad(uq, ((0, 0), (0, 0), (0, HEAD_PAD - MLA_NOPE - MLA_ROPE)))
    uqt = uq.reshape(MLA_Q_LORA, MLA_HEADS * HEAD_PAD).T
    ukv = w_ukv.reshape(MLA_KV_LORA, MLA_HEADS, MLA_NOPE + MLA_V)
    uk = ukv[:, :, :MLA_NOPE].reshape(MLA_KV_LORA, MIX_A)
    uvt = ukv[:, :, MLA_NOPE:].reshape(MLA_KV_LORA, MIX_A).T
    return tuple(w.astype(BF16) for w in (wlat, wu, wg, uqt, uk, uvt))


def kernel(x, positions, a_norm_g, a_w_in, a_q_norm_g, a_w_uq, a_kv_norm_g, a_w_ukv, a_pool_w,
           a_pool_scale, a_w_out, r_norm_g, r_w_in, r_decay_fwd, r_decay_bwd, r_gn_g, r_w_out,
           final_norm_g):
    depth = a_norm_g.shape[0] + r_norm_g.shape[0]
    assert depth % 2 == 0, "the final norm is fused into the last (odd) layer's back kernel"
    ca, sa, cat, sat, cr, sr = _rope_tables(positions)
    fg = final_norm_g[None]
    for layer in range(depth):
        i = layer // 2
        if layer % 2 == 0:
            wlat, wu, wg, wuqt, wuk, wuvt = _even_weights(a_w_in[i], a_w_uq[i], a_w_ukv[i])
            g = a_norm_g[i][None]
            qt, k, vt = _even_front(x, g, wlat, a_q_norm_g[i][None], wuqt, a_kv_norm_g[i][None],
                                    wuk, wuvt, ca, sa, cat, sat)
            a = _attention(qt, k, vt)
            x = _even_back(x, g, wu, wg, a, a_pool_w[i].astype(BF16), a_pool_scale[i][None],
                           a_w_out[i].astype(BF16))
        else:
            w = r_w_in[i]
            g = r_norm_g[i][None]
            q, k, v = _odd_front(x, g, w[:, :2 * RET_QK + RET_V].astype(BF16), cr, sr)
            dec_f = jnp.broadcast_to(r_decay_fwd[i][:, None, None], (RET_HEADS, 1, LANES))
            dec_b = jnp.broadcast_to(r_decay_bwd[i][:, None, None], (RET_HEADS, 1, LANES))
            o = _retention(q, k, v, dec_f, dec_b, r_gn_g[i][None])
            x = _odd_back(x, g, w[:, 2 * RET_QK + RET_V:].astype(BF16), o, r_w_out[i].astype(BF16),
                          fg, final_norm=(layer == depth - 1))
    return x
```

```python
import functools

import jax
import jax.numpy as jnp
from jax import lax
from jax.experimental import pallas as pl
from jax.experimental.pallas import tpu as pltpu

ROPE_BASE = 10000.0
NORM_EPS = 1e-6
LOG2E = 1.4426950408889634

D_MODEL = 1024
MLA_HEADS = 8
MLA_NOPE = 128
MLA_ROPE = 64
MLA_V = 128
MLA_Q_LORA = 384
MLA_KV_LORA = 128
POOL_WINDOWS = (2, 4, 8, 16)
POOL_DIM = 256
MIX_A = MLA_HEADS * MLA_V
MIX_B = len(POOL_WINDOWS) * POOL_DIM
RET_HEADS = 4
RET_DK = 256
RET_DV = 512
RET_QK = RET_HEADS * RET_DK
RET_V = RET_HEADS * RET_DV

LANES = 128
SUBLANES = 8
HEAD_PAD = 2 * LANES
POOL_HALO = SUBLANES
VMEM_LIMIT = 56 * 1024 * 1024

ROW_TILE = 1024
SUB_TILE = 512
ATTN_KV_CHUNK = 256
ATTN_HEADS_PER_STEP = 4
ATTN_GROUP = 2
RET_CHUNK = 256
RET_HEADS_PER_STEP = 2

BF16 = jnp.bfloat16
F32 = jnp.float32


def _params(*sem):
    return pltpu.CompilerParams(dimension_semantics=sem, vmem_limit_bytes=VMEM_LIMIT)


def _const_spec(shape):
    nd = len(shape)
    return pl.BlockSpec(shape, lambda *_: (0,) * nd, pipeline_mode=pl.Buffered(1))


def _sub_tiles(rows):
    sub = min(rows, SUB_TILE)
    return [slice(r, r + sub) for r in range(0, rows, sub)]


def _rms(x, g):
    return x * lax.rsqrt(jnp.mean(x * x, axis=-1, keepdims=True) + NORM_EPS) * g


def _silu(x):
    return x / (1.0 + jnp.exp(-x))


def _dot(a, b):
    return jnp.dot(a, b, preferred_element_type=F32)


def _dot_nt(a, b):
    return lax.dot_general(a, b, (((1,), (1,)), ((), ())), preferred_element_type=F32)


def _dot_tn(a, b):
    return lax.dot_general(a, b, (((0,), (0,)), ((), ())), preferred_element_type=F32)


def _split3(a):
    hi = a.astype(BF16)
    r1 = a - hi.astype(F32)
    mid = r1.astype(BF16)
    lo = (r1 - mid.astype(F32)).astype(BF16)
    return jnp.concatenate([hi, mid, lo], axis=1)


def _rope_tab_kernel(pos_ref, inv_r_ref, selc_ref, sels_ref, selt_ref,
                     ca_ref, sa_ref, cat_ref, sat_ref, cr_ref, sr_ref):
    tt = pos_ref.shape[1]
    sub = min(tt, 256)
    for r in range(0, tt, sub):
        rows = slice(r, r + sub)
        ang = pos_ref[0, rows, :].astype(F32) * inv_r_ref[...]
        c = jnp.cos(ang)
        s = jnp.sin(ang)
        cr_ref[0, rows, :] = c
        sr_ref[0, rows, :] = s
        c3 = _split3(c)
        s3 = _split3(s)
        ca_ref[0, rows, :] = _dot(c3, selc_ref[...])
        sa_ref[0, rows, :] = _dot(s3, sels_ref[...])
        cat_ref[0, :, rows] = _dot_nt(selt_ref[...], c3)
        sat_ref[0, :, rows] = _dot_nt(selt_ref[...], s3)


def _rope_tables(positions):
    B, T = positions.shape
    half_a = MLA_ROPE // 2
    stride = RET_DK // MLA_ROPE
    inv_r = 1.0 / (ROPE_BASE ** (jnp.arange(0, RET_DK, 2, dtype=F32) / RET_DK))
    assert inv_r.shape[0] == LANES and stride * half_a == LANES
    pick = (jnp.arange(LANES)[:, None] == stride * jnp.arange(half_a)[None, :]).astype(BF16)
    zeros = jnp.zeros((LANES, LANES - 2 * half_a), BF16)
    sel_c = jnp.concatenate([pick, pick, zeros], axis=1)
    sel_s = jnp.concatenate([-pick, pick, zeros], axis=1)
    tile3 = lambda m: jnp.concatenate([m, m, m], axis=0)
    tt = min(T, 1024)
    tab = jax.ShapeDtypeStruct((B, T, LANES), F32)
    tab_t = jax.ShapeDtypeStruct((B, half_a, T), F32)
    blk = pl.BlockSpec((1, tt, LANES), lambda b, t: (b, t, 0))
    blk_t = pl.BlockSpec((1, half_a, tt), lambda b, t: (b, 0, t))
    return pl.pallas_call(
        _rope_tab_kernel,
        grid=(B, T // tt),
        in_specs=[pl.BlockSpec((1, tt, 1), lambda b, t: (b, t, 0)),
                  _const_spec((1, LANES)), _const_spec((3 * LANES, LANES)),
                  _const_spec((3 * LANES, LANES)), _const_spec((half_a, 3 * LANES))],
        out_specs=[blk, blk, blk_t, blk_t, blk, blk],
        out_shape=[tab, tab, tab_t, tab_t, tab, tab],
        compiler_params=_params("parallel", "parallel"),
        name="rope_tables",
    )(positions.reshape(B, T, 1), inv_r[None], tile3(sel_c), tile3(sel_s), tile3(pick).T)


def _rope_pad(v, c, s):
    q = LANES // 4
    return v * c + (pltpu.roll(v, 3 * q, 1) + pltpu.roll(v, q, 1)) * s


def _even_front_kernel(x_ref, g_ref, wlat_ref, gq_ref, wuqt_ref, gkv_ref, wuk_ref, wuvt_ref,
                       ca_ref, sa_ref, cat_ref, sat_ref, qt_ref, kn_ref, kr_ref, vt_ref):
    q_scale = (MLA_NOPE + MLA_ROPE) ** -0.5 * LOG2E
    half = MLA_ROPE // 2
    for j, rows in enumerate(_sub_tiles(x_ref.shape[1])):
        h = _rms(x_ref[0, rows, :], g_ref[...]).astype(BF16)
        lat = _dot(h, wlat_ref[...])
        cq = _rms(lat[:, :MLA_Q_LORA], gq_ref[...]).astype(BF16)
        qt = _dot_nt(wuqt_ref[...], cq)
        ckv = _rms(lat[:, MLA_Q_LORA:MLA_Q_LORA + MLA_KV_LORA], gkv_ref[...]).astype(BF16)
        kn_ref[0, rows, :] = _dot(ckv, wuk_ref[...]).astype(BF16)
        vt_ref[0, :, rows] = _dot_nt(wuvt_ref[...], ckv).astype(BF16)
        kr_ref[0, rows, :] = _rope_pad(lat[:, MLA_Q_LORA + MLA_KV_LORA:], ca_ref[0, rows, :],
                                       sa_ref[0, rows, :]).astype(BF16)
        ct = cat_ref[0, :, rows]
        st = sat_ref[0, :, rows]
        for hd in range(MLA_HEADS):
            lo = hd * HEAD_PAD
            r1 = lo + MLA_NOPE
            r2 = r1 + half
            r3 = r2 + half
            x1 = qt[r1:r2]
            x2 = qt[r2:r3]
            qt_ref[0, j, lo:r1, :] = (qt[lo:r1] * q_scale).astype(BF16)
            qt_ref[0, j, r1:r2, :] = ((x1 * ct - x2 * st) * q_scale).astype(BF16)
            qt_ref[0, j, r2:r3, :] = ((x2 * ct + x1 * st) * q_scale).astype(BF16)
            qt_ref[0, j, r3:lo + HEAD_PAD, :] = jnp.zeros((lo + HEAD_PAD - r3, qt.shape[1]), BF16)


def _even_front(x, g, wlat, gq, wuqt, gkv, wuk, wuvt, ca, sa, cat, sat):
    B, T, D = x.shape
    tm = min(T, ROW_TILE)
    sub = min(tm, SUB_TILE)
    row = lambda w: pl.BlockSpec((1, tm, w), lambda b, t: (b, t, 0))
    col = lambda r: pl.BlockSpec((1, r, tm), lambda b, t: (b, 0, t))
    consts = [g, wlat, gq, wuqt, gkv, wuk, wuvt]
    return pl.pallas_call(
        _even_front_kernel,
        grid=(B, T // tm),
        in_specs=[row(D)] + [_const_spec(c.shape) for c in consts]
        + [row(LANES), row(LANES), col(MLA_ROPE // 2), col(MLA_ROPE // 2)],
        out_specs=[pl.BlockSpec((1, tm // sub, MLA_HEADS * HEAD_PAD, sub), lambda b, t: (b, t, 0, 0)),
                   row(MLA_HEADS * MLA_NOPE), row(LANES), col(MIX_A)],
        out_shape=[jax.ShapeDtypeStruct((B, T // sub, MLA_HEADS * HEAD_PAD, sub), BF16),
                   jax.ShapeDtypeStruct((B, T, MLA_HEADS * MLA_NOPE), BF16),
                   jax.ShapeDtypeStruct((B, T, LANES), BF16),
                   jax.ShapeDtypeStruct((B, MIX_A, T), BF16)],
        compiler_params=_params("parallel", "parallel"),
        name="even_front",
    )(x, *consts, ca, sa, cat, sat)


def _attn_kernel(qt_ref, kn_ref, kr_ref, vt_ref, o_ref, s_ref, m_ref):
    n_q, tq = qt_ref.shape[1], qt_ref.shape[3]
    grp_blocks, n_kc, kc = s_ref.shape[1], s_ref.shape[2], s_ref.shape[3]
    n_blk = n_q * (qt_ref.shape[2] // HEAD_PAD)
    n_grp = n_blk // grp_blocks
    sub = (kc // SUBLANES, SUBLANES, tq)

    def score_chunk(g, j, c, m8):
        hd, qb = divmod(g * grp_blocks + j, n_q)
        qt = qt_ref[0, qb, hd * HEAD_PAD:(hd + 1) * HEAD_PAD, :]
        keys = slice(c * kc, (c + 1) * kc)
        k = jnp.concatenate([kn_ref[0, keys, hd * MLA_NOPE:(hd + 1) * MLA_NOPE],
                             kr_ref[0, keys, :]], axis=1)
        s = _dot(k, qt)
        s_ref[g % 2, j, c] = s
        cm = jnp.max(s.reshape(sub), axis=0)
        return cm if m8 is None else jnp.maximum(m8, cm)

    def value_chunk(g, j, c, m, l8, acc):
        hd = (g * grp_blocks + j) // n_q
        p = jnp.exp2(s_ref[g % 2, j, c] - m)
        ps = jnp.sum(p.reshape(sub), axis=0)
        vt = vt_ref[0, hd * MLA_V:(hd + 1) * MLA_V, c * kc:(c + 1) * kc]
        pv = _dot(vt, p.astype(BF16))
        return (ps if l8 is None else l8 + ps), (pv if acc is None else acc + pv)

    def stage(k):
        run_v = k >= 1
        run_s = k < n_grp
        blocks = range(grp_blocks)
        if run_v:
            m = [jnp.max(m_ref[(k - 1) % 2, j], axis=0, keepdims=True) for j in blocks]
        m8 = [None] * grp_blocks
        l8 = [None] * grp_blocks
        acc = [None] * grp_blocks
        for c in range(n_kc):
            for j in blocks:
                if run_v:
                    l8[j], acc[j] = value_chunk(k - 1, j, c, m[j], l8[j], acc[j])
                if run_s:
                    m8[j] = score_chunk(k, j, c, m8[j])
        for j in blocks:
            if run_s:
                m_ref[k % 2, j] = m8[j]
            if run_v:
                hd, qb = divmod((k - 1) * grp_blocks + j, n_q)
                l = jnp.sum(l8[j], axis=0, keepdims=True)
                o_ref[0, qb * tq:(qb + 1) * tq, hd * MLA_V:(hd + 1) * MLA_V] = (
                    (acc[j] / l).T.astype(BF16))

    one = jnp.minimum(pl.program_id(0) + 1, 1)
    for k in range(n_grp + 1):
        lax.fori_loop(0, one, lambda _, carry, k=k: (stage(k), carry)[1], 0)


def _attention(qt, kn, kr, vt):
    B, T, _ = kn.shape
    n_q, tq = qt.shape[1], qt.shape[3]
    kc = min(T, ATTN_KV_CHUNK)
    hs = ATTN_HEADS_PER_STEP
    return pl.pallas_call(
        _attn_kernel,
        grid=(B, MLA_HEADS // hs),
        in_specs=[pl.BlockSpec((1, n_q, hs * HEAD_PAD, tq), lambda b, h: (b, 0, h, 0)),
                  pl.BlockSpec((1, T, hs * MLA_NOPE), lambda b, h: (b, 0, h)),
                  pl.BlockSpec((1, T, LANES), lambda b, h: (b, 0, 0)),
                  pl.BlockSpec((1, hs * MLA_V, T), lambda b, h: (b, h, 0))],
        out_specs=pl.BlockSpec((1, T, hs * MLA_V), lambda b, h: (b, 0, h)),
        out_shape=jax.ShapeDtypeStruct((B, T, MIX_A), BF16),
        scratch_shapes=[pltpu.VMEM((2, ATTN_GROUP, T // kc, kc, tq), F32),
                        pltpu.VMEM((2, ATTN_GROUP, SUBLANES, tq), F32)],
        compiler_params=_params("parallel", "parallel"),
        name="mla_attention",
    )(qt, kn, kr, vt)


def _even_back_kernel(x_ref, xp_ref, xn_ref, g_ref, wu_ref, wg_ref, a_ref, pw_ref, ps_ref, wo_ref,
                      o_ref, *, seq_len):
    tm = x_ref.shape[1]
    g = g_ref[...]
    wu = wu_ref[...]
    for rows in _sub_tiles(tm):
        sub = rows.stop - rows.start
        t0 = pl.program_id(1) * tm + rows.start
        x = x_ref[0, rows, :]
        h = _rms(x, g).astype(BF16)
        xp = xp_ref[0] if rows.start == 0 else x_ref[0, rows.start - POOL_HALO:rows.start, :]
        xn = xn_ref[0] if rows.stop == tm else x_ref[0, rows.stop:rows.stop + POOL_HALO, :]
        hp = _rms(xp, g).astype(BF16)
        hn = _rms(xn, g).astype(BF16)
        up = jnp.where(t0 > 0, _dot(hp, wu), 0.0)
        un = jnp.where(t0 + sub < seq_len, _dot(hn, wu), 0.0)
        u = _dot(h, wu)
        ue = jnp.concatenate([up, u, un], axis=0)
        ext = sub + 2 * POOL_HALO
        t = (t0 + lax.broadcasted_iota(jnp.int32, (sub, 1), 0))
        gate = _dot(h, wg_ref[...])
        sg = _silu(gate)
        ya = (a_ref[0, rows, :].astype(F32) * sg[:, :MIX_A]).astype(BF16)
        y = _dot(ya, wo_ref[:MIX_A, :])
        for gi, w in enumerate(POOL_WINDOWS):
            left = w // 2
            right = w - 1 - left
            cols = slice(gi * POOL_DIM, (gi + 1) * POOL_DIM)
            run = ue[:, cols]
            n = 1
            while n < left:
                run = run + pltpu.roll(run, ext - n, 0)
                n *= 2
            acc = (run + pltpu.roll(run, left, 0))[POOL_HALO:POOL_HALO + sub]
            cnt = (jnp.minimum(t + right, seq_len - 1) - jnp.maximum(t - left, 0) + 1).astype(F32)
            d = (acc / cnt - u[:, cols]).astype(BF16)
            bg = _dot(d, pw_ref[gi]) * ps_ref[:, cols]
            yb = (bg * sg[:, MIX_A + gi * POOL_DIM:MIX_A + (gi + 1) * POOL_DIM]).astype(BF16)
            y = y + _dot(yb, wo_ref[MIX_A + gi * POOL_DIM:MIX_A + (gi + 1) * POOL_DIM, :])
        o_ref[0, rows, :] = x + y


def _even_back(x, g, wu, wg, a, pw, ps, wo):
    B, T, D = x.shape
    tm = min(T, ROW_TILE)
    nb = tm // POOL_HALO
    last = T // POOL_HALO - 1
    row = lambda w: pl.BlockSpec((1, tm, w), lambda b, t: (b, t, 0))
    return pl.pallas_call(
        functools.partial(_even_back_kernel, seq_len=T),
        grid=(B, T // tm),
        in_specs=[row(D),
                  pl.BlockSpec((1, POOL_HALO, D), lambda b, t: (b, jnp.maximum(t * nb - 1, 0), 0)),
                  pl.BlockSpec((1, POOL_HALO, D), lambda b, t: (b, jnp.minimum((t + 1) * nb, last), 0)),
                  _const_spec(g.shape), _const_spec(wu.shape), _const_spec(wg.shape), row(MIX_A),
                  _const_spec(pw.shape), _const_spec(ps.shape), _const_spec(wo.shape)],
        out_specs=row(D),
        out_shape=jax.ShapeDtypeStruct((B, T, D), F32),
        compiler_params=_params("parallel", "parallel"),
        name="even_back",
    )(x, x, x, g, wu, wg, a, pw, ps, wo)


def _odd_front_kernel(x_ref, g_ref, w_ref, c_ref, s_ref, q_ref, k_ref, v_ref):
    half = RET_DK // 2
    k_scale = RET_DK ** -0.5
    for rows in _sub_tiles(x_ref.shape[1]):
        h = _rms(x_ref[0, rows, :], g_ref[...]).astype(BF16)
        qkv = _dot(h, w_ref[...])
        c = c_ref[0, rows, :]
        s = s_ref[0, rows, :]
        for hd in range(RET_HEADS):
            for base, ref, scale in ((0, q_ref, None), (RET_QK, k_ref, k_scale)):
                lo = base + hd * RET_DK
                x1 = qkv[:, lo:lo + half]
                x2 = qkv[:, lo + half:lo + RET_DK]
                o1 = x1 * c - x2 * s
                o2 = x2 * c + x1 * s
                if scale is not None:
                    o1 = o1 * scale
                    o2 = o2 * scale
                ref[0, rows, hd * RET_DK:hd * RET_DK + half] = o1.astype(BF16)
                ref[0, rows, hd * RET_DK + half:(hd + 1) * RET_DK] = o2.astype(BF16)
        v_ref[0, rows, :] = qkv[:, 2 * RET_QK:].astype(BF16)


def _odd_front(x, g, w, c, s):
    B, T, D = x.shape
    tm = min(T, ROW_TILE)
    row = lambda wd: pl.BlockSpec((1, tm, wd), lambda b, t: (b, t, 0))
    qk = jax.ShapeDtypeStruct((B, T, RET_QK), BF16)
    return pl.pallas_call(
        _odd_front_kernel,
        grid=(B, T // tm),
        in_specs=[row(D), _const_spec(g.shape), _const_spec(w.shape), row(LANES), row(LANES)],
        out_specs=[row(RET_QK), row(RET_QK), row(RET_V)],
        out_shape=[qk, qk, jax.ShapeDtypeStruct((B, T, RET_V), BF16)],
        compiler_params=_params("parallel", "parallel"),
        name="odd_front",
    )(x, g, w, c, s)


def _log_sigmoid(x):
    return jnp.minimum(x, 0.0) - jnp.log1p(jnp.exp(-jnp.abs(x)))


def _retention_kernel(q_ref, k_ref, v_ref, df_ref, db_ref, gn_ref, o_ref, sf_ref, acc_ref):
    T = q_ref.shape[1]
    C = min(T, RET_CHUNK)
    n_chunks = T // C
    heads = range(q_ref.shape[2] // RET_DK)
    ri = lax.broadcasted_iota(jnp.int32, (C, 1), 0).astype(F32)
    diff = (lax.broadcasted_iota(jnp.int32, (C, C), 0)
            - lax.broadcasted_iota(jnp.int32, (C, C), 1)).astype(F32)

    def decays(hd):
        lf1 = _log_sigmoid(df_ref[hd])[:, :1]
        lb1 = _log_sigmoid(db_ref[hd])[:, :1]
        return dict(
            dmat=jnp.exp(jnp.where(diff >= 0, diff * lf1, -diff * lb1)),
            xi_f=jnp.exp((ri + 1.0) * lf1),
            xi_b=jnp.exp((C - ri) * lb1),
            zeta_f=jnp.exp((C - 1.0 - ri) * lf1),
            zeta_b=jnp.exp(ri * lb1),
            cd_f=jnp.exp(C * lf1), cd_b=jnp.exp(C * lb1))

    dec = [decays(hd) for hd in heads]

    def state_update(hd, i, zeta, cd, first):
        rows = slice(i * C, (i + 1) * C)
        kz = (k_ref[0, rows, hd * RET_DK:(hd + 1) * RET_DK].astype(F32) * zeta).astype(BF16)
        upd = _dot_tn(kz, v_ref[0, rows, hd * RET_DV:(hd + 1) * RET_DV])
        acc_ref[hd] = upd if first else acc_ref[hd] * cd + upd

    for i in range(n_chunks - 1):
        for hd in heads:
            state_update(hd, i, dec[hd]["zeta_f"], dec[hd]["cd_f"], first=(i == 0))
            sf_ref[hd, i + 1] = acc_ref[hd].astype(BF16)

    for i in reversed(range(n_chunks)):
        rows = slice(i * C, (i + 1) * C)
        for hd in heads:
            d = dec[hd]
            vcols = slice(hd * RET_DV, (hd + 1) * RET_DV)
            q = q_ref[0, rows, hd * RET_DK:(hd + 1) * RET_DK]
            qf = q.astype(F32)
            s = (_dot_nt(q, k_ref[0, rows, hd * RET_DK:(hd + 1) * RET_DK]) * d["dmat"]).astype(BF16)
            o = _dot(s, v_ref[0, rows, vcols])
            if i > 0:
                o = o + _dot((qf * d["xi_f"]).astype(BF16), sf_ref[hd, i])
            if i < n_chunks - 1:
                o = o + _dot((qf * d["xi_b"]).astype(BF16), acc_ref[hd].astype(BF16))
            mu = jnp.mean(o, axis=-1, keepdims=True)
            oc = o - mu
            var = jnp.mean(oc * oc, axis=-1, keepdims=True)
            o_ref[0, rows, vcols] = (oc * lax.rsqrt(var + NORM_EPS) * gn_ref[:, vcols]).astype(BF16)
            if i > 0:
                state_update(hd, i, d["zeta_b"], d["cd_b"], first=(i == n_chunks - 1))


def _retention(q, k, v, dec_f, dec_b, gn_g):
    B, T, _ = q.shape
    C = min(T, RET_CHUNK)
    hs = RET_HEADS_PER_STEP
    head = lambda w: pl.BlockSpec((1, T, hs * w), lambda b, h: (b, 0, h))
    dec = pl.BlockSpec((hs, 1, LANES), lambda b, h: (h, 0, 0))
    return pl.pallas_call(
        _retention_kernel,
        grid=(B, RET_HEADS // hs),
        in_specs=[head(RET_DK), head(RET_DK), head(RET_DV), dec, dec,
                  pl.BlockSpec((1, hs * RET_DV), lambda b, h: (0, h))],
        out_specs=head(RET_DV),
        out_shape=jax.ShapeDtypeStruct((B, T, RET_V), BF16),
        scratch_shapes=[pltpu.VMEM((hs, T // C, RET_DK, RET_DV), BF16),
                        pltpu.VMEM((hs, RET_DK, RET_DV), F32)],
        compiler_params=_params("parallel", "parallel"),
        name="retention",
    )(q, k, v, dec_f, dec_b, gn_g)


def _odd_back_kernel(x_ref, g_ref, wg_ref, a_ref, wo_ref, fg_ref, o_ref, *, final_norm):
    for rows in _sub_tiles(x_ref.shape[1]):
        x = x_ref[0, rows, :]
        h = _rms(x, g_ref[...]).astype(BF16)
        gate = _dot(h, wg_ref[...])
        y = (a_ref[0, rows, :].astype(F32) * _silu(gate)).astype(BF16)
        out = x + _dot(y, wo_ref[...])
        if final_norm:
            out = _rms(out, fg_ref[...])
        o_ref[0, rows, :] = out


def _odd_back(x, g, wg, a, wo, fg, final_norm):
    B, T, D = x.shape
    tm = min(T, ROW_TILE)
    row = lambda w: pl.BlockSpec((1, tm, w), lambda b, t: (b, t, 0))
    return pl.pallas_call(
        functools.partial(_odd_back_kernel, final_norm=final_norm),
        grid=(B, T // tm),
        in_specs=[row(D), _const_spec(g.shape), _const_spec(wg.shape), row(RET_V),
                  _const_spec(wo.shape), _const_spec(fg.shape)],
        out_specs=row(D),
        out_shape=jax.ShapeDtypeStruct((B, T, D), F32),
        compiler_params=_params("parallel", "parallel"),
        name="odd_back",
    )(x, g, wg, a, wo, fg)


def _even_weights(w_in, w_uq, w_ukv):
    n_lat = MLA_Q_LORA + MLA_KV_LORA
    wlat = jnp.concatenate(
        [w_in[:, :n_lat + MLA_ROPE], jnp.zeros((D_MODEL, LANES - MLA_ROPE), w_in.dtype)], axis=1)
    wu = w_in[:, n_lat + MLA_ROPE:n_lat + MLA_ROPE + MIX_B]
    wg = w_in[:, n_lat + MLA_ROPE + MIX_B:]
    uq = w_uq.reshape(MLA_Q_LORA, MLA_HEADS, MLA_NOPE + MLA_ROPE)
    uq = jnp.pad(uq, ((0, 0), (0, 0), (0, HEAD_PAD - MLA_NOPE - MLA_ROPE)))
    uqt = uq.reshape(MLA_Q_LORA, MLA_HEADS * HEAD_PAD).T
    ukv = w_ukv.reshape(MLA_KV_LORA, MLA_HEADS, MLA_NOPE + MLA_V)
    uk = ukv[:, :, :MLA_NOPE].reshape(MLA_KV_LORA, MIX_A)
    uvt = ukv[:, :, MLA_NOPE:].reshape(MLA_KV_LORA, MIX_A).T
    return tuple(w.astype(BF16) for w in (wlat, wu, wg, uqt, uk, uvt))


def kernel(x, positions, a_norm_g, a_w_in, a_q_norm_g, a_w_uq, a_kv_norm_g, a_w_ukv, a_pool_w,
           a_pool_scale, a_w_out, r_norm_g, r_w_in, r_decay_fwd, r_decay_bwd, r_gn_g, r_w_out,
           final_norm_g):
    depth = a_norm_g.shape[0] + r_norm_g.shape[0]
    assert depth % 2 == 0, "the final norm is fused into the last (odd) layer's back kernel"
    ca, sa, cat, sat, cr, sr = _rope_tables(positions)
    fg = final_norm_g[None]
    for layer in range(depth):
        i = layer // 2
        if layer % 2 == 0:
            wlat, wu, wg, wuqt, wuk, wuvt = _even_weights(a_w_in[i], a_w_uq[i], a_w_ukv[i])
            g = a_norm_g[i][None]
            qt, kn, kr, vt = _even_front(x, g, wlat, a_q_norm_g[i][None], wuqt,
                                         a_kv_norm_g[i][None], wuk, wuvt, ca, sa, cat, sat)
            a = _attention(qt, kn, kr, vt)
            x = _even_back(x, g, wu, wg, a, a_pool_w[i].astype(BF16), a_pool_scale[i][None],
                           a_w_out[i].astype(BF16))
        else:
            w = r_w_in[i]
            g = r_norm_g[i][None]
            q, k, v = _odd_front(x, g, w[:, :2 * RET_QK + RET_V].astype(BF16), cr, sr)
            dec_f = jnp.broadcast_to(r_decay_fwd[i][:, None, None], (RET_HEADS, 1, LANES))
            dec_b = jnp.broadcast_to(r_decay_bwd[i][:, None, None], (RET_HEADS, 1, LANES))
            o = _retention(q, k, v, dec_f, dec_b, r_gn_g[i][None])
            x = _odd_back(x, g, w[:, 2 * RET_QK + RET_V:].astype(BF16), o, r_w_out[i].astype(BF16),
                          fg, final_norm=(layer == depth - 1))
    return x
```

```python
import functools

import jax
import jax.numpy as jnp
from jax import lax
from jax.experimental import pallas as pl
from jax.experimental.pallas import tpu as pltpu

ROPE_BASE = 10000.0
NORM_EPS = 1e-6
LOG2E = 1.4426950408889634

D_MODEL = 1024
MLA_HEADS = 8
MLA_NOPE = 128
MLA_ROPE = 64
MLA_V = 128
MLA_Q_LORA = 384
MLA_KV_LORA = 128
POOL_WINDOWS = (2, 4, 8, 16)
POOL_DIM = 256
MIX_A = MLA_HEADS * MLA_V
MIX_B = len(POOL_WINDOWS) * POOL_DIM
RET_HEADS = 4
RET_DK = 256
RET_DV = 512
RET_QK = RET_HEADS * RET_DK
RET_V = RET_HEADS * RET_DV

LANES = 128
SUBLANES = 8
HEAD_PAD = 2 * LANES
POOL_HALO = SUBLANES
VMEM_LIMIT = 56 * 1024 * 1024

ROW_TILE = 1024
SUB_TILE = 512
ATTN_KV_CHUNK = 256
ATTN_HEADS_PER_STEP = 4
ATTN_GROUP = 2
RET_CHUNK = 256
RET_HEADS_PER_STEP = 2

BF16 = jnp.bfloat16
F32 = jnp.float32


def _params(*sem, **kw):
    return pltpu.CompilerParams(dimension_semantics=sem, vmem_limit_bytes=VMEM_LIMIT, **kw)


def _const_spec(shape):
    nd = len(shape)
    return pl.BlockSpec(shape, lambda *_: (0,) * nd, pipeline_mode=pl.Buffered(1))


def _layer_spec(block, layer, col_block=0):
    idx = (layer,) + (0,) * (len(block) - 1) + (col_block,)
    return pl.BlockSpec((None,) + tuple(block), lambda *_: idx, pipeline_mode=pl.Buffered(1))


def _sub_tiles(rows):
    sub = min(rows, SUB_TILE)
    return [slice(r, r + sub) for r in range(0, rows, sub)]


def _rms(x, g):
    return x * lax.rsqrt(jnp.mean(x * x, axis=-1, keepdims=True) + NORM_EPS) * g


def _silu(x):
    return x / (1.0 + jnp.exp(-x))


def _dot(a, b):
    return jnp.dot(a, b, preferred_element_type=F32)


def _dot_nt(a, b):
    return lax.dot_general(a, b, (((1,), (1,)), ((), ())), preferred_element_type=F32)


def _dot_tn(a, b):
    return lax.dot_general(a, b, (((0,), (0,)), ((), ())), preferred_element_type=F32)


def _split3(a):
    hi = a.astype(BF16)
    r1 = a - hi.astype(F32)
    mid = r1.astype(BF16)
    lo = (r1 - mid.astype(F32)).astype(BF16)
    return jnp.concatenate([hi, mid, lo], axis=1)


def _rope_tab_kernel(pos_ref, inv_r_ref, selc_ref, sels_ref, selt_ref,
                     ca_ref, sa_ref, cat_ref, sat_ref, cr_ref, sr_ref):
    tt = pos_ref.shape[2]
    sub = min(tt, 256)
    for r in range(0, tt, sub):
        rows = slice(r, r + sub)
        pos = jnp.broadcast_to(pos_ref[0, :, rows].astype(F32), (SUBLANES, sub)).T[:, :1]
        ang = pos * inv_r_ref[...]
        c = jnp.cos(ang)
        s = jnp.sin(ang)
        cr_ref[0, rows, :] = c
        sr_ref[0, rows, :] = s
        c3 = _split3(c)
        s3 = _split3(s)
        ca_ref[0, rows, :] = _dot(c3, selc_ref[...])
        sa_ref[0, rows, :] = _dot(s3, sels_ref[...])
        cat_ref[0, :, rows] = _dot_nt(selt_ref[...], c3)
        sat_ref[0, :, rows] = _dot_nt(selt_ref[...], s3)


def _rope_tables(positions):
    B, T = positions.shape
    half_a = MLA_ROPE // 2
    stride = RET_DK // MLA_ROPE
    inv_r = 1.0 / (ROPE_BASE ** (jnp.arange(0, RET_DK, 2, dtype=F32) / RET_DK))
    assert inv_r.shape[0] == LANES and stride * half_a == LANES
    pick = (jnp.arange(LANES)[:, None] == stride * jnp.arange(half_a)[None, :]).astype(BF16)
    zeros = jnp.zeros((LANES, LANES - 2 * half_a), BF16)
    sel_c = jnp.concatenate([pick, pick, zeros], axis=1)
    sel_s = jnp.concatenate([-pick, pick, zeros], axis=1)
    tile3 = lambda m: jnp.concatenate([m, m, m], axis=0)
    tt = min(T, 1024)
    tab = jax.ShapeDtypeStruct((B, T, LANES), F32)
    tab_t = jax.ShapeDtypeStruct((B, half_a, T), F32)
    blk = pl.BlockSpec((1, tt, LANES), lambda b, t: (b, t, 0))
    blk_t = pl.BlockSpec((1, half_a, tt), lambda b, t: (b, 0, t))
    return pl.pallas_call(
        _rope_tab_kernel,
        grid=(B, T // tt),
        in_specs=[pl.BlockSpec((1, 1, tt), lambda b, t: (b, 0, t)),
                  _const_spec((1, LANES)), _const_spec((3 * LANES, LANES)),
                  _const_spec((3 * LANES, LANES)), _const_spec((half_a, 3 * LANES))],
        out_specs=[blk, blk, blk_t, blk_t, blk, blk],
        out_shape=[tab, tab, tab_t, tab_t, tab, tab],
        compiler_params=_params("parallel", "parallel"),
        name="rope_tables",
    )(positions.reshape(B, 1, T), inv_r[None], tile3(sel_c), tile3(sel_s), tile3(pick).T)


def _rope_pad(v, c, s):
    q = LANES // 4
    return v * c + (pltpu.roll(v, 3 * q, 1) + pltpu.roll(v, q, 1)) * s


def _even_front_kernel(x_ref, g_ref, wlat_ref, gq_ref, wuqt_ref, gkv_ref, wuk_ref, wuvt_ref,
                       ca_ref, sa_ref, cat_ref, sat_ref, qt_ref, kn_ref, kr_ref, vt_ref):
    q_scale = (MLA_NOPE + MLA_ROPE) ** -0.5 * LOG2E
    half = MLA_ROPE // 2
    for j, rows in enumerate(_sub_tiles(x_ref.shape[1])):
        h = _rms(x_ref[0, rows, :], g_ref[...]).astype(BF16)
        lat = _dot(h, wlat_ref[...])
        cq = _rms(lat[:, :MLA_Q_LORA], gq_ref[...]).astype(BF16)
        qt = _dot_nt(wuqt_ref[...], cq)
        ckv = _rms(lat[:, MLA_Q_LORA:MLA_Q_LORA + MLA_KV_LORA], gkv_ref[...]).astype(BF16)
        kn_ref[0, rows, :] = _dot(ckv, wuk_ref[...]).astype(BF16)
        vt_ref[0, :, rows] = _dot_nt(wuvt_ref[...], ckv).astype(BF16)
        kr_ref[0, rows, :] = _rope_pad(lat[:, MLA_Q_LORA + MLA_KV_LORA:], ca_ref[0, rows, :],
                                       sa_ref[0, rows, :]).astype(BF16)
        ct = cat_ref[0, :, rows]
        st = sat_ref[0, :, rows]
        for hd in range(MLA_HEADS):
            lo = hd * HEAD_PAD
            r1 = lo + MLA_NOPE
            r2 = r1 + half
            r3 = r2 + half
            x1 = qt[r1:r2]
            x2 = qt[r2:r3]
            qt_ref[0, j, lo:r1, :] = (qt[lo:r1] * q_scale).astype(BF16)
            qt_ref[0, j, r1:r2, :] = ((x1 * ct - x2 * st) * q_scale).astype(BF16)
            qt_ref[0, j, r2:r3, :] = ((x2 * ct + x1 * st) * q_scale).astype(BF16)
            qt_ref[0, j, r3:lo + HEAD_PAD, :] = jnp.zeros((lo + HEAD_PAD - r3, qt.shape[1]), BF16)


def _even_front(x, g, wlat, gq, wuqt, gkv, wuk, wuvt, ca, sa, cat, sat):
    B, T, D = x.shape
    tm = min(T, ROW_TILE)
    sub = min(tm, SUB_TILE)
    row = lambda w: pl.BlockSpec((1, tm, w), lambda b, t: (b, t, 0))
    col = lambda r: pl.BlockSpec((1, r, tm), lambda b, t: (b, 0, t))
    consts = [g, wlat, gq, wuqt, gkv, wuk, wuvt]
    return pl.pallas_call(
        _even_front_kernel,
        grid=(B, T // tm),
        in_specs=[row(D)] + [_const_spec(c.shape) for c in consts]
        + [row(LANES), row(LANES), col(MLA_ROPE // 2), col(MLA_ROPE // 2)],
        out_specs=[pl.BlockSpec((1, tm // sub, MLA_HEADS * HEAD_PAD, sub), lambda b, t: (b, t, 0, 0)),
                   row(MLA_HEADS * MLA_NOPE), row(LANES), col(MIX_A)],
        out_shape=[jax.ShapeDtypeStruct((B, T // sub, MLA_HEADS * HEAD_PAD, sub), BF16),
                   jax.ShapeDtypeStruct((B, T, MLA_HEADS * MLA_NOPE), BF16),
                   jax.ShapeDtypeStruct((B, T, LANES), BF16),
                   jax.ShapeDtypeStruct((B, MIX_A, T), BF16)],
        compiler_params=_params("parallel", "parallel"),
        name="even_front",
    )(x, *consts, ca, sa, cat, sat)


def _attn_kernel(qt_ref, kn_ref, kr_ref, vt_ref, o_ref, s_ref, m_ref):
    n_q, tq = qt_ref.shape[1], qt_ref.shape[3]
    grp_blocks, n_kc, kc = s_ref.shape[1], s_ref.shape[2], s_ref.shape[3]
    n_blk = n_q * (qt_ref.shape[2] // HEAD_PAD)
    n_grp = n_blk // grp_blocks
    sub = (kc // SUBLANES, SUBLANES, tq)

    def score_chunk(g, j, c, m8):
        hd, qb = divmod(g * grp_blocks + j, n_q)
        qt = qt_ref[0, qb, hd * HEAD_PAD:(hd + 1) * HEAD_PAD, :]
        keys = slice(c * kc, (c + 1) * kc)
        k = jnp.concatenate([kn_ref[0, keys, hd * MLA_NOPE:(hd + 1) * MLA_NOPE],
                             kr_ref[0, keys, :]], axis=1)
        s = _dot(k, qt)
        s_ref[g % 2, j, c] = s
        cm = jnp.max(s.reshape(sub), axis=0)
        return cm if m8 is None else jnp.maximum(m8, cm)

    def value_chunk(g, j, c, m, l8, acc):
        hd = (g * grp_blocks + j) // n_q
        p = jnp.exp2(s_ref[g % 2, j, c] - m)
        ps = jnp.sum(p.reshape(sub), axis=0)
        vt = vt_ref[0, hd * MLA_V:(hd + 1) * MLA_V, c * kc:(c + 1) * kc]
        pv = _dot(vt, p.astype(BF16))
        return (ps if l8 is None else l8 + ps), (pv if acc is None else acc + pv)

    def stage(k):
        run_v = k >= 1
        run_s = k < n_grp
        blocks = range(grp_blocks)
        if run_v:
            m = [jnp.max(m_ref[(k - 1) % 2, j], axis=0, keepdims=True) for j in blocks]
        m8 = [None] * grp_blocks
        l8 = [None] * grp_blocks
        acc = [None] * grp_blocks
        for c in range(n_kc):
            for j in blocks:
                if run_v:
                    l8[j], acc[j] = value_chunk(k - 1, j, c, m[j], l8[j], acc[j])
                if run_s:
                    m8[j] = score_chunk(k, j, c, m8[j])
        for j in blocks:
            if run_s:
                m_ref[k % 2, j] = m8[j]
            if run_v:
                hd, qb = divmod((k - 1) * grp_blocks + j, n_q)
                l = jnp.sum(l8[j], axis=0, keepdims=True)
                o_ref[0, qb * tq:(qb + 1) * tq, hd * MLA_V:(hd + 1) * MLA_V] = (
                    (acc[j] / l).T.astype(BF16))

    one = jnp.minimum(pl.program_id(0) + 1, 1)
    for k in range(n_grp + 1):
        lax.fori_loop(0, one, lambda _, carry, k=k: (stage(k), carry)[1], 0)


def _attention(qt, kn, kr, vt):
    B, T, _ = kn.shape
    n_q, tq = qt.shape[1], qt.shape[3]
    kc = min(T, ATTN_KV_CHUNK)
    hs = ATTN_HEADS_PER_STEP
    return pl.pallas_call(
        _attn_kernel,
        grid=(B, MLA_HEADS // hs),
        in_specs=[pl.BlockSpec((1, n_q, hs * HEAD_PAD, tq), lambda b, h: (b, 0, h, 0)),
                  pl.BlockSpec((1, T, hs * MLA_NOPE), lambda b, h: (b, 0, h)),
                  pl.BlockSpec((1, T, LANES), lambda b, h: (b, 0, 0)),
                  pl.BlockSpec((1, hs * MLA_V, T), lambda b, h: (b, h, 0))],
        out_specs=pl.BlockSpec((1, T, hs * MLA_V), lambda b, h: (b, 0, h)),
        out_shape=jax.ShapeDtypeStruct((B, T, MIX_A), BF16),
        scratch_shapes=[pltpu.VMEM((2, ATTN_GROUP, T // kc, kc, tq), F32),
                        pltpu.VMEM((2, ATTN_GROUP, SUBLANES, tq), F32)],
        compiler_params=_params("parallel", "parallel"),
        name="mla_attention",
    )(qt, kn, kr, vt)


def _even_back_kernel(x_ref, xp_ref, xn_ref, g_ref, wu_ref, wg_ref, a_ref, pw_ref, ps_ref, wo_ref,
                      o_ref, *, seq_len):
    tm = x_ref.shape[1]
    g = g_ref[...]
    wu = wu_ref[...]
    for rows in _sub_tiles(tm):
        sub = rows.stop - rows.start
        t0 = pl.program_id(1) * tm + rows.start
        x = x_ref[0, rows, :]
        h = _rms(x, g).astype(BF16)
        xp = xp_ref[0] if rows.start == 0 else x_ref[0, rows.start - POOL_HALO:rows.start, :]
        xn = xn_ref[0] if rows.stop == tm else x_ref[0, rows.stop:rows.stop + POOL_HALO, :]
        hp = _rms(xp, g).astype(BF16)
        hn = _rms(xn, g).astype(BF16)
        up = jnp.where(t0 > 0, _dot(hp, wu), 0.0)
        un = jnp.where(t0 + sub < seq_len, _dot(hn, wu), 0.0)
        u = _dot(h, wu)
        ue = jnp.concatenate([up, u, un], axis=0)
        ext = sub + 2 * POOL_HALO
        t = (t0 + lax.broadcasted_iota(jnp.int32, (sub, 1), 0))
        gate = _dot(h, wg_ref[...])
        sg = _silu(gate)
        ya = (a_ref[0, rows, :].astype(F32) * sg[:, :MIX_A]).astype(BF16)
        y = _dot(ya, wo_ref[:MIX_A, :])
        for gi, w in enumerate(POOL_WINDOWS):
            left = w // 2
            right = w - 1 - left
            cols = slice(gi * POOL_DIM, (gi + 1) * POOL_DIM)
            run = ue[:, cols]
            n = 1
            while n < left:
                run = run + pltpu.roll(run, ext - n, 0)
                n *= 2
            acc = (run + pltpu.roll(run, left, 0))[POOL_HALO:POOL_HALO + sub]
            cnt = (jnp.minimum(t + right, seq_len - 1) - jnp.maximum(t - left, 0) + 1).astype(F32)
            d = (acc / cnt - u[:, cols]).astype(BF16)
            bg = _dot(d, pw_ref[gi]) * ps_ref[:, cols]
            yb = (bg * sg[:, MIX_A + gi * POOL_DIM:MIX_A + (gi + 1) * POOL_DIM]).astype(BF16)
            y = y + _dot(yb, wo_ref[MIX_A + gi * POOL_DIM:MIX_A + (gi + 1) * POOL_DIM, :])
        o_ref[0, rows, :] = x + y


def _even_back(x, g, wu, wg, a, pw_all, ps, wo_all, layer):
    B, T, D = x.shape
    tm = min(T, ROW_TILE)
    nb = tm // POOL_HALO
    last = T // POOL_HALO - 1
    row = lambda w: pl.BlockSpec((1, tm, w), lambda b, t: (b, t, 0))
    return pl.pallas_call(
        functools.partial(_even_back_kernel, seq_len=T),
        grid=(B, T // tm),
        in_specs=[row(D),
                  pl.BlockSpec((1, POOL_HALO, D), lambda b, t: (b, jnp.maximum(t * nb - 1, 0), 0)),
                  pl.BlockSpec((1, POOL_HALO, D), lambda b, t: (b, jnp.minimum((t + 1) * nb, last), 0)),
                  _const_spec(g.shape), _const_spec(wu.shape), _const_spec(wg.shape), row(MIX_A),
                  _layer_spec(pw_all.shape[1:], layer), _const_spec(ps.shape),
                  _layer_spec(wo_all.shape[1:], layer)],
        out_specs=row(D),
        out_shape=jax.ShapeDtypeStruct((B, T, D), F32),
        compiler_params=_params("parallel", "parallel"),
        name="even_back",
    )(x, x, x, g, wu, wg, a, pw_all, ps, wo_all)


def _odd_front_kernel(x_ref, g_ref, w_ref, c_ref, s_ref, q_ref, k_ref, v_ref):
    half = RET_DK // 2
    k_scale = RET_DK ** -0.5
    for rows in _sub_tiles(x_ref.shape[1]):
        h = _rms(x_ref[0, rows, :], g_ref[...]).astype(BF16)
        qkv = _dot(h, w_ref[...])
        c = c_ref[0, rows, :]
        s = s_ref[0, rows, :]
        for hd in range(RET_HEADS):
            for base, ref, scale in ((0, q_ref, None), (RET_QK, k_ref, k_scale)):
                lo = base + hd * RET_DK
                x1 = qkv[:, lo:lo + half]
                x2 = qkv[:, lo + half:lo + RET_DK]
                o1 = x1 * c - x2 * s
                o2 = x2 * c + x1 * s
                if scale is not None:
                    o1 = o1 * scale
                    o2 = o2 * scale
                ref[0, rows, hd * RET_DK:hd * RET_DK + half] = o1.astype(BF16)
                ref[0, rows, hd * RET_DK + half:(hd + 1) * RET_DK] = o2.astype(BF16)
        v_ref[0, rows, :] = qkv[:, 2 * RET_QK:].astype(BF16)


def _odd_front(x, g, w_all, layer, c, s):
    B, T, D = x.shape
    tm = min(T, ROW_TILE)
    row = lambda wd: pl.BlockSpec((1, tm, wd), lambda b, t: (b, t, 0))
    qk = jax.ShapeDtypeStruct((B, T, RET_QK), BF16)
    return pl.pallas_call(
        _odd_front_kernel,
        grid=(B, T // tm),
        in_specs=[row(D), _const_spec(g.shape), _layer_spec((D, 2 * RET_QK + RET_V), layer),
                  row(LANES), row(LANES)],
        out_specs=[row(RET_QK), row(RET_QK), row(RET_V)],
        out_shape=[qk, qk, jax.ShapeDtypeStruct((B, T, RET_V), BF16)],
        compiler_params=_params("parallel", "parallel"),
        name="odd_front",
    )(x, g, w_all, c, s)


def _log_sigmoid(x):
    return jnp.minimum(x, 0.0) - jnp.log1p(jnp.exp(-jnp.abs(x)))


def _retention_kernel(q_ref, k_ref, v_ref, df_ref, db_ref, gn_ref, o_ref, sf_ref, acc_ref):
    T = q_ref.shape[1]
    C = min(T, RET_CHUNK)
    n_chunks = T // C
    heads = range(q_ref.shape[2] // RET_DK)
    ri = lax.broadcasted_iota(jnp.int32, (C, 1), 0).astype(F32)
    diff = (lax.broadcasted_iota(jnp.int32, (C, C), 0)
            - lax.broadcasted_iota(jnp.int32, (C, C), 1)).astype(F32)

    def decays(hd):
        lf1 = _log_sigmoid(df_ref[hd])[:, :1]
        lb1 = _log_sigmoid(db_ref[hd])[:, :1]
        return dict(
            dmat=jnp.exp(jnp.where(diff >= 0, diff * lf1, -diff * lb1)),
            xi_f=jnp.exp((ri + 1.0) * lf1),
            xi_b=jnp.exp((C - ri) * lb1),
            zeta_f=jnp.exp((C - 1.0 - ri) * lf1),
            zeta_b=jnp.exp(ri * lb1),
            cd_f=jnp.exp(C * lf1), cd_b=jnp.exp(C * lb1))

    dec = [decays(hd) for hd in heads]

    def state_update(hd, i, zeta, cd, first):
        rows = slice(i * C, (i + 1) * C)
        kz = (k_ref[0, rows, hd * RET_DK:(hd + 1) * RET_DK].astype(F32) * zeta).astype(BF16)
        upd = _dot_tn(kz, v_ref[0, rows, hd * RET_DV:(hd + 1) * RET_DV])
        acc_ref[hd] = upd if first else acc_ref[hd] * cd + upd

    for i in range(n_chunks - 1):
        for hd in heads:
            state_update(hd, i, dec[hd]["zeta_f"], dec[hd]["cd_f"], first=(i == 0))
            sf_ref[hd, i + 1] = acc_ref[hd].astype(BF16)

    for i in reversed(range(n_chunks)):
        rows = slice(i * C, (i + 1) * C)
        for hd in heads:
            d = dec[hd]
            vcols = slice(hd * RET_DV, (hd + 1) * RET_DV)
            q = q_ref[0, rows, hd * RET_DK:(hd + 1) * RET_DK]
            qf = q.astype(F32)
            s = (_dot_nt(q, k_ref[0, rows, hd * RET_DK:(hd + 1) * RET_DK]) * d["dmat"]).astype(BF16)
            o = _dot(s, v_ref[0, rows, vcols])
            if i > 0:
                o = o + _dot((qf * d["xi_f"]).astype(BF16), sf_ref[hd, i])
            if i < n_chunks - 1:
                o = o + _dot((qf * d["xi_b"]).astype(BF16), acc_ref[hd].astype(BF16))
            mu = jnp.mean(o, axis=-1, keepdims=True)
            oc = o - mu
            var = jnp.mean(oc * oc, axis=-1, keepdims=True)
            o_ref[0, rows, vcols] = (oc * lax.rsqrt(var + NORM_EPS) * gn_ref[:, vcols]).astype(BF16)
            if i > 0:
                state_update(hd, i, d["zeta_b"], d["cd_b"], first=(i == n_chunks - 1))


def _retention(q, k, v, dec_f, dec_b, gn_g):
    B, T, _ = q.shape
    C = min(T, RET_CHUNK)
    hs = RET_HEADS_PER_STEP
    head = lambda w: pl.BlockSpec((1, T, hs * w), lambda b, h: (b, 0, h))
    dec = pl.BlockSpec((hs, 1, LANES), lambda b, h: (h, 0, 0))
    return pl.pallas_call(
        _retention_kernel,
        grid=(B, RET_HEADS // hs),
        in_specs=[head(RET_DK), head(RET_DK), head(RET_DV), dec, dec,
                  pl.BlockSpec((1, hs * RET_DV), lambda b, h: (0, h))],
        out_specs=head(RET_DV),
        out_shape=jax.ShapeDtypeStruct((B, T, RET_V), BF16),
        scratch_shapes=[pltpu.VMEM((hs, T // C, RET_DK, RET_DV), BF16),
                        pltpu.VMEM((hs, RET_DK, RET_DV), F32)],
        compiler_params=_params("parallel", "parallel"),
        name="retention",
    )(q, k, v, dec_f, dec_b, gn_g)


def _odd_back_kernel(x_ref, g_ref, wg_ref, a_ref, wo_ref, fg_ref, o_ref, *, final_norm):
    for rows in _sub_tiles(x_ref.shape[1]):
        x = x_ref[0, rows, :]
        h = _rms(x, g_ref[...]).astype(BF16)
        gate = _dot(h, wg_ref[...])
        y = (a_ref[0, rows, :].astype(F32) * _silu(gate)).astype(BF16)
        out = x + _dot(y, wo_ref[...])
        if final_norm:
            out = _rms(out, fg_ref[...])
        o_ref[0, rows, :] = out


def _odd_back(x, g, w_all, a, wo_all, layer, fg, final_norm):
    B, T, D = x.shape
    tm = min(T, ROW_TILE)
    row = lambda w: pl.BlockSpec((1, tm, w), lambda b, t: (b, t, 0))
    return pl.pallas_call(
        functools.partial(_odd_back_kernel, final_norm=final_norm),
        grid=(B, T // tm),
        in_specs=[row(D), _const_spec(g.shape),
                  _layer_spec((D, RET_V), layer, col_block=(2 * RET_QK + RET_V) // RET_V),
                  row(RET_V), _layer_spec(wo_all.shape[1:], layer), _const_spec(fg.shape)],
        out_specs=row(D),
        out_shape=jax.ShapeDtypeStruct((B, T, D), F32),
        compiler_params=_params("parallel", "parallel"),
        name="odd_back",
    )(x, g, w_all, a, wo_all, fg)


def _even_weights(w_in, w_uq, w_ukv):
    n_lat = MLA_Q_LORA + MLA_KV_LORA
    wlat = jnp.concatenate(
        [w_in[:, :n_lat + MLA_ROPE], jnp.zeros((D_MODEL, LANES - MLA_ROPE), w_in.dtype)], axis=1)
    wu = w_in[:, n_lat + MLA_ROPE:n_lat + MLA_ROPE + MIX_B]
    wg = w_in[:, n_lat + MLA_ROPE + MIX_B:]
    uq = w_uq.reshape(MLA_Q_LORA, MLA_HEADS, MLA_NOPE + MLA_ROPE)
    uq = jnp.pad(uq, ((0, 0), (0, 0), (0, HEAD_PAD - MLA_NOPE - MLA_ROPE)))
    uqt = uq.reshape(MLA_Q_LORA, MLA_HEADS * HEAD_PAD).T
    ukv = w_ukv.reshape(MLA_KV_LORA, MLA_HEADS, MLA_NOPE + MLA_V)
    uk = ukv[:, :, :MLA_NOPE].reshape(MLA_KV_LORA, MIX_A)
    uvt = ukv[:, :, MLA_NOPE:].reshape(MLA_KV_LORA, MIX_A).T
    return tuple(w.astype(BF16) for w in (wlat, wu, wg, uqt, uk, uvt))


def kernel(x, positions, a_norm_g, a_w_in, a_q_norm_g, a_w_uq, a_kv_norm_g, a_w_ukv, a_pool_w,
           a_pool_scale, a_w_out, r_norm_g, r_w_in, r_decay_fwd, r_decay_bwd, r_gn_g, r_w_out,
           final_norm_g):
    depth = a_norm_g.shape[0] + r_norm_g.shape[0]
    assert depth % 2 == 0, "the final norm is fused into the last (odd) layer's back kernel"
    ca, sa, cat, sat, cr, sr = _rope_tables(positions)
    fg = final_norm_g[None]
    a_pool_w, a_w_out, r_w_in, r_w_out = (w.astype(BF16) for w in (a_pool_w, a_w_out, r_w_in, r_w_out))
    for layer in range(depth):
        i = layer // 2
        if layer % 2 == 0:
            wlat, wu, wg, wuqt, wuk, wuvt = _even_weights(a_w_in[i], a_w_uq[i], a_w_ukv[i])
            g = a_norm_g[i][None]
            qt, kn, kr, vt = _even_front(x, g, wlat, a_q_norm_g[i][None], wuqt,
                                         a_kv_norm_g[i][None], wuk, wuvt, ca, sa, cat, sat)
            a = _attention(qt, kn, kr, vt)
            x = _even_back(x, g, wu, wg, a, a_pool_w, a_pool_scale[i][None], a_w_out, i)
        else:
            g = r_norm_g[i][None]
            q, k, v = _odd_front(x, g, r_w_in, i, cr, sr)
            dec_f = jnp.broadcast_to(r_decay_fwd[i][:, None, None], (RET_HEADS, 1, LANES))
            dec_b = jnp.broadcast_to(r_decay_bwd[i][:, None, None], (RET_HEADS, 1, LANES))
            o = _retention(q, k, v, dec_f, dec_b, r_gn_g[i][None])
            x = _odd_back(x, g, r_w_in, o, r_w_out, i, fg, final_norm=(layer == depth - 1))
    return x
```

```python
import functools

import jax
import jax.numpy as jnp
from jax import lax
from jax.experimental import pallas as pl
from jax.experimental.pallas import tpu as pltpu

ROPE_BASE = 10000.0
NORM_EPS = 1e-6
LOG2E = 1.4426950408889634

D_MODEL = 1024
MLA_HEADS = 8
MLA_NOPE = 128
MLA_ROPE = 64
MLA_V = 128
MLA_Q_LORA = 384
MLA_KV_LORA = 128
POOL_WINDOWS = (2, 4, 8, 16)
POOL_DIM = 256
MIX_A = MLA_HEADS * MLA_V
MIX_B = len(POOL_WINDOWS) * POOL_DIM
RET_HEADS = 4
RET_DK = 256
RET_DV = 512
RET_QK = RET_HEADS * RET_DK
RET_V = RET_HEADS * RET_DV

LANES = 128
SUBLANES = 8
HEAD_PAD = 2 * LANES
Q_ROWS = MLA_HEADS * (MLA_NOPE + MLA_ROPE)
POOL_HALO = SUBLANES
VMEM_LIMIT = 56 * 1024 * 1024

ROW_TILE = 1024
SUB_TILE = 512
ATTN_KV_CHUNK = 256
ATTN_HEADS_PER_STEP = 4
ATTN_GROUP = 2
RET_CHUNK = 256
RET_HEADS_PER_STEP = 2

BF16 = jnp.bfloat16
F32 = jnp.float32


def _params(*sem, **kw):
    return pltpu.CompilerParams(dimension_semantics=sem, vmem_limit_bytes=VMEM_LIMIT, **kw)


def _const_spec(shape):
    nd = len(shape)
    return pl.BlockSpec(shape, lambda *_: (0,) * nd, pipeline_mode=pl.Buffered(1))


def _layer_spec(block, layer, col_block=0):
    idx = (layer,) + (0,) * (len(block) - 1) + (col_block,)
    return pl.BlockSpec((None,) + tuple(block), lambda *_: idx, pipeline_mode=pl.Buffered(1))


def _sub_tiles(rows):
    sub = min(rows, SUB_TILE)
    return [slice(r, r + sub) for r in range(0, rows, sub)]


def _rms(x, g):
    return x * lax.rsqrt(jnp.mean(x * x, axis=-1, keepdims=True) + NORM_EPS) * g


def _silu(x):
    return x / (1.0 + jnp.exp(-x))


def _dot(a, b):
    return jnp.dot(a, b, preferred_element_type=F32)


def _dot_nt(a, b):
    return lax.dot_general(a, b, (((1,), (1,)), ((), ())), preferred_element_type=F32)


def _dot_tn(a, b):
    return lax.dot_general(a, b, (((0,), (0,)), ((), ())), preferred_element_type=F32)


def _split3(a):
    hi = a.astype(BF16)
    r1 = a - hi.astype(F32)
    mid = r1.astype(BF16)
    lo = (r1 - mid.astype(F32)).astype(BF16)
    return jnp.concatenate([hi, mid, lo], axis=1)


def _rope_tab_kernel(pos_ref, inv_r_ref, selc_ref, sels_ref, selt_ref,
                     ca_ref, sa_ref, cat_ref, sat_ref, cr_ref, sr_ref):
    tt = pos_ref.shape[2]
    sub = min(tt, 256)
    for r in range(0, tt, sub):
        rows = slice(r, r + sub)
        pos = jnp.broadcast_to(pos_ref[0, :, rows].astype(F32), (SUBLANES, sub)).T[:, :1]
        ang = pos * inv_r_ref[...]
        c = jnp.cos(ang)
        s = jnp.sin(ang)
        cr_ref[0, rows, :] = c
        sr_ref[0, rows, :] = s
        c3 = _split3(c)
        s3 = _split3(s)
        ca_ref[0, rows, :] = _dot(c3, selc_ref[...])
        sa_ref[0, rows, :] = _dot(s3, sels_ref[...])
        cat_ref[0, :, rows] = _dot_nt(selt_ref[...], c3)
        sat_ref[0, :, rows] = _dot_nt(selt_ref[...], s3)


def _rope_tables(positions):
    B, T = positions.shape
    half_a = MLA_ROPE // 2
    stride = RET_DK // MLA_ROPE
    inv_r = 1.0 / (ROPE_BASE ** (jnp.arange(0, RET_DK, 2, dtype=F32) / RET_DK))
    assert inv_r.shape[0] == LANES and stride * half_a == LANES
    pick = (jnp.arange(LANES)[:, None] == stride * jnp.arange(half_a)[None, :]).astype(BF16)
    zeros = jnp.zeros((LANES, LANES - 2 * half_a), BF16)
    sel_c = jnp.concatenate([pick, pick, zeros], axis=1)
    sel_s = jnp.concatenate([-pick, pick, zeros], axis=1)
    tile3 = lambda m: jnp.concatenate([m, m, m], axis=0)
    tt = min(T, 1024)
    tab = jax.ShapeDtypeStruct((B, T, LANES), F32)
    tab_t = jax.ShapeDtypeStruct((B, half_a, T), F32)
    blk = pl.BlockSpec((1, tt, LANES), lambda b, t: (b, t, 0))
    blk_t = pl.BlockSpec((1, half_a, tt), lambda b, t: (b, 0, t))
    return pl.pallas_call(
        _rope_tab_kernel,
        grid=(B, T // tt),
        in_specs=[pl.BlockSpec((1, 1, tt), lambda b, t: (b, 0, t)),
                  _const_spec((1, LANES)), _const_spec((3 * LANES, LANES)),
                  _const_spec((3 * LANES, LANES)), _const_spec((half_a, 3 * LANES))],
        out_specs=[blk, blk, blk_t, blk_t, blk, blk],
        out_shape=[tab, tab, tab_t, tab_t, tab, tab],
        compiler_params=_params("parallel", "parallel"),
        name="rope_tables",
    )(positions.reshape(B, 1, T), inv_r[None], tile3(sel_c), tile3(sel_s), tile3(pick).T)


def _rope_pad(v, c, s):
    q = LANES // 4
    return v * c + (pltpu.roll(v, 3 * q, 1) + pltpu.roll(v, q, 1)) * s


def _even_front_kernel(x_ref, g_ref, wlat_ref, gq_ref, wuqt_ref, gkv_ref, wuk_ref, wuvt_ref,
                       ca_ref, sa_ref, cat_ref, sat_ref, qt_ref, kn_ref, kr_ref, vt_ref):
    q_scale = (MLA_NOPE + MLA_ROPE) ** -0.5 * LOG2E
    half = MLA_ROPE // 2
    for j, rows in enumerate(_sub_tiles(x_ref.shape[1])):
        h = _rms(x_ref[0, rows, :], g_ref[...]).astype(BF16)
        lat = _dot(h, wlat_ref[...])
        cq = _rms(lat[:, :MLA_Q_LORA], gq_ref[...]).astype(BF16)
        qt = _dot_nt(wuqt_ref[...], cq)
        ckv = _rms(lat[:, MLA_Q_LORA:MLA_Q_LORA + MLA_KV_LORA], gkv_ref[...]).astype(BF16)
        kn_ref[0, rows, :] = _dot(ckv, wuk_ref[...]).astype(BF16)
        vt_ref[0, :, rows] = _dot_nt(wuvt_ref[...], ckv).astype(BF16)
        kr_ref[0, rows, :] = _rope_pad(lat[:, MLA_Q_LORA + MLA_KV_LORA:], ca_ref[0, rows, :],
                                       sa_ref[0, rows, :]).astype(BF16)
        ct = cat_ref[0, :, rows]
        st = sat_ref[0, :, rows]
        qt_ref[0, j, :MIX_A, :] = (qt[:MIX_A] * q_scale).astype(BF16)
        for hd in range(MLA_HEADS):
            r1 = MIX_A + hd * MLA_ROPE
            r2 = r1 + half
            r3 = r2 + half
            x1 = qt[r1:r2]
            x2 = qt[r2:r3]
            qt_ref[0, j, r1:r2, :] = ((x1 * ct - x2 * st) * q_scale).astype(BF16)
            qt_ref[0, j, r2:r3, :] = ((x2 * ct + x1 * st) * q_scale).astype(BF16)


def _even_front(x, g, wlat, gq, wuqt, gkv, wuk, wuvt, ca, sa, cat, sat):
    B, T, D = x.shape
    tm = min(T, ROW_TILE)
    sub = min(tm, SUB_TILE)
    row = lambda w: pl.BlockSpec((1, tm, w), lambda b, t: (b, t, 0))
    col = lambda r: pl.BlockSpec((1, r, tm), lambda b, t: (b, 0, t))
    consts = [g, wlat, gq, wuqt, gkv, wuk, wuvt]
    return pl.pallas_call(
        _even_front_kernel,
        grid=(B, T // tm),
        in_specs=[row(D)] + [_const_spec(c.shape) for c in consts]
        + [row(LANES), row(LANES), col(MLA_ROPE // 2), col(MLA_ROPE // 2)],
        out_specs=[pl.BlockSpec((1, tm // sub, Q_ROWS, sub), lambda b, t: (b, t, 0, 0)),
                   row(MLA_HEADS * MLA_NOPE), row(LANES), col(MIX_A)],
        out_shape=[jax.ShapeDtypeStruct((B, T // sub, Q_ROWS, sub), BF16),
                   jax.ShapeDtypeStruct((B, T, MLA_HEADS * MLA_NOPE), BF16),
                   jax.ShapeDtypeStruct((B, T, LANES), BF16),
                   jax.ShapeDtypeStruct((B, MIX_A, T), BF16)],
        compiler_params=_params("parallel", "parallel"),
        name="even_front",
    )(x, *consts, ca, sa, cat, sat)


def _attn_kernel(qn_ref, qr_ref, kn_ref, kr_ref, vt_ref, o_ref, s_ref, m_ref):
    n_q, tq = qn_ref.shape[1], qn_ref.shape[3]
    grp_blocks, n_kc, kc = s_ref.shape[1], s_ref.shape[2], s_ref.shape[3]
    n_blk = n_q * (qn_ref.shape[2] // MLA_NOPE)
    q_pad = jnp.zeros((HEAD_PAD - MLA_NOPE - MLA_ROPE, tq), BF16)
    n_grp = n_blk // grp_blocks
    sub = (kc // SUBLANES, SUBLANES, tq)

    def score_chunk(g, j, c, m8):
        hd, qb = divmod(g * grp_blocks + j, n_q)
        qt = jnp.concatenate([qn_ref[0, qb, hd * MLA_NOPE:(hd + 1) * MLA_NOPE, :],
                              qr_ref[0, qb, hd * MLA_ROPE:(hd + 1) * MLA_ROPE, :], q_pad], axis=0)
        keys = slice(c * kc, (c + 1) * kc)
        k = jnp.concatenate([kn_ref[0, keys, hd * MLA_NOPE:(hd + 1) * MLA_NOPE],
                             kr_ref[0, keys, :]], axis=1)
        s = _dot(k, qt)
        s_ref[g % 2, j, c] = s
        cm = jnp.max(s.reshape(sub), axis=0)
        return cm if m8 is None else jnp.maximum(m8, cm)

    def value_chunk(g, j, c, m, l8, acc):
        hd = (g * grp_blocks + j) // n_q
        p = jnp.exp2(s_ref[g % 2, j, c] - m)
        ps = jnp.sum(p.reshape(sub), axis=0)
        vt = vt_ref[0, hd * MLA_V:(hd + 1) * MLA_V, c * kc:(c + 1) * kc]
        pv = _dot(vt, p.astype(BF16))
        return (ps if l8 is None else l8 + ps), (pv if acc is None else acc + pv)

    def stage(k):
        run_v = k >= 1
        run_s = k < n_grp
        blocks = range(grp_blocks)
        if run_v:
            m = [jnp.max(m_ref[(k - 1) % 2, j], axis=0, keepdims=True) for j in blocks]
        m8 = [None] * grp_blocks
        l8 = [None] * grp_blocks
        acc = [None] * grp_blocks
        for c in range(n_kc):
            for j in blocks:
                if run_v:
                    l8[j], acc[j] = value_chunk(k - 1, j, c, m[j], l8[j], acc[j])
                if run_s:
                    m8[j] = score_chunk(k, j, c, m8[j])
        for j in blocks:
            if run_s:
                m_ref[k % 2, j] = m8[j]
            if run_v:
                hd, qb = divmod((k - 1) * grp_blocks + j, n_q)
                l = jnp.sum(l8[j], axis=0, keepdims=True)
                o_ref[0, qb * tq:(qb + 1) * tq, hd * MLA_V:(hd + 1) * MLA_V] = (
                    (acc[j] / l).T.astype(BF16))

    one = jnp.minimum(pl.program_id(0) + 1, 1)
    for k in range(n_grp + 1):
        lax.fori_loop(0, one, lambda _, carry, k=k: (stage(k), carry)[1], 0)


def _attention(qt, kn, kr, vt):
    B, T, _ = kn.shape
    n_q, tq = qt.shape[1], qt.shape[3]
    kc = min(T, ATTN_KV_CHUNK)
    hs = ATTN_HEADS_PER_STEP
    rope_blk0 = MIX_A // (hs * MLA_ROPE)
    return pl.pallas_call(
        _attn_kernel,
        grid=(B, MLA_HEADS // hs),
        in_specs=[pl.BlockSpec((1, n_q, hs * MLA_NOPE, tq), lambda b, h: (b, 0, h, 0)),
                  pl.BlockSpec((1, n_q, hs * MLA_ROPE, tq), lambda b, h: (b, 0, rope_blk0 + h, 0)),
                  pl.BlockSpec((1, T, hs * MLA_NOPE), lambda b, h: (b, 0, h)),
                  pl.BlockSpec((1, T, LANES), lambda b, h: (b, 0, 0)),
                  pl.BlockSpec((1, hs * MLA_V, T), lambda b, h: (b, h, 0))],
        out_specs=pl.BlockSpec((1, T, hs * MLA_V), lambda b, h: (b, 0, h)),
        out_shape=jax.ShapeDtypeStruct((B, T, MIX_A), BF16),
        scratch_shapes=[pltpu.VMEM((2, ATTN_GROUP, T // kc, kc, tq), F32),
                        pltpu.VMEM((2, ATTN_GROUP, SUBLANES, tq), F32)],
        compiler_params=_params("parallel", "parallel"),
        name="mla_attention",
    )(qt, qt, kn, kr, vt)


def _even_back_kernel(x_ref, xp_ref, xn_ref, g_ref, wu_ref, wg_ref, a_ref, pw_ref, ps_ref, wo_ref,
                      o_ref, *, seq_len):
    tm = x_ref.shape[1]
    g = g_ref[...]
    wu = wu_ref[...]
    for rows in _sub_tiles(tm):
        sub = rows.stop - rows.start
        t0 = pl.program_id(1) * tm + rows.start
        x = x_ref[0, rows, :]
        h = _rms(x, g).astype(BF16)
        xp = xp_ref[0] if rows.start == 0 else x_ref[0, rows.start - POOL_HALO:rows.start, :]
        xn = xn_ref[0] if rows.stop == tm else x_ref[0, rows.stop:rows.stop + POOL_HALO, :]
        hp = _rms(xp, g).astype(BF16)
        hn = _rms(xn, g).astype(BF16)
        up = jnp.where(t0 > 0, _dot(hp, wu), 0.0)
        un = jnp.where(t0 + sub < seq_len, _dot(hn, wu), 0.0)
        u = _dot(h, wu)
        ue = jnp.concatenate([up, u, un], axis=0)
        ext = sub + 2 * POOL_HALO
        t = (t0 + lax.broadcasted_iota(jnp.int32, (sub, 1), 0))
        gate = _dot(h, wg_ref[...])
        sg = _silu(gate)
        ya = (a_ref[0, rows, :].astype(F32) * sg[:, :MIX_A]).astype(BF16)
        y = _dot(ya, wo_ref[:MIX_A, :])
        for gi, w in enumerate(POOL_WINDOWS):
            left = w // 2
            right = w - 1 - left
            cols = slice(gi * POOL_DIM, (gi + 1) * POOL_DIM)
            run = ue[:, cols]
            n = 1
            while n < left:
                run = run + pltpu.roll(run, ext - n, 0)
                n *= 2
            acc = (run + pltpu.roll(run, left, 0))[POOL_HALO:POOL_HALO + sub]
            cnt = (jnp.minimum(t + right, seq_len - 1) - jnp.maximum(t - left, 0) + 1).astype(F32)
            d = (acc / cnt - u[:, cols]).astype(BF16)
            bg = _dot(d, pw_ref[gi]) * ps_ref[:, cols]
            yb = (bg * sg[:, MIX_A + gi * POOL_DIM:MIX_A + (gi + 1) * POOL_DIM]).astype(BF16)
            y = y + _dot(yb, wo_ref[MIX_A + gi * POOL_DIM:MIX_A + (gi + 1) * POOL_DIM, :])
        o_ref[0, rows, :] = x + y


def _even_back(x, g, wu, wg, a, pw_all, ps, wo_all, layer):
    B, T, D = x.shape
    tm = min(T, ROW_TILE)
    nb = tm // POOL_HALO
    last = T // POOL_HALO - 1
    row = lambda w: pl.BlockSpec((1, tm, w), lambda b, t: (b, t, 0))
    return pl.pallas_call(
        functools.partial(_even_back_kernel, seq_len=T),
        grid=(B, T // tm),
        in_specs=[row(D),
                  pl.BlockSpec((1, POOL_HALO, D), lambda b, t: (b, jnp.maximum(t * nb - 1, 0), 0)),
                  pl.BlockSpec((1, POOL_HALO, D), lambda b, t: (b, jnp.minimum((t + 1) * nb, last), 0)),
                  _const_spec(g.shape), _const_spec(wu.shape), _const_spec(wg.shape), row(MIX_A),
                  _layer_spec(pw_all.shape[1:], layer), _const_spec(ps.shape),
                  _layer_spec(wo_all.shape[1:], layer)],
        out_specs=row(D),
        out_shape=jax.ShapeDtypeStruct((B, T, D), F32),
        compiler_params=_params("parallel", "parallel"),
        name="even_back",
    )(x, x, x, g, wu, wg, a, pw_all, ps, wo_all)


def _odd_front_kernel(x_ref, g_ref, w_ref, c_ref, s_ref, q_ref, k_ref, v_ref):
    half = RET_DK // 2
    k_scale = RET_DK ** -0.5
    for rows in _sub_tiles(x_ref.shape[1]):
        h = _rms(x_ref[0, rows, :], g_ref[...]).astype(BF16)
        qkv = _dot(h, w_ref[...])
        c = c_ref[0, rows, :]
        s = s_ref[0, rows, :]
        for hd in range(RET_HEADS):
            for base, ref, scale in ((0, q_ref, None), (RET_QK, k_ref, k_scale)):
                lo = base + hd * RET_DK
                x1 = qkv[:, lo:lo + half]
                x2 = qkv[:, lo + half:lo + RET_DK]
                o1 = x1 * c - x2 * s
                o2 = x2 * c + x1 * s
                if scale is not None:
                    o1 = o1 * scale
                    o2 = o2 * scale
                ref[0, rows, hd * RET_DK:hd * RET_DK + half] = o1.astype(BF16)
                ref[0, rows, hd * RET_DK + half:(hd + 1) * RET_DK] = o2.astype(BF16)
        v_ref[0, rows, :] = qkv[:, 2 * RET_QK:].astype(BF16)


def _odd_front(x, g, w_all, layer, c, s):
    B, T, D = x.shape
    tm = min(T, ROW_TILE)
    row = lambda wd: pl.BlockSpec((1, tm, wd), lambda b, t: (b, t, 0))
    qk = jax.ShapeDtypeStruct((B, T, RET_QK), BF16)
    return pl.pallas_call(
        _odd_front_kernel,
        grid=(B, T // tm),
        in_specs=[row(D), _const_spec(g.shape), _layer_spec((D, 2 * RET_QK + RET_V), layer),
                  row(LANES), row(LANES)],
        out_specs=[row(RET_QK), row(RET_QK), row(RET_V)],
        out_shape=[qk, qk, jax.ShapeDtypeStruct((B, T, RET_V), BF16)],
        compiler_params=_params("parallel", "parallel"),
        name="odd_front",
    )(x, g, w_all, c, s)


def _log_sigmoid(x):
    return jnp.minimum(x, 0.0) - jnp.log1p(jnp.exp(-jnp.abs(x)))


def _retention_kernel(q_ref, k_ref, v_ref, df_ref, db_ref, gn_ref, o_ref, sf_ref, acc_ref):
    T = q_ref.shape[1]
    C = min(T, RET_CHUNK)
    n_chunks = T // C
    heads = range(q_ref.shape[2] // RET_DK)
    ri = lax.broadcasted_iota(jnp.int32, (C, 1), 0).astype(F32)
    diff = (lax.broadcasted_iota(jnp.int32, (C, C), 0)
            - lax.broadcasted_iota(jnp.int32, (C, C), 1)).astype(F32)

    def decays(hd):
        lf1 = _log_sigmoid(df_ref[hd])[:, :1]
        lb1 = _log_sigmoid(db_ref[hd])[:, :1]
        return dict(
            dmat=jnp.exp(jnp.where(diff >= 0, diff * lf1, -diff * lb1)),
            xi_f=jnp.exp((ri + 1.0) * lf1),
            xi_b=jnp.exp((C - ri) * lb1),
            zeta_f=jnp.exp((C - 1.0 - ri) * lf1),
            zeta_b=jnp.exp(ri * lb1),
            cd_f=jnp.exp(C * lf1), cd_b=jnp.exp(C * lb1))

    dec = [decays(hd) for hd in heads]

    def state_update(hd, i, zeta, cd, first):
        rows = slice(i * C, (i + 1) * C)
        kz = (k_ref[0, rows, hd * RET_DK:(hd + 1) * RET_DK].astype(F32) * zeta).astype(BF16)
        upd = _dot_tn(kz, v_ref[0, rows, hd * RET_DV:(hd + 1) * RET_DV])
        acc_ref[hd] = upd if first else acc_ref[hd] * cd + upd

    for i in range(n_chunks - 1):
        for hd in heads:
            state_update(hd, i, dec[hd]["zeta_f"], dec[hd]["cd_f"], first=(i == 0))
            sf_ref[hd, i + 1] = acc_ref[hd].astype(BF16)

    for i in reversed(range(n_chunks)):
        rows = slice(i * C, (i + 1) * C)
        for hd in heads:
            d = dec[hd]
            vcols = slice(hd * RET_DV, (hd + 1) * RET_DV)
            q = q_ref[0, rows, hd * RET_DK:(hd + 1) * RET_DK]
            qf = q.astype(F32)
            s = (_dot_nt(q, k_ref[0, rows, hd * RET_DK:(hd + 1) * RET_DK]) * d["dmat"]).astype(BF16)
            o = _dot(s, v_ref[0, rows, vcols])
            if i > 0:
                o = o + _dot((qf * d["xi_f"]).astype(BF16), sf_ref[hd, i])
            if i < n_chunks - 1:
                o = o + _dot((qf * d["xi_b"]).astype(BF16), acc_ref[hd].astype(BF16))
            mu = jnp.mean(o, axis=-1, keepdims=True)
            oc = o - mu
            var = jnp.mean(oc * oc, axis=-1, keepdims=True)
            o_ref[0, rows, vcols] = (oc * lax.rsqrt(var + NORM_EPS) * gn_ref[:, vcols]).astype(BF16)
            if i > 0:
                state_update(hd, i, d["zeta_b"], d["cd_b"], first=(i == n_chunks - 1))


def _retention(q, k, v, dec_f, dec_b, gn_g):
    B, T, _ = q.shape
    C = min(T, RET_CHUNK)
    hs = RET_HEADS_PER_STEP
    head = lambda w: pl.BlockSpec((1, T, hs * w), lambda b, h: (b, 0, h))
    dec = pl.BlockSpec((hs, 1, LANES), lambda b, h: (h, 0, 0))
    return pl.pallas_call(
        _retention_kernel,
        grid=(B, RET_HEADS // hs),
        in_specs=[head(RET_DK), head(RET_DK), head(RET_DV), dec, dec,
                  pl.BlockSpec((1, hs * RET_DV), lambda b, h: (0, h))],
        out_specs=head(RET_DV),
        out_shape=jax.ShapeDtypeStruct((B, T, RET_V), BF16),
        scratch_shapes=[pltpu.VMEM((hs, T // C, RET_DK, RET_DV), BF16),
                        pltpu.VMEM((hs, RET_DK, RET_DV), F32)],
        compiler_params=_params("parallel", "parallel"),
        name="retention",
    )(q, k, v, dec_f, dec_b, gn_g)


def _odd_back_kernel(x_ref, g_ref, wg_ref, a_ref, wo_ref, fg_ref, o_ref, *, final_norm):
    for rows in _sub_tiles(x_ref.shape[1]):
        x = x_ref[0, rows, :]
        h = _rms(x, g_ref[...]).astype(BF16)
        gate = _dot(h, wg_ref[...])
        y = (a_ref[0, rows, :].astype(F32) * _silu(gate)).astype(BF16)
        out = x + _dot(y, wo_ref[...])
        if final_norm:
            out = _rms(out, fg_ref[...])
        o_ref[0, rows, :] = out


def _odd_back(x, g, w_all, a, wo_all, layer, fg, final_norm):
    B, T, D = x.shape
    tm = min(T, ROW_TILE)
    row = lambda w: pl.BlockSpec((1, tm, w), lambda b, t: (b, t, 0))
    return pl.pallas_call(
        functools.partial(_odd_back_kernel, final_norm=final_norm),
        grid=(B, T // tm),
        in_specs=[row(D), _const_spec(g.shape),
                  _layer_spec((D, RET_V), layer, col_block=(2 * RET_QK + RET_V) // RET_V),
                  row(RET_V), _layer_spec(wo_all.shape[1:], layer), _const_spec(fg.shape)],
        out_specs=row(D),
        out_shape=jax.ShapeDtypeStruct((B, T, D), F32),
        compiler_params=_params("parallel", "parallel"),
        name="odd_back",
    )(x, g, w_all, a, wo_all, fg)


def _even_weights(w_in, w_uq, w_ukv):
    n_lat = MLA_Q_LORA + MLA_KV_LORA
    wlat = jnp.concatenate(
        [w_in[:, :n_lat + MLA_ROPE], jnp.zeros((D_MODEL, LANES - MLA_ROPE), w_in.dtype)], axis=1)
    wu = w_in[:, n_lat + MLA_ROPE:n_lat + MLA_ROPE + MIX_B]
    wg = w_in[:, n_lat + MLA_ROPE + MIX_B:]
    uq = w_uq.reshape(MLA_Q_LORA, MLA_HEADS, MLA_NOPE + MLA_ROPE)
    uqt = jnp.concatenate([uq[:, :, :MLA_NOPE].reshape(MLA_Q_LORA, MIX_A),
                           uq[:, :, MLA_NOPE:].reshape(MLA_Q_LORA, MLA_HEADS * MLA_ROPE)],
                          axis=1).T
    ukv = w_ukv.reshape(MLA_KV_LORA, MLA_HEADS, MLA_NOPE + MLA_V)
    uk = ukv[:, :, :MLA_NOPE].reshape(MLA_KV_LORA, MIX_A)
    uvt = ukv[:, :, MLA_NOPE:].reshape(MLA_KV_LORA, MIX_A).T
    return tuple(w.astype(BF16) for w in (wlat, wu, wg, uqt, uk, uvt))


def kernel(x, positions, a_norm_g, a_w_in, a_q_norm_g, a_w_uq, a_kv_norm_g, a_w_ukv, a_pool_w,
           a_pool_scale, a_w_out, r_norm_g, r_w_in, r_decay_fwd, r_decay_bwd, r_gn_g, r_w_out,
           final_norm_g):
    depth = a_norm_g.shape[0] + r_norm_g.shape[0]
    assert depth % 2 == 0, "the final norm is fused into the last (odd) layer's back kernel"
    ca, sa, cat, sat, cr, sr = _rope_tables(positions)
    fg = final_norm_g[None]
    a_pool_w, a_w_out, r_w_in, r_w_out = (w.astype(BF16) for w in (a_pool_w, a_w_out, r_w_in, r_w_out))
    for layer in range(depth):
        i = layer // 2
        if layer % 2 == 0:
            wlat, wu, wg, wuqt, wuk, wuvt = _even_weights(a_w_in[i], a_w_uq[i], a_w_ukv[i])
            g = a_norm_g[i][None]
            qt, kn, kr, vt = _even_front(x, g, wlat, a_q_norm_g[i][None], wuqt,
                                         a_kv_norm_g[i][None], wuk, wuvt, ca, sa, cat, sat)
            a = _attention(qt, kn, kr, vt)
            x = _even_back(x, g, wu, wg, a, a_pool_w, a_pool_scale[i][None], a_w_out, i)
        else:
            g = r_norm_g[i][None]
            q, k, v = _odd_front(x, g, r_w_in, i, cr, sr)
            dec_f = jnp.broadcast_to(r_decay_fwd[i][:, None, None], (RET_HEADS, 1, LANES))
            dec_b = jnp.broadcast_to(r_decay_bwd[i][:, None, None], (RET_HEADS, 1, LANES))
            o = _retention(q, k, v, dec_f, dec_b, r_gn_g[i][None])
            x = _odd_back(x, g, r_w_in, o, r_w_out, i, fg, final_norm=(layer == depth - 1))
    return x
```

```python
import functools

import jax
import jax.numpy as jnp
from jax import lax
from jax.experimental import pallas as pl
from jax.experimental.pallas import tpu as pltpu

ROPE_BASE = 10000.0
NORM_EPS = 1e-6
LOG2E = 1.4426950408889634

D_MODEL = 1024
MLA_HEADS = 8
MLA_NOPE = 128
MLA_ROPE = 64
MLA_V = 128
MLA_Q_LORA = 384
MLA_KV_LORA = 128
POOL_WINDOWS = (2, 4, 8, 16)
POOL_DIM = 256
MIX_A = MLA_HEADS * MLA_V
MIX_B = len(POOL_WINDOWS) * POOL_DIM
RET_HEADS = 4
RET_DK = 256
RET_DV = 512
RET_QK = RET_HEADS * RET_DK
RET_V = RET_HEADS * RET_DV

LANES = 128
SUBLANES = 8
HEAD_PAD = 2 * LANES
Q_ROWS = MLA_HEADS * (MLA_NOPE + MLA_ROPE)
POOL_HALO = SUBLANES
VMEM_LIMIT = 56 * 1024 * 1024

ROW_TILE = 1024
SUB_TILE = 512
ATTN_KV_CHUNK = 256
ATTN_HEADS_PER_STEP = 4
ATTN_GROUP = 2
RET_CHUNK = 256
RET_HEADS_PER_STEP = 2

BF16 = jnp.bfloat16
F32 = jnp.float32


def _params(*sem, **kw):
    return pltpu.CompilerParams(dimension_semantics=sem, vmem_limit_bytes=VMEM_LIMIT, **kw)


def _const_spec(shape):
    nd = len(shape)
    return pl.BlockSpec(shape, lambda *_: (0,) * nd, pipeline_mode=pl.Buffered(1))


def _layer_spec(block, layer, col_block=0):
    idx = (layer,) + (0,) * (len(block) - 1) + (col_block,)
    return pl.BlockSpec((None,) + tuple(block), lambda *_: idx, pipeline_mode=pl.Buffered(1))


def _sub_tiles(rows):
    sub = min(rows, SUB_TILE)
    return [slice(r, r + sub) for r in range(0, rows, sub)]


def _rms(x, g):
    return x * lax.rsqrt(jnp.mean(x * x, axis=-1, keepdims=True) + NORM_EPS) * g


def _silu(x):
    return x / (1.0 + jnp.exp(-x))


def _dot(a, b):
    return jnp.dot(a, b, preferred_element_type=F32)


def _dot_nt(a, b):
    return lax.dot_general(a, b, (((1,), (1,)), ((), ())), preferred_element_type=F32)


def _dot_tn(a, b):
    return lax.dot_general(a, b, (((0,), (0,)), ((), ())), preferred_element_type=F32)


def _split3(a):
    hi = a.astype(BF16)
    r1 = a - hi.astype(F32)
    mid = r1.astype(BF16)
    lo = (r1 - mid.astype(F32)).astype(BF16)
    return jnp.concatenate([hi, mid, lo], axis=1)


def _rope_tab_kernel(pos_ref, inv_r_ref, selc_ref, sels_ref, selt_ref,
                     ca_ref, sa_ref, cat_ref, sat_ref, cr_ref, sr_ref):
    tt = pos_ref.shape[2]
    sub = min(tt, 256)
    for r in range(0, tt, sub):
        rows = slice(r, r + sub)
        pos = jnp.broadcast_to(pos_ref[0, :, rows].astype(F32), (SUBLANES, sub)).T[:, :1]
        ang = pos * inv_r_ref[...]
        c = jnp.cos(ang)
        s = jnp.sin(ang)
        cr_ref[0, rows, :] = c
        sr_ref[0, rows, :] = s
        c3 = _split3(c)
        s3 = _split3(s)
        ca_ref[0, rows, :] = _dot(c3, selc_ref[...])
        sa_ref[0, rows, :] = _dot(s3, sels_ref[...])
        cat_ref[0, :, rows] = _dot_nt(selt_ref[...], c3)
        sat_ref[0, :, rows] = _dot_nt(selt_ref[...], s3)


def _rope_tables(positions):
    B, T = positions.shape
    half_a = MLA_ROPE // 2
    stride = RET_DK // MLA_ROPE
    inv_r = 1.0 / (ROPE_BASE ** (jnp.arange(0, RET_DK, 2, dtype=F32) / RET_DK))
    assert inv_r.shape[0] == LANES and stride * half_a == LANES
    pick = (jnp.arange(LANES)[:, None] == stride * jnp.arange(half_a)[None, :]).astype(BF16)
    zeros = jnp.zeros((LANES, LANES - 2 * half_a), BF16)
    sel_c = jnp.concatenate([pick, pick, zeros], axis=1)
    sel_s = jnp.concatenate([-pick, pick, zeros], axis=1)
    tile3 = lambda m: jnp.concatenate([m, m, m], axis=0)
    tt = min(T, 1024)
    tab = jax.ShapeDtypeStruct((B, T, LANES), F32)
    tab_t = jax.ShapeDtypeStruct((B, half_a, T), F32)
    blk = pl.BlockSpec((1, tt, LANES), lambda b, t: (b, t, 0))
    blk_t = pl.BlockSpec((1, half_a, tt), lambda b, t: (b, 0, t))
    return pl.pallas_call(
        _rope_tab_kernel,
        grid=(B, T // tt),
        in_specs=[pl.BlockSpec((1, 1, tt), lambda b, t: (b, 0, t)),
                  _const_spec((1, LANES)), _const_spec((3 * LANES, LANES)),
                  _const_spec((3 * LANES, LANES)), _const_spec((half_a, 3 * LANES))],
        out_specs=[blk, blk, blk_t, blk_t, blk, blk],
        out_shape=[tab, tab, tab_t, tab_t, tab, tab],
        compiler_params=_params("parallel", "parallel"),
        name="rope_tables",
    )(positions.reshape(B, 1, T), inv_r[None], tile3(sel_c), tile3(sel_s), tile3(pick).T)


def _rope_pad(v, c, s):
    q = LANES // 4
    return v * c + (pltpu.roll(v, 3 * q, 1) + pltpu.roll(v, q, 1)) * s


def _even_front_kernel(x_ref, g_ref, wlat_ref, gq_ref, wuqt_ref, gkv_ref, wuk_ref, wuvt_ref,
                       ca_ref, sa_ref, cat_ref, sat_ref, qt_ref, kn_ref, kr_ref, vt_ref):
    q_scale = (MLA_NOPE + MLA_ROPE) ** -0.5 * LOG2E
    half = MLA_ROPE // 2
    for j, rows in enumerate(_sub_tiles(x_ref.shape[1])):
        h = _rms(x_ref[0, rows, :], g_ref[...]).astype(BF16)
        lat = _dot(h, wlat_ref[...])
        cq = _rms(lat[:, :MLA_Q_LORA], gq_ref[...]).astype(BF16)
        qt = _dot_nt(wuqt_ref[...], cq)
        ckv = _rms(lat[:, MLA_Q_LORA:MLA_Q_LORA + MLA_KV_LORA], gkv_ref[...]).astype(BF16)
        kn_ref[0, rows, :] = _dot(ckv, wuk_ref[...]).astype(BF16)
        vt_ref[0, :, rows] = _dot_nt(wuvt_ref[...], ckv).astype(BF16)
        kr_ref[0, rows, :] = _rope_pad(lat[:, MLA_Q_LORA + MLA_KV_LORA:], ca_ref[0, rows, :],
                                       sa_ref[0, rows, :]).astype(BF16)
        ct = cat_ref[0, :, rows]
        st = sat_ref[0, :, rows]
        qt_ref[0, j, :MIX_A, :] = (qt[:MIX_A] * q_scale).astype(BF16)
        for hd in range(MLA_HEADS):
            r1 = MIX_A + hd * MLA_ROPE
            r2 = r1 + half
            r3 = r2 + half
            x1 = qt[r1:r2]
            x2 = qt[r2:r3]
            qt_ref[0, j, r1:r2, :] = ((x1 * ct - x2 * st) * q_scale).astype(BF16)
            qt_ref[0, j, r2:r3, :] = ((x2 * ct + x1 * st) * q_scale).astype(BF16)


def _even_front(x, g, wlat, gq, wuqt, gkv, wuk, wuvt, ca, sa, cat, sat):
    B, T, D = x.shape
    tm = min(T, ROW_TILE)
    sub = min(tm, SUB_TILE)
    row = lambda w: pl.BlockSpec((1, tm, w), lambda b, t: (b, t, 0))
    col = lambda r: pl.BlockSpec((1, r, tm), lambda b, t: (b, 0, t))
    consts = [g, wlat, gq, wuqt, gkv, wuk, wuvt]
    return pl.pallas_call(
        _even_front_kernel,
        grid=(B, T // tm),
        in_specs=[row(D)] + [_const_spec(c.shape) for c in consts]
        + [row(LANES), row(LANES), col(MLA_ROPE // 2), col(MLA_ROPE // 2)],
        out_specs=[pl.BlockSpec((1, tm // sub, Q_ROWS, sub), lambda b, t: (b, t, 0, 0)),
                   row(MLA_HEADS * MLA_NOPE), row(LANES), col(MIX_A)],
        out_shape=[jax.ShapeDtypeStruct((B, T // sub, Q_ROWS, sub), BF16),
                   jax.ShapeDtypeStruct((B, T, MLA_HEADS * MLA_NOPE), BF16),
                   jax.ShapeDtypeStruct((B, T, LANES), BF16),
                   jax.ShapeDtypeStruct((B, MIX_A, T), BF16)],
        compiler_params=_params("parallel", "parallel"),
        name="even_front",
    )(x, *consts, ca, sa, cat, sat)


def _attn_kernel(qn_ref, qr_ref, kn_ref, kr_ref, vt_ref, o_ref, s_ref, m_ref):
    n_q, tq = qn_ref.shape[1], qn_ref.shape[3]
    grp_blocks, n_kc, kc = s_ref.shape[1], s_ref.shape[2], s_ref.shape[3]
    n_blk = n_q * (qn_ref.shape[2] // MLA_NOPE)
    q_pad = jnp.zeros((HEAD_PAD - MLA_NOPE - MLA_ROPE, tq), BF16)
    n_grp = n_blk // grp_blocks
    sub = (kc // SUBLANES, SUBLANES, tq)

    def score_chunk(g, j, c, m8):
        hd, qb = divmod(g * grp_blocks + j, n_q)
        qt = jnp.concatenate([qn_ref[0, qb, hd * MLA_NOPE:(hd + 1) * MLA_NOPE, :],
                              qr_ref[0, qb, hd * MLA_ROPE:(hd + 1) * MLA_ROPE, :], q_pad], axis=0)
        keys = slice(c * kc, (c + 1) * kc)
        k = jnp.concatenate([kn_ref[0, keys, hd * MLA_NOPE:(hd + 1) * MLA_NOPE],
                             kr_ref[0, keys, :]], axis=1)
        s = _dot(k, qt)
        s_ref[g % 2, j, c] = s
        cm = jnp.max(s.reshape(sub), axis=0)
        return cm if m8 is None else jnp.maximum(m8, cm)

    def value_chunk(g, j, c, m, l8, acc):
        hd = (g * grp_blocks + j) // n_q
        p = jnp.exp2(s_ref[g % 2, j, c] - m)
        ps = jnp.sum(p.reshape(sub), axis=0)
        vt = vt_ref[0, hd * MLA_V:(hd + 1) * MLA_V, c * kc:(c + 1) * kc]
        pv = _dot(vt, p.astype(BF16))
        return (ps if l8 is None else l8 + ps), (pv if acc is None else acc + pv)

    def stage(k):
        run_v = k >= 1
        run_s = k < n_grp
        blocks = range(grp_blocks)
        if run_v:
            m = [jnp.max(m_ref[(k - 1) % 2, j], axis=0, keepdims=True) for j in blocks]
        m8 = [None] * grp_blocks
        l8 = [None] * grp_blocks
        acc = [None] * grp_blocks
        for c in range(n_kc):
            for j in blocks:
                if run_v:
                    l8[j], acc[j] = value_chunk(k - 1, j, c, m[j], l8[j], acc[j])
                if run_s:
                    m8[j] = score_chunk(k, j, c, m8[j])
        for j in blocks:
            if run_s:
                m_ref[k % 2, j] = m8[j]
            if run_v:
                hd, qb = divmod((k - 1) * grp_blocks + j, n_q)
                l = jnp.sum(l8[j], axis=0, keepdims=True)
                o_ref[0, qb * tq:(qb + 1) * tq, hd * MLA_V:(hd + 1) * MLA_V] = (
                    (acc[j] / l).T.astype(BF16))

    one = jnp.minimum(pl.program_id(0) + 1, 1)
    for k in range(n_grp + 1):
        lax.fori_loop(0, one, lambda _, carry, k=k: (stage(k), carry)[1], 0)


def _attention(qt, kn, kr, vt):
    B, T, _ = kn.shape
    n_q, tq = qt.shape[1], qt.shape[3]
    kc = min(T, ATTN_KV_CHUNK)
    hs = ATTN_HEADS_PER_STEP
    rope_blk0 = MIX_A // (hs * MLA_ROPE)
    return pl.pallas_call(
        _attn_kernel,
        grid=(B, MLA_HEADS // hs),
        in_specs=[pl.BlockSpec((1, n_q, hs * MLA_NOPE, tq), lambda b, h: (b, 0, h, 0)),
                  pl.BlockSpec((1, n_q, hs * MLA_ROPE, tq), lambda b, h: (b, 0, rope_blk0 + h, 0)),
                  pl.BlockSpec((1, T, hs * MLA_NOPE), lambda b, h: (b, 0, h)),
                  pl.BlockSpec((1, T, LANES), lambda b, h: (b, 0, 0)),
                  pl.BlockSpec((1, hs * MLA_V, T), lambda b, h: (b, h, 0))],
        out_specs=pl.BlockSpec((1, T, hs * MLA_V), lambda b, h: (b, 0, h)),
        out_shape=jax.ShapeDtypeStruct((B, T, MIX_A), BF16),
        scratch_shapes=[pltpu.VMEM((2, ATTN_GROUP, T // kc, kc, tq), F32),
                        pltpu.VMEM((2, ATTN_GROUP, SUBLANES, tq), F32)],
        compiler_params=_params("parallel", "parallel"),
        name="mla_attention",
    )(qt, qt, kn, kr, vt)


def _pool_fold_kernel(wu_ref, pw_ref, ps_ref, o_ref):
    w = lax.dot_general(wu_ref[...], pw_ref[0], (((1,), (0,)), ((), ())),
                        precision=lax.Precision.HIGHEST, preferred_element_type=F32)
    o_ref[...] = (w * ps_ref[...]).astype(BF16)


def _pool_fold(wu, pw, ps):
    D = wu.shape[0]
    n_g = len(POOL_WINDOWS)
    return pl.pallas_call(
        _pool_fold_kernel,
        grid=(n_g,),
        in_specs=[pl.BlockSpec((D, POOL_DIM), lambda g: (0, g)),
                  pl.BlockSpec((1, POOL_DIM, POOL_DIM), lambda g: (g, 0, 0)),
                  pl.BlockSpec((1, POOL_DIM), lambda g: (0, g))],
        out_specs=pl.BlockSpec((D, POOL_DIM), lambda g: (0, g)),
        out_shape=jax.ShapeDtypeStruct((D, MIX_B), BF16),
        compiler_params=_params("parallel"),
        name="pool_fold",
    )(wu, pw, ps)


def _even_back_kernel(x_ref, xp_ref, xn_ref, g_ref, wu_ref, wg_ref, a_ref, wo_ref, o_ref, *, seq_len):
    tm = x_ref.shape[1]
    g = g_ref[...]
    wu = wu_ref[...]
    for rows in _sub_tiles(tm):
        sub = rows.stop - rows.start
        t0 = pl.program_id(1) * tm + rows.start
        x = x_ref[0, rows, :]
        h = _rms(x, g).astype(BF16)
        xp = xp_ref[0] if rows.start == 0 else x_ref[0, rows.start - POOL_HALO:rows.start, :]
        xn = xn_ref[0] if rows.stop == tm else x_ref[0, rows.stop:rows.stop + POOL_HALO, :]
        hp = _rms(xp, g).astype(BF16)
        hn = _rms(xn, g).astype(BF16)
        up = jnp.where(t0 > 0, _dot(hp, wu), 0.0)
        un = jnp.where(t0 + sub < seq_len, _dot(hn, wu), 0.0)
        u = _dot(h, wu)
        ue = jnp.concatenate([up, u, un], axis=0)
        ext = sub + 2 * POOL_HALO
        t = (t0 + lax.broadcasted_iota(jnp.int32, (sub, 1), 0))
        gate = _dot(h, wg_ref[...])
        sg = _silu(gate)
        ya = (a_ref[0, rows, :].astype(F32) * sg[:, :MIX_A]).astype(BF16)
        y = _dot(ya, wo_ref[:MIX_A, :])
        for gi, w in enumerate(POOL_WINDOWS):
            left = w // 2
            right = w - 1 - left
            cols = slice(gi * POOL_DIM, (gi + 1) * POOL_DIM)
            run = ue[:, cols]
            n = 1
            while n < left:
                run = run + pltpu.roll(run, ext - n, 0)
                n *= 2
            acc = (run + pltpu.roll(run, left, 0))[POOL_HALO:POOL_HALO + sub]
            cnt = (jnp.minimum(t + right, seq_len - 1) - jnp.maximum(t - left, 0) + 1).astype(F32)
            bg = acc / cnt - u[:, cols]
            yb = (bg * sg[:, MIX_A + gi * POOL_DIM:MIX_A + (gi + 1) * POOL_DIM]).astype(BF16)
            y = y + _dot(yb, wo_ref[MIX_A + gi * POOL_DIM:MIX_A + (gi + 1) * POOL_DIM, :])
        o_ref[0, rows, :] = x + y


def _even_back(x, g, wu, wg, a, wo_all, layer):
    B, T, D = x.shape
    tm = min(T, ROW_TILE)
    nb = tm // POOL_HALO
    last = T // POOL_HALO - 1
    row = lambda w: pl.BlockSpec((1, tm, w), lambda b, t: (b, t, 0))
    return pl.pallas_call(
        functools.partial(_even_back_kernel, seq_len=T),
        grid=(B, T // tm),
        in_specs=[row(D),
                  pl.BlockSpec((1, POOL_HALO, D), lambda b, t: (b, jnp.maximum(t * nb - 1, 0), 0)),
                  pl.BlockSpec((1, POOL_HALO, D), lambda b, t: (b, jnp.minimum((t + 1) * nb, last), 0)),
                  _const_spec(g.shape), _const_spec(wu.shape), _const_spec(wg.shape), row(MIX_A),
                  _layer_spec(wo_all.shape[1:], layer)],
        out_specs=row(D),
        out_shape=jax.ShapeDtypeStruct((B, T, D), F32),
        compiler_params=_params("parallel", "parallel"),
        name="even_back",
    )(x, x, x, g, wu, wg, a, wo_all)


def _odd_front_kernel(x_ref, g_ref, w_ref, c_ref, s_ref, q_ref, k_ref, v_ref):
    half = RET_DK // 2
    k_scale = RET_DK ** -0.5
    for rows in _sub_tiles(x_ref.shape[1]):
        h = _rms(x_ref[0, rows, :], g_ref[...]).astype(BF16)
        qkv = _dot(h, w_ref[...])
        c = c_ref[0, rows, :]
        s = s_ref[0, rows, :]
        for hd in range(RET_HEADS):
            for base, ref, scale in ((0, q_ref, None), (RET_QK, k_ref, k_scale)):
                lo = base + hd * RET_DK
                x1 = qkv[:, lo:lo + half]
                x2 = qkv[:, lo + half:lo + RET_DK]
                o1 = x1 * c - x2 * s
                o2 = x2 * c + x1 * s
                if scale is not None:
                    o1 = o1 * scale
                    o2 = o2 * scale
                ref[0, rows, hd * RET_DK:hd * RET_DK + half] = o1.astype(BF16)
                ref[0, rows, hd * RET_DK + half:(hd + 1) * RET_DK] = o2.astype(BF16)
        v_ref[0, rows, :] = qkv[:, 2 * RET_QK:].astype(BF16)


def _odd_front(x, g, w_all, layer, c, s):
    B, T, D = x.shape
    tm = min(T, ROW_TILE)
    row = lambda wd: pl.BlockSpec((1, tm, wd), lambda b, t: (b, t, 0))
    qk = jax.ShapeDtypeStruct((B, T, RET_QK), BF16)
    return pl.pallas_call(
        _odd_front_kernel,
        grid=(B, T // tm),
        in_specs=[row(D), _const_spec(g.shape), _layer_spec((D, 2 * RET_QK + RET_V), layer),
                  row(LANES), row(LANES)],
        out_specs=[row(RET_QK), row(RET_QK), row(RET_V)],
        out_shape=[qk, qk, jax.ShapeDtypeStruct((B, T, RET_V), BF16)],
        compiler_params=_params("parallel", "parallel"),
        name="odd_front",
    )(x, g, w_all, c, s)


def _log_sigmoid(x):
    return jnp.minimum(x, 0.0) - jnp.log1p(jnp.exp(-jnp.abs(x)))


def _retention_kernel(q_ref, k_ref, v_ref, df_ref, db_ref, gn_ref, o_ref, sf_ref, acc_ref):
    T = q_ref.shape[1]
    C = min(T, RET_CHUNK)
    n_chunks = T // C
    heads = range(q_ref.shape[2] // RET_DK)
    ri = lax.broadcasted_iota(jnp.int32, (C, 1), 0).astype(F32)
    diff = (lax.broadcasted_iota(jnp.int32, (C, C), 0)
            - lax.broadcasted_iota(jnp.int32, (C, C), 1)).astype(F32)

    def decays(hd):
        lf1 = _log_sigmoid(df_ref[hd])[:, :1]
        lb1 = _log_sigmoid(db_ref[hd])[:, :1]
        return dict(
            dmat=jnp.exp(jnp.where(diff >= 0, diff * lf1, -diff * lb1)),
            xi_f=jnp.exp((ri + 1.0) * lf1),
            xi_b=jnp.exp((C - ri) * lb1),
            zeta_f=jnp.exp((C - 1.0 - ri) * lf1),
            zeta_b=jnp.exp(ri * lb1),
            cd_f=jnp.exp(C * lf1), cd_b=jnp.exp(C * lb1))

    dec = [decays(hd) for hd in heads]

    def state_update(hd, i, zeta, cd, first):
        rows = slice(i * C, (i + 1) * C)
        kz = (k_ref[0, rows, hd * RET_DK:(hd + 1) * RET_DK].astype(F32) * zeta).astype(BF16)
        upd = _dot_tn(kz, v_ref[0, rows, hd * RET_DV:(hd + 1) * RET_DV])
        acc_ref[hd] = upd if first else acc_ref[hd] * cd + upd

    for i in range(n_chunks - 1):
        for hd in heads:
            state_update(hd, i, dec[hd]["zeta_f"], dec[hd]["cd_f"], first=(i == 0))
            sf_ref[hd, i + 1] = acc_ref[hd].astype(BF16)

    for i in reversed(range(n_chunks)):
        rows = slice(i * C, (i + 1) * C)
        for hd in heads:
            d = dec[hd]
            vcols = slice(hd * RET_DV, (hd + 1) * RET_DV)
            q = q_ref[0, rows, hd * RET_DK:(hd + 1) * RET_DK]
            qf = q.astype(F32)
            s = (_dot_nt(q, k_ref[0, rows, hd * RET_DK:(hd + 1) * RET_DK]) * d["dmat"]).astype(BF16)
            o = _dot(s, v_ref[0, rows, vcols])
            if i > 0:
                o = o + _dot((qf * d["xi_f"]).astype(BF16), sf_ref[hd, i])
            if i < n_chunks - 1:
                o = o + _dot((qf * d["xi_b"]).astype(BF16), acc_ref[hd].astype(BF16))
            mu = jnp.mean(o, axis=-1, keepdims=True)
            oc = o - mu
            var = jnp.mean(oc * oc, axis=-1, keepdims=True)
            o_ref[0, rows, vcols] = (oc * lax.rsqrt(var + NORM_EPS) * gn_ref[:, vcols]).astype(BF16)
            if i > 0:
                state_update(hd, i, d["zeta_b"], d["cd_b"], first=(i == n_chunks - 1))


def _retention(q, k, v, dec_f, dec_b, gn_g):
    B, T, _ = q.shape
    C = min(T, RET_CHUNK)
    hs = RET_HEADS_PER_STEP
    head = lambda w: pl.BlockSpec((1, T, hs * w), lambda b, h: (b, 0, h))
    dec = pl.BlockSpec((hs, 1, LANES), lambda b, h: (h, 0, 0))
    return pl.pallas_call(
        _retention_kernel,
        grid=(B, RET_HEADS // hs),
        in_specs=[head(RET_DK), head(RET_DK), head(RET_DV), dec, dec,
                  pl.BlockSpec((1, hs * RET_DV), lambda b, h: (0, h))],
        out_specs=head(RET_DV),
        out_shape=jax.ShapeDtypeStruct((B, T, RET_V), BF16),
        scratch_shapes=[pltpu.VMEM((hs, T // C, RET_DK, RET_DV), BF16),
                        pltpu.VMEM((hs, RET_DK, RET_DV), F32)],
        compiler_params=_params("parallel", "parallel"),
        name="retention",
    )(q, k, v, dec_f, dec_b, gn_g)


def _odd_back_kernel(x_ref, g_ref, wg_ref, a_ref, wo_ref, fg_ref, o_ref, *, final_norm):
    for rows in _sub_tiles(x_ref.shape[1]):
        x = x_ref[0, rows, :]
        h = _rms(x, g_ref[...]).astype(BF16)
        gate = _dot(h, wg_ref[...])
        y = (a_ref[0, rows, :].astype(F32) * _silu(gate)).astype(BF16)
        out = x + _dot(y, wo_ref[...])
        if final_norm:
            out = _rms(out, fg_ref[...])
        o_ref[0, rows, :] = out


def _odd_back(x, g, w_all, a, wo_all, layer, fg, final_norm):
    B, T, D = x.shape
    tm = min(T, ROW_TILE)
    row = lambda w: pl.BlockSpec((1, tm, w), lambda b, t: (b, t, 0))
    return pl.pallas_call(
        functools.partial(_odd_back_kernel, final_norm=final_norm),
        grid=(B, T // tm),
        in_specs=[row(D), _const_spec(g.shape),
                  _layer_spec((D, RET_V), layer, col_block=(2 * RET_QK + RET_V) // RET_V),
                  row(RET_V), _layer_spec(wo_all.shape[1:], layer), _const_spec(fg.shape)],
        out_specs=row(D),
        out_shape=jax.ShapeDtypeStruct((B, T, D), F32),
        compiler_params=_params("parallel", "parallel"),
        name="odd_back",
    )(x, g, w_all, a, wo_all, fg)


def _even_weights(w_in, w_uq, w_ukv):
    n_lat = MLA_Q_LORA + MLA_KV_LORA
    wlat = jnp.concatenate(
        [w_in[:, :n_lat + MLA_ROPE], jnp.zeros((D_MODEL, LANES - MLA_ROPE), w_in.dtype)], axis=1)
    wu = w_in[:, n_lat + MLA_ROPE:n_lat + MLA_ROPE + MIX_B]
    wg = w_in[:, n_lat + MLA_ROPE + MIX_B:]
    uq = w_uq.reshape(MLA_Q_LORA, MLA_HEADS, MLA_NOPE + MLA_ROPE)
    uqt = jnp.concatenate([uq[:, :, :MLA_NOPE].reshape(MLA_Q_LORA, MIX_A),
                           uq[:, :, MLA_NOPE:].reshape(MLA_Q_LORA, MLA_HEADS * MLA_ROPE)],
                          axis=1).T
    ukv = w_ukv.reshape(MLA_KV_LORA, MLA_HEADS, MLA_NOPE + MLA_V)
    uk = ukv[:, :, :MLA_NOPE].reshape(MLA_KV_LORA, MIX_A)
    uvt = ukv[:, :, MLA_NOPE:].reshape(MLA_KV_LORA, MIX_A).T
    return (wu,) + tuple(w.astype(BF16) for w in (wlat, wg, uqt, uk, uvt))


def kernel(x, positions, a_norm_g, a_w_in, a_q_norm_g, a_w_uq, a_kv_norm_g, a_w_ukv, a_pool_w,
           a_pool_scale, a_w_out, r_norm_g, r_w_in, r_decay_fwd, r_decay_bwd, r_gn_g, r_w_out,
           final_norm_g):
    depth = a_norm_g.shape[0] + r_norm_g.shape[0]
    assert depth % 2 == 0, "the final norm is fused into the last (odd) layer's back kernel"
    ca, sa, cat, sat, cr, sr = _rope_tables(positions)
    fg = final_norm_g[None]
    a_w_out, r_w_in, r_w_out = (w.astype(BF16) for w in (a_w_out, r_w_in, r_w_out))
    for layer in range(depth):
        i = layer // 2
        if layer % 2 == 0:
            wu, wlat, wg, wuqt, wuk, wuvt = _even_weights(a_w_in[i], a_w_uq[i], a_w_ukv[i])
            g = a_norm_g[i][None]
            qt, kn, kr, vt = _even_front(x, g, wlat, a_q_norm_g[i][None], wuqt,
                                         a_kv_norm_g[i][None], wuk, wuvt, ca, sa, cat, sat)
            a = _attention(qt, kn, kr, vt)
            wu = _pool_fold(wu, a_pool_w[i], a_pool_scale[i][None])
            x = _even_back(x, g, wu, wg, a, a_w_out, i)
        else:
            g = r_norm_g[i][None]
            q, k, v = _odd_front(x, g, r_w_in, i, cr, sr)
            dec_f = jnp.broadcast_to(r_decay_fwd[i][:, None, None], (RET_HEADS, 1, LANES))
            dec_b = jnp.broadcast_to(r_decay_bwd[i][:, None, None], (RET_HEADS, 1, LANES))
            o = _retention(q, k, v, dec_f, dec_b, r_gn_g[i][None])
            x = _odd_back(x, g, r_w_in, o, r_w_out, i, fg, final_norm=(layer == depth - 1))
    return x
```

```python
import functools

import jax
import jax.numpy as jnp
from jax import lax
from jax.experimental import pallas as pl
from jax.experimental.pallas import tpu as pltpu

ROPE_BASE = 10000.0
NORM_EPS = 1e-6
LOG2E = 1.4426950408889634

D_MODEL = 1024
MLA_HEADS = 8
MLA_NOPE = 128
MLA_ROPE = 64
MLA_V = 128
MLA_Q_LORA = 384
MLA_KV_LORA = 128
POOL_WINDOWS = (2, 4, 8, 16)
POOL_DIM = 256
MIX_A = MLA_HEADS * MLA_V
MIX_B = len(POOL_WINDOWS) * POOL_DIM
RET_HEADS = 4
RET_DK = 256
RET_DV = 512
RET_QK = RET_HEADS * RET_DK
RET_V = RET_HEADS * RET_DV

LANES = 128
SUBLANES = 8
HEAD_PAD = 2 * LANES
Q_ROWS = MLA_HEADS * (MLA_NOPE + MLA_ROPE)
POOL_HALO = SUBLANES
VMEM_LIMIT = 56 * 1024 * 1024

ROW_TILE = 1024
SUB_TILE = 512
ATTN_KV_CHUNK = 256
ATTN_HEADS_PER_STEP = 4
ATTN_GROUP = 2
RET_CHUNK = 256
RET_HEADS_PER_STEP = 2

BF16 = jnp.bfloat16
F32 = jnp.float32


def _params(*sem, **kw):
    return pltpu.CompilerParams(dimension_semantics=sem, vmem_limit_bytes=VMEM_LIMIT, **kw)


def _const_spec(shape):
    nd = len(shape)
    return pl.BlockSpec(shape, lambda *_: (0,) * nd, pipeline_mode=pl.Buffered(1))


def _layer_spec(block, layer, col_block=0):
    idx = (layer,) + (0,) * (len(block) - 1) + (col_block,)
    return pl.BlockSpec((None,) + tuple(block), lambda *_: idx, pipeline_mode=pl.Buffered(1))


def _sub_tiles(rows):
    sub = min(rows, SUB_TILE)
    return [slice(r, r + sub) for r in range(0, rows, sub)]


def _rms(x, g):
    return x * lax.rsqrt(jnp.mean(x * x, axis=-1, keepdims=True) + NORM_EPS) * g


def _silu(x):
    return x / (1.0 + jnp.exp(-x))


def _dot(a, b):
    return jnp.dot(a, b, preferred_element_type=F32)


def _dot_nt(a, b):
    return lax.dot_general(a, b, (((1,), (1,)), ((), ())), preferred_element_type=F32)


def _dot_tn(a, b):
    return lax.dot_general(a, b, (((0,), (0,)), ((), ())), preferred_element_type=F32)


def _split3(a):
    hi = a.astype(BF16)
    r1 = a - hi.astype(F32)
    mid = r1.astype(BF16)
    lo = (r1 - mid.astype(F32)).astype(BF16)
    return jnp.concatenate([hi, mid, lo], axis=1)


def _rope_tab_kernel(pos_ref, inv_r_ref, selc_ref, sels_ref, selt_ref,
                     ca_ref, sa_ref, cat_ref, sat_ref, cr_ref, sr_ref):
    tt = pos_ref.shape[2]
    sub = min(tt, 256)
    for r in range(0, tt, sub):
        rows = slice(r, r + sub)
        pos = jnp.broadcast_to(pos_ref[0, :, rows].astype(F32), (SUBLANES, sub)).T[:, :1]
        ang = pos * inv_r_ref[...]
        c = jnp.cos(ang)
        s = jnp.sin(ang)
        cr_ref[0, rows, :] = c
        sr_ref[0, rows, :] = s
        c3 = _split3(c)
        s3 = _split3(s)
        ca_ref[0, rows, :] = _dot(c3, selc_ref[...])
        sa_ref[0, rows, :] = _dot(s3, sels_ref[...])
        cat_ref[0, :, rows] = _dot_nt(selt_ref[...], c3)
        sat_ref[0, :, rows] = _dot_nt(selt_ref[...], s3)


def _rope_tables(positions):
    B, T = positions.shape
    half_a = MLA_ROPE // 2
    stride = RET_DK // MLA_ROPE
    inv_r = 1.0 / (ROPE_BASE ** (jnp.arange(0, RET_DK, 2, dtype=F32) / RET_DK))
    assert inv_r.shape[0] == LANES and stride * half_a == LANES
    pick = (jnp.arange(LANES)[:, None] == stride * jnp.arange(half_a)[None, :]).astype(BF16)
    zeros = jnp.zeros((LANES, LANES - 2 * half_a), BF16)
    sel_c = jnp.concatenate([pick, pick, zeros], axis=1)
    sel_s = jnp.concatenate([-pick, pick, zeros], axis=1)
    tile3 = lambda m: jnp.concatenate([m, m, m], axis=0)
    tt = min(T, 1024)
    tab = jax.ShapeDtypeStruct((B, T, LANES), F32)
    tab_t = jax.ShapeDtypeStruct((B, half_a, T), F32)
    blk = pl.BlockSpec((1, tt, LANES), lambda b, t: (b, t, 0))
    blk_t = pl.BlockSpec((1, half_a, tt), lambda b, t: (b, 0, t))
    return pl.pallas_call(
        _rope_tab_kernel,
        grid=(B, T // tt),
        in_specs=[pl.BlockSpec((1, 1, tt), lambda b, t: (b, 0, t)),
                  _const_spec((1, LANES)), _const_spec((3 * LANES, LANES)),
                  _const_spec((3 * LANES, LANES)), _const_spec((half_a, 3 * LANES))],
        out_specs=[blk, blk, blk_t, blk_t, blk, blk],
        out_shape=[tab, tab, tab_t, tab_t, tab, tab],
        compiler_params=_params("parallel", "parallel"),
        name="rope_tables",
    )(positions.reshape(B, 1, T), inv_r[None], tile3(sel_c), tile3(sel_s), tile3(pick).T)


def _rope_pad(v, c, s):
    q = LANES // 4
    return v * c + (pltpu.roll(v, 3 * q, 1) + pltpu.roll(v, q, 1)) * s


def _even_front_kernel(x_ref, g_ref, wlat_ref, gq_ref, wuqt_ref, gkv_ref, wuvt_ref,
                       ca_ref, sa_ref, cat_ref, sat_ref, qt_ref, kl_ref, vt_ref):
    q_scale = (MLA_NOPE + MLA_ROPE) ** -0.5 * LOG2E
    half = MLA_ROPE // 2
    for j, rows in enumerate(_sub_tiles(x_ref.shape[1])):
        h = _rms(x_ref[0, rows, :], g_ref[...]).astype(BF16)
        lat = _dot(h, wlat_ref[...])
        cq = _rms(lat[:, :MLA_Q_LORA], gq_ref[...]).astype(BF16)
        qt = _dot_nt(wuqt_ref[...], cq)
        ckv = _rms(lat[:, MLA_Q_LORA:MLA_Q_LORA + MLA_KV_LORA], gkv_ref[...]).astype(BF16)
        kl_ref[0, rows, :MLA_KV_LORA] = ckv
        vt_ref[0, :, rows] = _dot_nt(wuvt_ref[...], ckv).astype(BF16)
        kl_ref[0, rows, MLA_KV_LORA:] = _rope_pad(lat[:, MLA_Q_LORA + MLA_KV_LORA:],
                                                  ca_ref[0, rows, :], sa_ref[0, rows, :]).astype(BF16)
        ct = cat_ref[0, :, rows]
        st = sat_ref[0, :, rows]
        qt_ref[0, j, :MIX_A, :] = (qt[:MIX_A] * q_scale).astype(BF16)
        for hd in range(MLA_HEADS):
            r1 = MIX_A + hd * MLA_ROPE
            r2 = r1 + half
            r3 = r2 + half
            x1 = qt[r1:r2]
            x2 = qt[r2:r3]
            qt_ref[0, j, r1:r2, :] = ((x1 * ct - x2 * st) * q_scale).astype(BF16)
            qt_ref[0, j, r2:r3, :] = ((x2 * ct + x1 * st) * q_scale).astype(BF16)


def _q_fold_kernel(uk_ref, uq_ref, o_ref):
    o_ref[...] = lax.dot_general(uk_ref[...], uq_ref[...], (((1,), (1,)), ((), ())),
                                 precision=lax.Precision.HIGHEST,
                                 preferred_element_type=F32).astype(BF16)


def _q_fold(uq_nope, uk):
    return pl.pallas_call(
        _q_fold_kernel,
        grid=(MLA_HEADS,),
        in_specs=[pl.BlockSpec((MLA_KV_LORA, MLA_NOPE), lambda h: (0, h)),
                  pl.BlockSpec((MLA_Q_LORA, MLA_NOPE), lambda h: (0, h))],
        out_specs=pl.BlockSpec((MLA_KV_LORA, MLA_Q_LORA), lambda h: (h, 0)),
        out_shape=jax.ShapeDtypeStruct((MLA_HEADS * MLA_KV_LORA, MLA_Q_LORA), BF16),
        compiler_params=_params("parallel"),
        name="q_fold",
    )(uk, uq_nope)


def _even_front(x, g, wlat, gq, wuqt, gkv, wuvt, ca, sa, cat, sat):
    B, T, D = x.shape
    tm = min(T, ROW_TILE)
    sub = min(tm, SUB_TILE)
    row = lambda w: pl.BlockSpec((1, tm, w), lambda b, t: (b, t, 0))
    col = lambda r: pl.BlockSpec((1, r, tm), lambda b, t: (b, 0, t))
    consts = [g, wlat, gq, wuqt, gkv, wuvt]
    return pl.pallas_call(
        _even_front_kernel,
        grid=(B, T // tm),
        in_specs=[row(D)] + [_const_spec(c.shape) for c in consts]
        + [row(LANES), row(LANES), col(MLA_ROPE // 2), col(MLA_ROPE // 2)],
        out_specs=[pl.BlockSpec((1, tm // sub, Q_ROWS, sub), lambda b, t: (b, t, 0, 0)),
                   row(HEAD_PAD), col(MIX_A)],
        out_shape=[jax.ShapeDtypeStruct((B, T // sub, Q_ROWS, sub), BF16),
                   jax.ShapeDtypeStruct((B, T, HEAD_PAD), BF16),
                   jax.ShapeDtypeStruct((B, MIX_A, T), BF16)],
        compiler_params=_params("parallel", "parallel"),
        name="even_front",
    )(x, *consts, ca, sa, cat, sat)


def _attn_kernel(qn_ref, qr_ref, kl_ref, vt_ref, o_ref, s_ref, m_ref):
    n_q, tq = qn_ref.shape[1], qn_ref.shape[3]
    grp_blocks, n_kc, kc = s_ref.shape[1], s_ref.shape[2], s_ref.shape[3]
    n_blk = n_q * (qn_ref.shape[2] // MLA_NOPE)
    q_pad = jnp.zeros((HEAD_PAD - MLA_NOPE - MLA_ROPE, tq), BF16)
    n_grp = n_blk // grp_blocks
    sub = (kc // SUBLANES, SUBLANES, tq)

    def score_chunk(g, j, c, m8):
        hd, qb = divmod(g * grp_blocks + j, n_q)
        qt = jnp.concatenate([qn_ref[0, qb, hd * MLA_NOPE:(hd + 1) * MLA_NOPE, :],
                              qr_ref[0, qb, hd * MLA_ROPE:(hd + 1) * MLA_ROPE, :], q_pad], axis=0)
        s = _dot(kl_ref[0, c * kc:(c + 1) * kc, :], qt)
        s_ref[g % 2, j, c] = s
        cm = jnp.max(s.reshape(sub), axis=0)
        return cm if m8 is None else jnp.maximum(m8, cm)

    def value_chunk(g, j, c, m, l8, acc):
        hd = (g * grp_blocks + j) // n_q
        p = jnp.exp2(s_ref[g % 2, j, c] - m)
        ps = jnp.sum(p.reshape(sub), axis=0)
        vt = vt_ref[0, hd * MLA_V:(hd + 1) * MLA_V, c * kc:(c + 1) * kc]
        pv = _dot(vt, p.astype(BF16))
        return (ps if l8 is None else l8 + ps), (pv if acc is None else acc + pv)

    def stage(k):
        run_v = k >= 1
        run_s = k < n_grp
        blocks = range(grp_blocks)
        if run_v:
            m = [jnp.max(m_ref[(k - 1) % 2, j], axis=0, keepdims=True) for j in blocks]
        m8 = [None] * grp_blocks
        l8 = [None] * grp_blocks
        acc = [None] * grp_blocks
        for c in range(n_kc):
            for j in blocks:
                if run_v:
                    l8[j], acc[j] = value_chunk(k - 1, j, c, m[j], l8[j], acc[j])
                if run_s:
                    m8[j] = score_chunk(k, j, c, m8[j])
        for j in blocks:
            if run_s:
                m_ref[k % 2, j] = m8[j]
            if run_v:
                hd, qb = divmod((k - 1) * grp_blocks + j, n_q)
                l = jnp.sum(l8[j], axis=0, keepdims=True)
                o_ref[0, qb * tq:(qb + 1) * tq, hd * MLA_V:(hd + 1) * MLA_V] = (
                    (acc[j] / l).T.astype(BF16))

    one = jnp.minimum(pl.program_id(0) + 1, 1)
    for k in range(n_grp + 1):
        lax.fori_loop(0, one, lambda _, carry, k=k: (stage(k), carry)[1], 0)


def _attention(qt, kl, vt):
    B, T, _ = kl.shape
    n_q, tq = qt.shape[1], qt.shape[3]
    kc = min(T, ATTN_KV_CHUNK)
    hs = ATTN_HEADS_PER_STEP
    rope_blk0 = MIX_A // (hs * MLA_ROPE)
    return pl.pallas_call(
        _attn_kernel,
        grid=(B, MLA_HEADS // hs),
        in_specs=[pl.BlockSpec((1, n_q, hs * MLA_NOPE, tq), lambda b, h: (b, 0, h, 0)),
                  pl.BlockSpec((1, n_q, hs * MLA_ROPE, tq), lambda b, h: (b, 0, rope_blk0 + h, 0)),
                  pl.BlockSpec((1, T, HEAD_PAD), lambda b, h: (b, 0, 0)),
                  pl.BlockSpec((1, hs * MLA_V, T), lambda b, h: (b, h, 0))],
        out_specs=pl.BlockSpec((1, T, hs * MLA_V), lambda b, h: (b, 0, h)),
        out_shape=jax.ShapeDtypeStruct((B, T, MIX_A), BF16),
        scratch_shapes=[pltpu.VMEM((2, ATTN_GROUP, T // kc, kc, tq), F32),
                        pltpu.VMEM((2, ATTN_GROUP, SUBLANES, tq), F32)],
        compiler_params=_params("parallel", "parallel"),
        name="mla_attention",
    )(qt, qt, kl, vt)


def _pool_fold_kernel(wu_ref, pw_ref, ps_ref, o_ref):
    w = lax.dot_general(wu_ref[...], pw_ref[0], (((1,), (0,)), ((), ())),
                        precision=lax.Precision.HIGHEST, preferred_element_type=F32)
    o_ref[...] = (w * ps_ref[...]).astype(BF16)


def _pool_fold(wu, pw, ps):
    D = wu.shape[0]
    n_g = len(POOL_WINDOWS)
    return pl.pallas_call(
        _pool_fold_kernel,
        grid=(n_g,),
        in_specs=[pl.BlockSpec((D, POOL_DIM), lambda g: (0, g)),
                  pl.BlockSpec((1, POOL_DIM, POOL_DIM), lambda g: (g, 0, 0)),
                  pl.BlockSpec((1, POOL_DIM), lambda g: (0, g))],
        out_specs=pl.BlockSpec((D, POOL_DIM), lambda g: (0, g)),
        out_shape=jax.ShapeDtypeStruct((D, MIX_B), BF16),
        compiler_params=_params("parallel"),
        name="pool_fold",
    )(wu, pw, ps)


def _even_back_kernel(x_ref, xp_ref, xn_ref, g_ref, wu_ref, wg_ref, a_ref, wo_ref, o_ref, *, seq_len):
    tm = x_ref.shape[1]
    g = g_ref[...]
    wu = wu_ref[...]
    for rows in _sub_tiles(tm):
        sub = rows.stop - rows.start
        t0 = pl.program_id(1) * tm + rows.start
        x = x_ref[0, rows, :]
        h = _rms(x, g).astype(BF16)
        xp = xp_ref[0] if rows.start == 0 else x_ref[0, rows.start - POOL_HALO:rows.start, :]
        xn = xn_ref[0] if rows.stop == tm else x_ref[0, rows.stop:rows.stop + POOL_HALO, :]
        hp = _rms(xp, g).astype(BF16)
        hn = _rms(xn, g).astype(BF16)
        up = jnp.where(t0 > 0, _dot(hp, wu), 0.0)
        un = jnp.where(t0 + sub < seq_len, _dot(hn, wu), 0.0)
        u = _dot(h, wu)
        ue = jnp.concatenate([up, u, un], axis=0)
        ext = sub + 2 * POOL_HALO
        t = (t0 + lax.broadcasted_iota(jnp.int32, (sub, 1), 0))
        gate = _dot(h, wg_ref[...])
        sg = _silu(gate)
        ya = (a_ref[0, rows, :].astype(F32) * sg[:, :MIX_A]).astype(BF16)
        y = _dot(ya, wo_ref[:MIX_A, :])
        for gi, w in enumerate(POOL_WINDOWS):
            left = w // 2
            right = w - 1 - left
            cols = slice(gi * POOL_DIM, (gi + 1) * POOL_DIM)
            run = ue[:, cols]
            n = 1
            while n < left:
                run = run + pltpu.roll(run, ext - n, 0)
                n *= 2
            acc = (run + pltpu.roll(run, left, 0))[POOL_HALO:POOL_HALO + sub]
            cnt = (jnp.minimum(t + right, seq_len - 1) - jnp.maximum(t - left, 0) + 1).astype(F32)
            bg = acc / cnt - u[:, cols]
            yb = (bg * sg[:, MIX_A + gi * POOL_DIM:MIX_A + (gi + 1) * POOL_DIM]).astype(BF16)
            y = y + _dot(yb, wo_ref[MIX_A + gi * POOL_DIM:MIX_A + (gi + 1) * POOL_DIM, :])
        o_ref[0, rows, :] = x + y


def _even_back(x, g, wu, wg, a, wo_all, layer):
    B, T, D = x.shape
    tm = min(T, ROW_TILE)
    nb = tm // POOL_HALO
    last = T // POOL_HALO - 1
    row = lambda w: pl.BlockSpec((1, tm, w), lambda b, t: (b, t, 0))
    return pl.pallas_call(
        functools.partial(_even_back_kernel, seq_len=T),
        grid=(B, T // tm),
        in_specs=[row(D),
                  pl.BlockSpec((1, POOL_HALO, D), lambda b, t: (b, jnp.maximum(t * nb - 1, 0), 0)),
                  pl.BlockSpec((1, POOL_HALO, D), lambda b, t: (b, jnp.minimum((t + 1) * nb, last), 0)),
                  _const_spec(g.shape), _const_spec(wu.shape), _const_spec(wg.shape), row(MIX_A),
                  _layer_spec(wo_all.shape[1:], layer)],
        out_specs=row(D),
        out_shape=jax.ShapeDtypeStruct((B, T, D), F32),
        compiler_params=_params("parallel", "parallel"),
        name="even_back",
    )(x, x, x, g, wu, wg, a, wo_all)


def _odd_front_kernel(x_ref, g_ref, w_ref, c_ref, s_ref, q_ref, k_ref, v_ref):
    half = RET_DK // 2
    k_scale = RET_DK ** -0.5
    for rows in _sub_tiles(x_ref.shape[1]):
        h = _rms(x_ref[0, rows, :], g_ref[...]).astype(BF16)
        qkv = _dot(h, w_ref[...])
        c = c_ref[0, rows, :]
        s = s_ref[0, rows, :]
        for hd in range(RET_HEADS):
            for base, ref, scale in ((0, q_ref, None), (RET_QK, k_ref, k_scale)):
                lo = base + hd * RET_DK
                x1 = qkv[:, lo:lo + half]
                x2 = qkv[:, lo + half:lo + RET_DK]
                o1 = x1 * c - x2 * s
                o2 = x2 * c + x1 * s
                if scale is not None:
                    o1 = o1 * scale
                    o2 = o2 * scale
                ref[0, rows, hd * RET_DK:hd * RET_DK + half] = o1.astype(BF16)
                ref[0, rows, hd * RET_DK + half:(hd + 1) * RET_DK] = o2.astype(BF16)
        v_ref[0, rows, :] = qkv[:, 2 * RET_QK:].astype(BF16)


def _odd_front(x, g, w_all, layer, c, s):
    B, T, D = x.shape
    tm = min(T, ROW_TILE)
    row = lambda wd: pl.BlockSpec((1, tm, wd), lambda b, t: (b, t, 0))
    qk = jax.ShapeDtypeStruct((B, T, RET_QK), BF16)
    return pl.pallas_call(
        _odd_front_kernel,
        grid=(B, T // tm),
        in_specs=[row(D), _const_spec(g.shape), _layer_spec((D, 2 * RET_QK + RET_V), layer),
                  row(LANES), row(LANES)],
        out_specs=[row(RET_QK), row(RET_QK), row(RET_V)],
        out_shape=[qk, qk, jax.ShapeDtypeStruct((B, T, RET_V), BF16)],
        compiler_params=_params("parallel", "parallel"),
        name="odd_front",
    )(x, g, w_all, c, s)


def _log_sigmoid(x):
    return jnp.minimum(x, 0.0) - jnp.log1p(jnp.exp(-jnp.abs(x)))


def _retention_kernel(q_ref, k_ref, v_ref, df_ref, db_ref, gn_ref, o_ref, sf_ref, acc_ref):
    T = q_ref.shape[1]
    C = min(T, RET_CHUNK)
    n_chunks = T // C
    heads = range(q_ref.shape[2] // RET_DK)
    ri = lax.broadcasted_iota(jnp.int32, (C, 1), 0).astype(F32)
    diff = (lax.broadcasted_iota(jnp.int32, (C, C), 0)
            - lax.broadcasted_iota(jnp.int32, (C, C), 1)).astype(F32)

    def decays(hd):
        lf1 = _log_sigmoid(df_ref[hd])[:, :1]
        lb1 = _log_sigmoid(db_ref[hd])[:, :1]
        return dict(
            dmat=jnp.exp(jnp.where(diff >= 0, diff * lf1, -diff * lb1)),
            xi_f=jnp.exp((ri + 1.0) * lf1),
            xi_b=jnp.exp((C - ri) * lb1),
            zeta_f=jnp.exp((C - 1.0 - ri) * lf1),
            zeta_b=jnp.exp(ri * lb1),
            cd_f=jnp.exp(C * lf1), cd_b=jnp.exp(C * lb1))

    dec = [decays(hd) for hd in heads]

    def state_update(hd, i, zeta, cd, first):
        rows = slice(i * C, (i + 1) * C)
        kz = (k_ref[0, rows, hd * RET_DK:(hd + 1) * RET_DK].astype(F32) * zeta).astype(BF16)
        upd = _dot_tn(kz, v_ref[0, rows, hd * RET_DV:(hd + 1) * RET_DV])
        acc_ref[hd] = upd if first else acc_ref[hd] * cd + upd

    for i in range(n_chunks - 1):
        for hd in heads:
            state_update(hd, i, dec[hd]["zeta_f"], dec[hd]["cd_f"], first=(i == 0))
            sf_ref[hd, i + 1] = acc_ref[hd].astype(BF16)

    for i in reversed(range(n_chunks)):
        rows = slice(i * C, (i + 1) * C)
        for hd in heads:
            d = dec[hd]
            vcols = slice(hd * RET_DV, (hd + 1) * RET_DV)
            q = q_ref[0, rows, hd * RET_DK:(hd + 1) * RET_DK]
            qf = q.astype(F32)
            s = (_dot_nt(q, k_ref[0, rows, hd * RET_DK:(hd + 1) * RET_DK]) * d["dmat"]).astype(BF16)
            o = _dot(s, v_ref[0, rows, vcols])
            if i > 0:
                o = o + _dot((qf * d["xi_f"]).astype(BF16), sf_ref[hd, i])
            if i < n_chunks - 1:
                o = o + _dot((qf * d["xi_b"]).astype(BF16), acc_ref[hd].astype(BF16))
            mu = jnp.mean(o, axis=-1, keepdims=True)
            oc = o - mu
            var = jnp.mean(oc * oc, axis=-1, keepdims=True)
            o_ref[0, rows, vcols] = (oc * lax.rsqrt(var + NORM_EPS) * gn_ref[:, vcols]).astype(BF16)
            if i > 0:
                state_update(hd, i, d["zeta_b"], d["cd_b"], first=(i == n_chunks - 1))


def _retention(q, k, v, dec_f, dec_b, gn_g):
    B, T, _ = q.shape
    C = min(T, RET_CHUNK)
    hs = RET_HEADS_PER_STEP
    head = lambda w: pl.BlockSpec((1, T, hs * w), lambda b, h: (b, 0, h))
    dec = pl.BlockSpec((hs, 1, LANES), lambda b, h: (h, 0, 0))
    return pl.pallas_call(
        _retention_kernel,
        grid=(B, RET_HEADS // hs),
        in_specs=[head(RET_DK), head(RET_DK), head(RET_DV), dec, dec,
                  pl.BlockSpec((1, hs * RET_DV), lambda b, h: (0, h))],
        out_specs=head(RET_DV),
        out_shape=jax.ShapeDtypeStruct((B, T, RET_V), BF16),
        scratch_shapes=[pltpu.VMEM((hs, T // C, RET_DK, RET_DV), BF16),
                        pltpu.VMEM((hs, RET_DK, RET_DV), F32)],
        compiler_params=_params("parallel", "parallel"),
        name="retention",
    )(q, k, v, dec_f, dec_b, gn_g)


def _odd_back_kernel(x_ref, g_ref, wg_ref, a_ref, wo_ref, fg_ref, o_ref, *, final_norm):
    for rows in _sub_tiles(x_ref.shape[1]):
        x = x_ref[0, rows, :]
        h = _rms(x, g_ref[...]).astype(BF16)
        gate = _dot(h, wg_ref[...])
        y = (a_ref[0, rows, :].astype(F32) * _silu(gate)).astype(BF16)
        out = x + _dot(y, wo_ref[...])
        if final_norm:
            out = _rms(out, fg_ref[...])
        o_ref[0, rows, :] = out


def _odd_back(x, g, w_all, a, wo_all, layer, fg, final_norm):
    B, T, D = x.shape
    tm = min(T, ROW_TILE)
    row = lambda w: pl.BlockSpec((1, tm, w), lambda b, t: (b, t, 0))
    return pl.pallas_call(
        functools.partial(_odd_back_kernel, final_norm=final_norm),
        grid=(B, T // tm),
        in_specs=[row(D), _const_spec(g.shape),
                  _layer_spec((D, RET_V), layer, col_block=(2 * RET_QK + RET_V) // RET_V),
                  row(RET_V), _layer_spec(wo_all.shape[1:], layer), _const_spec(fg.shape)],
        out_specs=row(D),
        out_shape=jax.ShapeDtypeStruct((B, T, D), F32),
        compiler_params=_params("parallel", "parallel"),
        name="odd_back",
    )(x, g, w_all, a, wo_all, fg)


def _even_weights(w_in, w_uq, w_ukv):
    n_lat = MLA_Q_LORA + MLA_KV_LORA
    wlat = jnp.concatenate(
        [w_in[:, :n_lat + MLA_ROPE], jnp.zeros((D_MODEL, LANES - MLA_ROPE), w_in.dtype)], axis=1)
    wu = w_in[:, n_lat + MLA_ROPE:n_lat + MLA_ROPE + MIX_B]
    wg = w_in[:, n_lat + MLA_ROPE + MIX_B:]
    uq = w_uq.reshape(MLA_Q_LORA, MLA_HEADS, MLA_NOPE + MLA_ROPE)
    uq_nope = uq[:, :, :MLA_NOPE].reshape(MLA_Q_LORA, MIX_A)
    uq_rope_t = uq[:, :, MLA_NOPE:].reshape(MLA_Q_LORA, MLA_HEADS * MLA_ROPE).T
    ukv = w_ukv.reshape(MLA_KV_LORA, MLA_HEADS, MLA_NOPE + MLA_V)
    uk = ukv[:, :, :MLA_NOPE].reshape(MLA_KV_LORA, MIX_A)
    uvt = ukv[:, :, MLA_NOPE:].reshape(MLA_KV_LORA, MIX_A).T
    return (wu, uq_nope, uk) + tuple(w.astype(BF16) for w in (wlat, wg, uq_rope_t, uvt))


def kernel(x, positions, a_norm_g, a_w_in, a_q_norm_g, a_w_uq, a_kv_norm_g, a_w_ukv, a_pool_w,
           a_pool_scale, a_w_out, r_norm_g, r_w_in, r_decay_fwd, r_decay_bwd, r_gn_g, r_w_out,
           final_norm_g):
    depth = a_norm_g.shape[0] + r_norm_g.shape[0]
    assert depth % 2 == 0, "the final norm is fused into the last (odd) layer's back kernel"
    ca, sa, cat, sat, cr, sr = _rope_tables(positions)
    fg = final_norm_g[None]
    a_w_out, r_w_in, r_w_out = (w.astype(BF16) for w in (a_w_out, r_w_in, r_w_out))
    for layer in range(depth):
        i = layer // 2
        if layer % 2 == 0:
            wu, uq_nope, uk, wlat, wg, uq_rope_t, wuvt = _even_weights(a_w_in[i], a_w_uq[i], a_w_ukv[i])
            g = a_norm_g[i][None]
            wuqt = jnp.concatenate([_q_fold(uq_nope, uk), uq_rope_t], axis=0)
            qt, kl, vt = _even_front(x, g, wlat, a_q_norm_g[i][None], wuqt, a_kv_norm_g[i][None],
                                     wuvt, ca, sa, cat, sat)
            a = _attention(qt, kl, vt)
            wu = _pool_fold(wu, a_pool_w[i], a_pool_scale[i][None])
            x = _even_back(x, g, wu, wg, a, a_w_out, i)
        else:
            g = r_norm_g[i][None]
            q, k, v = _odd_front(x, g, r_w_in, i, cr, sr)
            dec_f = jnp.broadcast_to(r_decay_fwd[i][:, None, None], (RET_HEADS, 1, LANES))
            dec_b = jnp.broadcast_to(r_decay_bwd[i][:, None, None], (RET_HEADS, 1, LANES))
            o = _retention(q, k, v, dec_f, dec_b, r_gn_g[i][None])
            x = _odd_back(x, g, r_w_in, o, r_w_out, i, fg, final_norm=(layer == depth - 1))
    return x
```

```python
import functools

import jax
import jax.numpy as jnp
from jax import lax
from jax.experimental import pallas as pl
from jax.experimental.pallas import tpu as pltpu

ROPE_BASE = 10000.0
NORM_EPS = 1e-6
LOG2E = 1.4426950408889634

D_MODEL = 1024
MLA_HEADS = 8
MLA_NOPE = 128
MLA_ROPE = 64
MLA_V = 128
MLA_Q_LORA = 384
MLA_KV_LORA = 128
POOL_WINDOWS = (2, 4, 8, 16)
POOL_DIM = 256
MIX_A = MLA_HEADS * MLA_V
MIX_B = len(POOL_WINDOWS) * POOL_DIM
RET_HEADS = 4
RET_DK = 256
RET_DV = 512
RET_QK = RET_HEADS * RET_DK
RET_V = RET_HEADS * RET_DV

LANES = 128
SUBLANES = 8
HEAD_PAD = 2 * LANES
Q_ROWS = MLA_HEADS * (MLA_NOPE + MLA_ROPE)
POOL_HALO = SUBLANES
VMEM_LIMIT = 56 * 1024 * 1024

ROW_TILE = 1024
SUB_TILE = 512
ATTN_KV_CHUNK = 256
ATTN_HEADS_PER_STEP = 8
ATTN_GROUP = 2
RET_CHUNK = 256
RET_HEADS_PER_STEP = 2

BF16 = jnp.bfloat16
F32 = jnp.float32


def _params(*sem, **kw):
    return pltpu.CompilerParams(dimension_semantics=sem, vmem_limit_bytes=VMEM_LIMIT, **kw)


def _const_spec(shape):
    nd = len(shape)
    return pl.BlockSpec(shape, lambda *_: (0,) * nd, pipeline_mode=pl.Buffered(1))


def _layer_spec(block, layer, col_block=0):
    idx = (layer,) + (0,) * (len(block) - 1) + (col_block,)
    return pl.BlockSpec((None,) + tuple(block), lambda *_: idx, pipeline_mode=pl.Buffered(1))


def _sub_tiles(rows):
    sub = min(rows, SUB_TILE)
    return [slice(r, r + sub) for r in range(0, rows, sub)]


def _rms(x, g):
    return x * lax.rsqrt(jnp.mean(x * x, axis=-1, keepdims=True) + NORM_EPS) * g


def _silu(x):
    return x / (1.0 + jnp.exp(-x))


def _dot(a, b):
    return jnp.dot(a, b, preferred_element_type=F32)


def _dot_nt(a, b):
    return lax.dot_general(a, b, (((1,), (1,)), ((), ())), preferred_element_type=F32)


def _dot_tn(a, b):
    return lax.dot_general(a, b, (((0,), (0,)), ((), ())), preferred_element_type=F32)


def _split3(a):
    hi = a.astype(BF16)
    r1 = a - hi.astype(F32)
    mid = r1.astype(BF16)
    lo = (r1 - mid.astype(F32)).astype(BF16)
    return jnp.concatenate([hi, mid, lo], axis=1)


def _rope_tab_kernel(pos_ref, inv_r_ref, selc_ref, sels_ref, selt_ref,
                     ca_ref, sa_ref, cat_ref, sat_ref, cr_ref, sr_ref):
    tt = pos_ref.shape[2]
    sub = min(tt, 256)
    for r in range(0, tt, sub):
        rows = slice(r, r + sub)
        pos = jnp.broadcast_to(pos_ref[0, :, rows].astype(F32), (SUBLANES, sub)).T[:, :1]
        ang = pos * inv_r_ref[...]
        c = jnp.cos(ang)
        s = jnp.sin(ang)
        cr_ref[0, rows, :] = c
        sr_ref[0, rows, :] = s
        c3 = _split3(c)
        s3 = _split3(s)
        ca_ref[0, rows, :] = _dot(c3, selc_ref[...])
        sa_ref[0, rows, :] = _dot(s3, sels_ref[...])
        cat_ref[0, :, rows] = _dot_nt(selt_ref[...], c3)
        sat_ref[0, :, rows] = _dot_nt(selt_ref[...], s3)


def _rope_tables(positions):
    B, T = positions.shape
    half_a = MLA_ROPE // 2
    stride = RET_DK // MLA_ROPE
    inv_r = 1.0 / (ROPE_BASE ** (jnp.arange(0, RET_DK, 2, dtype=F32) / RET_DK))
    assert inv_r.shape[0] == LANES and stride * half_a == LANES
    pick = (jnp.arange(LANES)[:, None] == stride * jnp.arange(half_a)[None, :]).astype(BF16)
    zeros = jnp.zeros((LANES, LANES - 2 * half_a), BF16)
    sel_c = jnp.concatenate([pick, pick, zeros], axis=1)
    sel_s = jnp.concatenate([-pick, pick, zeros], axis=1)
    tile3 = lambda m: jnp.concatenate([m, m, m], axis=0)
    tt = min(T, 1024)
    tab = jax.ShapeDtypeStruct((B, T, LANES), F32)
    tab_t = jax.ShapeDtypeStruct((B, half_a, T), F32)
    blk = pl.BlockSpec((1, tt, LANES), lambda b, t: (b, t, 0))
    blk_t = pl.BlockSpec((1, half_a, tt), lambda b, t: (b, 0, t))
    return pl.pallas_call(
        _rope_tab_kernel,
        grid=(B, T // tt),
        in_specs=[pl.BlockSpec((1, 1, tt), lambda b, t: (b, 0, t)),
                  _const_spec((1, LANES)), _const_spec((3 * LANES, LANES)),
                  _const_spec((3 * LANES, LANES)), _const_spec((half_a, 3 * LANES))],
        out_specs=[blk, blk, blk_t, blk_t, blk, blk],
        out_shape=[tab, tab, tab_t, tab_t, tab, tab],
        compiler_params=_params("parallel", "parallel"),
        name="rope_tables",
    )(positions.reshape(B, 1, T), inv_r[None], tile3(sel_c), tile3(sel_s), tile3(pick).T)


def _rope_pad(v, c, s):
    q = LANES // 4
    return v * c + (pltpu.roll(v, 3 * q, 1) + pltpu.roll(v, q, 1)) * s


def _even_front_kernel(x_ref, g_ref, wlat_ref, gq_ref, wuqt_ref, gkv_ref, wuvt_ref,
                       ca_ref, sa_ref, cat_ref, sat_ref, qt_ref, kl_ref, vt_ref):
    q_scale = (MLA_NOPE + MLA_ROPE) ** -0.5 * LOG2E
    half = MLA_ROPE // 2
    for j, rows in enumerate(_sub_tiles(x_ref.shape[1])):
        h = _rms(x_ref[0, rows, :], g_ref[...]).astype(BF16)
        lat = _dot(h, wlat_ref[...])
        cq = _rms(lat[:, :MLA_Q_LORA], gq_ref[...]).astype(BF16)
        qt = _dot_nt(wuqt_ref[...], cq)
        ckv = _rms(lat[:, MLA_Q_LORA:MLA_Q_LORA + MLA_KV_LORA], gkv_ref[...]).astype(BF16)
        kl_ref[0, rows, :MLA_KV_LORA] = ckv
        vt_ref[0, :, rows] = _dot_nt(wuvt_ref[...], ckv).astype(BF16)
        kl_ref[0, rows, MLA_KV_LORA:] = _rope_pad(lat[:, MLA_Q_LORA + MLA_KV_LORA:],
                                                  ca_ref[0, rows, :], sa_ref[0, rows, :]).astype(BF16)
        ct = cat_ref[0, :, rows]
        st = sat_ref[0, :, rows]
        qt_ref[0, j, :MIX_A, :] = (qt[:MIX_A] * q_scale).astype(BF16)
        for hd in range(MLA_HEADS):
            r1 = MIX_A + hd * MLA_ROPE
            r2 = r1 + half
            r3 = r2 + half
            x1 = qt[r1:r2]
            x2 = qt[r2:r3]
            qt_ref[0, j, r1:r2, :] = ((x1 * ct - x2 * st) * q_scale).astype(BF16)
            qt_ref[0, j, r2:r3, :] = ((x2 * ct + x1 * st) * q_scale).astype(BF16)


def _q_fold_kernel(uk_ref, uq_ref, o_ref):
    o_ref[...] = lax.dot_general(uk_ref[...], uq_ref[...], (((1,), (1,)), ((), ())),
                                 precision=lax.Precision.HIGHEST,
                                 preferred_element_type=F32).astype(BF16)


def _q_fold(uq_nope, uk):
    return pl.pallas_call(
        _q_fold_kernel,
        grid=(MLA_HEADS,),
        in_specs=[pl.BlockSpec((MLA_KV_LORA, MLA_NOPE), lambda h: (0, h)),
                  pl.BlockSpec((MLA_Q_LORA, MLA_NOPE), lambda h: (0, h))],
        out_specs=pl.BlockSpec((MLA_KV_LORA, MLA_Q_LORA), lambda h: (h, 0)),
        out_shape=jax.ShapeDtypeStruct((MLA_HEADS * MLA_KV_LORA, MLA_Q_LORA), BF16),
        compiler_params=_params("parallel"),
        name="q_fold",
    )(uk, uq_nope)


def _even_front(x, g, wlat, gq, wuqt, gkv, wuvt, ca, sa, cat, sat):
    B, T, D = x.shape
    tm = min(T, ROW_TILE)
    sub = min(tm, SUB_TILE)
    row = lambda w: pl.BlockSpec((1, tm, w), lambda b, t: (b, t, 0))
    col = lambda r: pl.BlockSpec((1, r, tm), lambda b, t: (b, 0, t))
    consts = [g, wlat, gq, wuqt, gkv, wuvt]
    return pl.pallas_call(
        _even_front_kernel,
        grid=(B, T // tm),
        in_specs=[row(D)] + [_const_spec(c.shape) for c in consts]
        + [row(LANES), row(LANES), col(MLA_ROPE // 2), col(MLA_ROPE // 2)],
        out_specs=[pl.BlockSpec((1, tm // sub, Q_ROWS, sub), lambda b, t: (b, t, 0, 0)),
                   row(HEAD_PAD), col(MIX_A)],
        out_shape=[jax.ShapeDtypeStruct((B, T // sub, Q_ROWS, sub), BF16),
                   jax.ShapeDtypeStruct((B, T, HEAD_PAD), BF16),
                   jax.ShapeDtypeStruct((B, MIX_A, T), BF16)],
        compiler_params=_params("parallel", "parallel"),
        name="even_front",
    )(x, *consts, ca, sa, cat, sat)


def _attn_kernel(qn_ref, qr_ref, kl_ref, vt_ref, o_ref, s_ref, m_ref):
    n_q, tq = qn_ref.shape[1], qn_ref.shape[3]
    grp_blocks, n_kc, kc = s_ref.shape[1], s_ref.shape[2], s_ref.shape[3]
    n_blk = n_q * (qn_ref.shape[2] // MLA_NOPE)
    q_pad = jnp.zeros((HEAD_PAD - MLA_NOPE - MLA_ROPE, tq), BF16)
    n_grp = n_blk // grp_blocks
    sub = (kc // SUBLANES, SUBLANES, tq)

    def score_chunk(g, j, c, m8):
        hd, qb = divmod(g * grp_blocks + j, n_q)
        qt = jnp.concatenate([qn_ref[0, qb, hd * MLA_NOPE:(hd + 1) * MLA_NOPE, :],
                              qr_ref[0, qb, hd * MLA_ROPE:(hd + 1) * MLA_ROPE, :], q_pad], axis=0)
        s = _dot(kl_ref[0, c * kc:(c + 1) * kc, :], qt)
        s_ref[g % 2, j, c] = s
        cm = jnp.max(s.reshape(sub), axis=0)
        return cm if m8 is None else jnp.maximum(m8, cm)

    def value_chunk(g, j, c, m, l8, acc):
        hd = (g * grp_blocks + j) // n_q
        p = jnp.exp2(s_ref[g % 2, j, c] - m)
        ps = jnp.sum(p.reshape(sub), axis=0)
        vt = vt_ref[0, hd * MLA_V:(hd + 1) * MLA_V, c * kc:(c + 1) * kc]
        pv = _dot(vt, p.astype(BF16))
        return (ps if l8 is None else l8 + ps), (pv if acc is None else acc + pv)

    def stage(k):
        run_v = k >= 1
        run_s = k < n_grp
        blocks = range(grp_blocks)
        if run_v:
            m = [jnp.max(m_ref[(k - 1) % 2, j], axis=0, keepdims=True) for j in blocks]
        m8 = [None] * grp_blocks
        l8 = [None] * grp_blocks
        acc = [None] * grp_blocks
        for c in range(n_kc):
            for j in blocks:
                if run_v:
                    l8[j], acc[j] = value_chunk(k - 1, j, c, m[j], l8[j], acc[j])
                if run_s:
                    m8[j] = score_chunk(k, j, c, m8[j])
        for j in blocks:
            if run_s:
                m_ref[k % 2, j] = m8[j]
            if run_v:
                hd, qb = divmod((k - 1) * grp_blocks + j, n_q)
                l = jnp.sum(l8[j], axis=0, keepdims=True)
                o_ref[0, qb * tq:(qb + 1) * tq, hd * MLA_V:(hd + 1) * MLA_V] = (
                    (acc[j] / l).T.astype(BF16))

    one = jnp.minimum(pl.program_id(0) + 1, 1)
    for k in range(n_grp + 1):
        lax.fori_loop(0, one, lambda _, carry, k=k: (stage(k), carry)[1], 0)


def _attention(qt, kl, vt):
    B, T, _ = kl.shape
    n_q, tq = qt.shape[1], qt.shape[3]
    kc = min(T, ATTN_KV_CHUNK)
    hs = ATTN_HEADS_PER_STEP
    rope_blk0 = MIX_A // (hs * MLA_ROPE)
    return pl.pallas_call(
        _attn_kernel,
        grid=(B, MLA_HEADS // hs),
        in_specs=[pl.BlockSpec((1, n_q, hs * MLA_NOPE, tq), lambda b, h: (b, 0, h, 0)),
                  pl.BlockSpec((1, n_q, hs * MLA_ROPE, tq), lambda b, h: (b, 0, rope_blk0 + h, 0)),
                  pl.BlockSpec((1, T, HEAD_PAD), lambda b, h: (b, 0, 0)),
                  pl.BlockSpec((1, hs * MLA_V, T), lambda b, h: (b, h, 0))],
        out_specs=pl.BlockSpec((1, T, hs * MLA_V), lambda b, h: (b, 0, h)),
        out_shape=jax.ShapeDtypeStruct((B, T, MIX_A), BF16),
        scratch_shapes=[pltpu.VMEM((2, ATTN_GROUP, T // kc, kc, tq), F32),
                        pltpu.VMEM((2, ATTN_GROUP, SUBLANES, tq), F32)],
        compiler_params=_params("parallel", "parallel"),
        name="mla_attention",
    )(qt, qt, kl, vt)


def _pool_fold_kernel(wu_ref, pw_ref, ps_ref, o_ref):
    w = lax.dot_general(wu_ref[...], pw_ref[0], (((1,), (0,)), ((), ())),
                        precision=lax.Precision.HIGHEST, preferred_element_type=F32)
    o_ref[...] = (w * ps_ref[...]).astype(BF16)


def _pool_fold(wu, pw, ps):
    D = wu.shape[0]
    n_g = len(POOL_WINDOWS)
    return pl.pallas_call(
        _pool_fold_kernel,
        grid=(n_g,),
        in_specs=[pl.BlockSpec((D, POOL_DIM), lambda g: (0, g)),
                  pl.BlockSpec((1, POOL_DIM, POOL_DIM), lambda g: (g, 0, 0)),
                  pl.BlockSpec((1, POOL_DIM), lambda g: (0, g))],
        out_specs=pl.BlockSpec((D, POOL_DIM), lambda g: (0, g)),
        out_shape=jax.ShapeDtypeStruct((D, MIX_B), BF16),
        compiler_params=_params("parallel"),
        name="pool_fold",
    )(wu, pw, ps)


def _even_back_kernel(x_ref, xp_ref, xn_ref, g_ref, wu_ref, wg_ref, a_ref, wo_ref, o_ref, *, seq_len):
    tm = x_ref.shape[1]
    g = g_ref[...]
    wu = wu_ref[...]
    for rows in _sub_tiles(tm):
        sub = rows.stop - rows.start
        t0 = pl.program_id(1) * tm + rows.start
        x = x_ref[0, rows, :]
        h = _rms(x, g).astype(BF16)
        xp = xp_ref[0] if rows.start == 0 else x_ref[0, rows.start - POOL_HALO:rows.start, :]
        xn = xn_ref[0] if rows.stop == tm else x_ref[0, rows.stop:rows.stop + POOL_HALO, :]
        hp = _rms(xp, g).astype(BF16)
        hn = _rms(xn, g).astype(BF16)
        up = jnp.where(t0 > 0, _dot(hp, wu), 0.0)
        un = jnp.where(t0 + sub < seq_len, _dot(hn, wu), 0.0)
        u = _dot(h, wu)
        ue = jnp.concatenate([up, u, un], axis=0)
        ext = sub + 2 * POOL_HALO
        t = (t0 + lax.broadcasted_iota(jnp.int32, (sub, 1), 0))
        gate = _dot(h, wg_ref[...])
        sg = _silu(gate)
        ya = (a_ref[0, rows, :].astype(F32) * sg[:, :MIX_A]).astype(BF16)
        y = _dot(ya, wo_ref[:MIX_A, :])
        for gi, w in enumerate(POOL_WINDOWS):
            left = w // 2
            right = w - 1 - left
            cols = slice(gi * POOL_DIM, (gi + 1) * POOL_DIM)
            run = ue[:, cols]
            n = 1
            while n < left:
                run = run + pltpu.roll(run, ext - n, 0)
                n *= 2
            acc = (run + pltpu.roll(run, left, 0))[POOL_HALO:POOL_HALO + sub]
            cnt = (jnp.minimum(t + right, seq_len - 1) - jnp.maximum(t - left, 0) + 1).astype(F32)
            bg = acc / cnt - u[:, cols]
            yb = (bg * sg[:, MIX_A + gi * POOL_DIM:MIX_A + (gi + 1) * POOL_DIM]).astype(BF16)
            y = y + _dot(yb, wo_ref[MIX_A + gi * POOL_DIM:MIX_A + (gi + 1) * POOL_DIM, :])
        o_ref[0, rows, :] = x + y


def _even_back(x, g, wu, wg, a, wo_all, layer):
    B, T, D = x.shape
    tm = min(T, ROW_TILE)
    nb = tm // POOL_HALO
    last = T // POOL_HALO - 1
    row = lambda w: pl.BlockSpec((1, tm, w), lambda b, t: (b, t, 0))
    return pl.pallas_call(
        functools.partial(_even_back_kernel, seq_len=T),
        grid=(B, T // tm),
        in_specs=[row(D),
                  pl.BlockSpec((1, POOL_HALO, D), lambda b, t: (b, jnp.maximum(t * nb - 1, 0), 0)),
                  pl.BlockSpec((1, POOL_HALO, D), lambda b, t: (b, jnp.minimum((t + 1) * nb, last), 0)),
                  _const_spec(g.shape), _const_spec(wu.shape), _const_spec(wg.shape), row(MIX_A),
                  _layer_spec(wo_all.shape[1:], layer)],
        out_specs=row(D),
        out_shape=jax.ShapeDtypeStruct((B, T, D), F32),
        compiler_params=_params("parallel", "parallel"),
        name="even_back",
    )(x, x, x, g, wu, wg, a, wo_all)


def _odd_front_kernel(x_ref, g_ref, w_ref, c_ref, s_ref, q_ref, k_ref, v_ref):
    half = RET_DK // 2
    k_scale = RET_DK ** -0.5
    for rows in _sub_tiles(x_ref.shape[1]):
        h = _rms(x_ref[0, rows, :], g_ref[...]).astype(BF16)
        qkv = _dot(h, w_ref[...])
        c = c_ref[0, rows, :]
        s = s_ref[0, rows, :]
        for hd in range(RET_HEADS):
            for base, ref, scale in ((0, q_ref, None), (RET_QK, k_ref, k_scale)):
                lo = base + hd * RET_DK
                x1 = qkv[:, lo:lo + half]
                x2 = qkv[:, lo + half:lo + RET_DK]
                o1 = x1 * c - x2 * s
                o2 = x2 * c + x1 * s
                if scale is not None:
                    o1 = o1 * scale
                    o2 = o2 * scale
                ref[0, rows, hd * RET_DK:hd * RET_DK + half] = o1.astype(BF16)
                ref[0, rows, hd * RET_DK + half:(hd + 1) * RET_DK] = o2.astype(BF16)
        v_ref[0, rows, :] = qkv[:, 2 * RET_QK:].astype(BF16)


def _odd_front(x, g, w_all, layer, c, s):
    B, T, D = x.shape
    tm = min(T, ROW_TILE)
    row = lambda wd: pl.BlockSpec((1, tm, wd), lambda b, t: (b, t, 0))
    qk = jax.ShapeDtypeStruct((B, T, RET_QK), BF16)
    return pl.pallas_call(
        _odd_front_kernel,
        grid=(B, T // tm),
        in_specs=[row(D), _const_spec(g.shape), _layer_spec((D, 2 * RET_QK + RET_V), layer),
                  row(LANES), row(LANES)],
        out_specs=[row(RET_QK), row(RET_QK), row(RET_V)],
        out_shape=[qk, qk, jax.ShapeDtypeStruct((B, T, RET_V), BF16)],
        compiler_params=_params("parallel", "parallel"),
        name="odd_front",
    )(x, g, w_all, c, s)


def _log_sigmoid(x):
    return jnp.minimum(x, 0.0) - jnp.log1p(jnp.exp(-jnp.abs(x)))


def _retention_kernel(q_ref, k_ref, v_ref, df_ref, db_ref, gn_ref, o_ref, sf_ref, acc_ref):
    T = q_ref.shape[1]
    C = min(T, RET_CHUNK)
    n_chunks = T // C
    heads = range(q_ref.shape[2] // RET_DK)
    ri = lax.broadcasted_iota(jnp.int32, (C, 1), 0).astype(F32)
    diff = (lax.broadcasted_iota(jnp.int32, (C, C), 0)
            - lax.broadcasted_iota(jnp.int32, (C, C), 1)).astype(F32)

    def decays(hd):
        lf1 = _log_sigmoid(df_ref[hd])[:, :1]
        lb1 = _log_sigmoid(db_ref[hd])[:, :1]
        return dict(
            dmat=jnp.exp(jnp.where(diff >= 0, diff * lf1, -diff * lb1)),
            xi_f=jnp.exp((ri + 1.0) * lf1),
            xi_b=jnp.exp((C - ri) * lb1),
            zeta_f=jnp.exp((C - 1.0 - ri) * lf1),
            zeta_b=jnp.exp(ri * lb1),
            cd_f=jnp.exp(C * lf1), cd_b=jnp.exp(C * lb1))

    dec = [decays(hd) for hd in heads]

    def state_update(hd, i, zeta, cd, first):
        rows = slice(i * C, (i + 1) * C)
        kz = (k_ref[0, rows, hd * RET_DK:(hd + 1) * RET_DK].astype(F32) * zeta).astype(BF16)
        upd = _dot_tn(kz, v_ref[0, rows, hd * RET_DV:(hd + 1) * RET_DV])
        acc_ref[hd] = upd if first else acc_ref[hd] * cd + upd

    for i in range(n_chunks - 1):
        for hd in heads:
            state_update(hd, i, dec[hd]["zeta_f"], dec[hd]["cd_f"], first=(i == 0))
            sf_ref[hd, i + 1] = acc_ref[hd].astype(BF16)

    for i in reversed(range(n_chunks)):
        rows = slice(i * C, (i + 1) * C)
        for hd in heads:
            d = dec[hd]
            vcols = slice(hd * RET_DV, (hd + 1) * RET_DV)
            q = q_ref[0, rows, hd * RET_DK:(hd + 1) * RET_DK]
            qf = q.astype(F32)
            s = (_dot_nt(q, k_ref[0, rows, hd * RET_DK:(hd + 1) * RET_DK]) * d["dmat"]).astype(BF16)
            o = _dot(s, v_ref[0, rows, vcols])
            if i > 0:
                o = o + _dot((qf * d["xi_f"]).astype(BF16), sf_ref[hd, i])
            if i < n_chunks - 1:
                o = o + _dot((qf * d["xi_b"]).astype(BF16), acc_ref[hd].astype(BF16))
            mu = jnp.mean(o, axis=-1, keepdims=True)
            oc = o - mu
            var = jnp.mean(oc * oc, axis=-1, keepdims=True)
            o_ref[0, rows, vcols] = (oc * lax.rsqrt(var + NORM_EPS) * gn_ref[:, vcols]).astype(BF16)
            if i > 0:
                state_update(hd, i, d["zeta_b"], d["cd_b"], first=(i == n_chunks - 1))


def _retention(q, k, v, dec_f, dec_b, gn_g):
    B, T, _ = q.shape
    C = min(T, RET_CHUNK)
    hs = RET_HEADS_PER_STEP
    head = lambda w: pl.BlockSpec((1, T, hs * w), lambda b, h: (b, 0, h))
    dec = pl.BlockSpec((hs, 1, LANES), lambda b, h: (h, 0, 0))
    return pl.pallas_call(
        _retention_kernel,
        grid=(B, RET_HEADS // hs),
        in_specs=[head(RET_DK), head(RET_DK), head(RET_DV), dec, dec,
                  pl.BlockSpec((1, hs * RET_DV), lambda b, h: (0, h))],
        out_specs=head(RET_DV),
        out_shape=jax.ShapeDtypeStruct((B, T, RET_V), BF16),
        scratch_shapes=[pltpu.VMEM((hs, T // C, RET_DK, RET_DV), BF16),
                        pltpu.VMEM((hs, RET_DK, RET_DV), F32)],
        compiler_params=_params("parallel", "parallel"),
        name="retention",
    )(q, k, v, dec_f, dec_b, gn_g)


def _odd_back_kernel(x_ref, g_ref, wg_ref, a_ref, wo_ref, fg_ref, o_ref, *, final_norm):
    for rows in _sub_tiles(x_ref.shape[1]):
        x = x_ref[0, rows, :]
        h = _rms(x, g_ref[...]).astype(BF16)
        gate = _dot(h, wg_ref[...])
        y = (a_ref[0, rows, :].astype(F32) * _silu(gate)).astype(BF16)
        out = x + _dot(y, wo_ref[...])
        if final_norm:
            out = _rms(out, fg_ref[...])
        o_ref[0, rows, :] = out


def _odd_back(x, g, w_all, a, wo_all, layer, fg, final_norm):
    B, T, D = x.shape
    tm = min(T, ROW_TILE)
    row = lambda w: pl.BlockSpec((1, tm, w), lambda b, t: (b, t, 0))
    return pl.pallas_call(
        functools.partial(_odd_back_kernel, final_norm=final_norm),
        grid=(B, T // tm),
        in_specs=[row(D), _const_spec(g.shape),
                  _layer_spec((D, RET_V), layer, col_block=(2 * RET_QK + RET_V) // RET_V),
                  row(RET_V), _layer_spec(wo_all.shape[1:], layer), _const_spec(fg.shape)],
        out_specs=row(D),
        out_shape=jax.ShapeDtypeStruct((B, T, D), F32),
        compiler_params=_params("parallel", "parallel"),
        name="odd_back",
    )(x, g, w_all, a, wo_all, fg)


def _even_weights(w_in, w_uq, w_ukv):
    n_lat = MLA_Q_LORA + MLA_KV_LORA
    wlat = jnp.concatenate(
        [w_in[:, :n_lat + MLA_ROPE], jnp.zeros((D_MODEL, LANES - MLA_ROPE), w_in.dtype)], axis=1)
    wu = w_in[:, n_lat + MLA_ROPE:n_lat + MLA_ROPE + MIX_B]
    wg = w_in[:, n_lat + MLA_ROPE + MIX_B:]
    uq = w_uq.reshape(MLA_Q_LORA, MLA_HEADS, MLA_NOPE + MLA_ROPE)
    uq_nope = uq[:, :, :MLA_NOPE].reshape(MLA_Q_LORA, MIX_A)
    uq_rope_t = uq[:, :, MLA_NOPE:].reshape(MLA_Q_LORA, MLA_HEADS * MLA_ROPE).T
    ukv = w_ukv.reshape(MLA_KV_LORA, MLA_HEADS, MLA_NOPE + MLA_V)
    uk = ukv[:, :, :MLA_NOPE].reshape(MLA_KV_LORA, MIX_A)
    uvt = ukv[:, :, MLA_NOPE:].reshape(MLA_KV_LORA, MIX_A).T
    return (wu, uq_nope, uk) + tuple(w.astype(BF16) for w in (wlat, wg, uq_rope_t, uvt))


def kernel(x, positions, a_norm_g, a_w_in, a_q_norm_g, a_w_uq, a_kv_norm_g, a_w_ukv, a_pool_w,
           a_pool_scale, a_w_out, r_norm_g, r_w_in, r_decay_fwd, r_decay_bwd, r_gn_g, r_w_out,
           final_norm_g):
    depth = a_norm_g.shape[0] + r_norm_g.shape[0]
    assert depth % 2 == 0, "the final norm is fused into the last (odd) layer's back kernel"
    ca, sa, cat, sat, cr, sr = _rope_tables(positions)
    fg = final_norm_g[None]
    a_w_out, r_w_in, r_w_out = (w.astype(BF16) for w in (a_w_out, r_w_in, r_w_out))
    for layer in range(depth):
        i = layer // 2
        if layer % 2 == 0:
            wu, uq_nope, uk, wlat, wg, uq_rope_t, wuvt = _even_weights(a_w_in[i], a_w_uq[i], a_w_ukv[i])
            g = a_norm_g[i][None]
            wuqt = jnp.concatenate([_q_fold(uq_nope, uk), uq_rope_t], axis=0)
            qt, kl, vt = _even_front(x, g, wlat, a_q_norm_g[i][None], wuqt, a_kv_norm_g[i][None],
                                     wuvt, ca, sa, cat, sat)
            a = _attention(qt, kl, vt)
            wu = _pool_fold(wu, a_pool_w[i], a_pool_scale[i][None])
            x = _even_back(x, g, wu, wg, a, a_w_out, i)
        else:
            g = r_norm_g[i][None]
            q, k, v = _odd_front(x, g, r_w_in, i, cr, sr)
            dec_f = jnp.broadcast_to(r_decay_fwd[i][:, None, None], (RET_HEADS, 1, LANES))
            dec_b = jnp.broadcast_to(r_decay_bwd[i][:, None, None], (RET_HEADS, 1, LANES))
            o = _retention(q, k, v, dec_f, dec_b, r_gn_g[i][None])
            x = _odd_back(x, g, r_w_in, o, r_w_out, i, fg, final_norm=(layer == depth - 1))
    return x
```

```python
import functools

import jax
import jax.numpy as jnp
from jax import lax
from jax.experimental import pallas as pl
from jax.experimental.pallas import tpu as pltpu

ROPE_BASE = 10000.0
NORM_EPS = 1e-6
LOG2E = 1.4426950408889634

D_MODEL = 1024
MLA_HEADS = 8
MLA_NOPE = 128
MLA_ROPE = 64
MLA_V = 128
MLA_Q_LORA = 384
MLA_KV_LORA = 128
POOL_WINDOWS = (2, 4, 8, 16)
POOL_DIM = 256
MIX_A = MLA_HEADS * MLA_V
MIX_B = len(POOL_WINDOWS) * POOL_DIM
RET_HEADS = 4
RET_DK = 256
RET_DV = 512
RET_QK = RET_HEADS * RET_DK
RET_V = RET_HEADS * RET_DV

LANES = 128
SUBLANES = 8
HEAD_PAD = 2 * LANES
Q_ROWS = MLA_HEADS * (MLA_NOPE + MLA_ROPE)
POOL_HALO = SUBLANES
VMEM_LIMIT = 56 * 1024 * 1024

ROW_TILE = 1024
SUB_TILE = 512
ATTN_KV_CHUNK = 256
ATTN_HEADS_PER_STEP = 8
SCORE_CHUNKS = 2
ATTN_GROUP = 2
RET_CHUNK = 256
RET_HEADS_PER_STEP = 2

BF16 = jnp.bfloat16
F32 = jnp.float32


def _params(*sem, **kw):
    return pltpu.CompilerParams(dimension_semantics=sem, vmem_limit_bytes=VMEM_LIMIT, **kw)


def _const_spec(shape):
    nd = len(shape)
    return pl.BlockSpec(shape, lambda *_: (0,) * nd, pipeline_mode=pl.Buffered(1))


def _layer_spec(block, layer, col_block=0):
    idx = (layer,) + (0,) * (len(block) - 1) + (col_block,)
    return pl.BlockSpec((None,) + tuple(block), lambda *_: idx, pipeline_mode=pl.Buffered(1))


def _sub_tiles(rows):
    sub = min(rows, SUB_TILE)
    return [slice(r, r + sub) for r in range(0, rows, sub)]


def _rms(x, g):
    return x * lax.rsqrt(jnp.mean(x * x, axis=-1, keepdims=True) + NORM_EPS) * g


def _silu(x):
    return x / (1.0 + jnp.exp(-x))


def _dot(a, b):
    return jnp.dot(a, b, preferred_element_type=F32)


def _dot_nt(a, b):
    return lax.dot_general(a, b, (((1,), (1,)), ((), ())), preferred_element_type=F32)


def _dot_tn(a, b):
    return lax.dot_general(a, b, (((0,), (0,)), ((), ())), preferred_element_type=F32)


def _split3(a):
    hi = a.astype(BF16)
    r1 = a - hi.astype(F32)
    mid = r1.astype(BF16)
    lo = (r1 - mid.astype(F32)).astype(BF16)
    return jnp.concatenate([hi, mid, lo], axis=1)


def _rope_tab_kernel(pos_ref, inv_r_ref, selc_ref, sels_ref, selt_ref,
                     ca_ref, sa_ref, cat_ref, sat_ref, cr_ref, sr_ref):
    tt = pos_ref.shape[2]
    sub = min(tt, 256)
    for r in range(0, tt, sub):
        rows = slice(r, r + sub)
        pos = jnp.broadcast_to(pos_ref[0, :, rows].astype(F32), (SUBLANES, sub)).T[:, :1]
        ang = pos * inv_r_ref[...]
        c = jnp.cos(ang)
        s = jnp.sin(ang)
        cr_ref[0, rows, :] = c
        sr_ref[0, rows, :] = s
        c3 = _split3(c)
        s3 = _split3(s)
        ca_ref[0, rows, :] = _dot(c3, selc_ref[...])
        sa_ref[0, rows, :] = _dot(s3, sels_ref[...])
        cat_ref[0, :, rows] = _dot_nt(selt_ref[...], c3)
        sat_ref[0, :, rows] = _dot_nt(selt_ref[...], s3)


def _rope_tables(positions):
    B, T = positions.shape
    half_a = MLA_ROPE // 2
    stride = RET_DK // MLA_ROPE
    inv_r = 1.0 / (ROPE_BASE ** (jnp.arange(0, RET_DK, 2, dtype=F32) / RET_DK))
    assert inv_r.shape[0] == LANES and stride * half_a == LANES
    pick = (jnp.arange(LANES)[:, None] == stride * jnp.arange(half_a)[None, :]).astype(BF16)
    zeros = jnp.zeros((LANES, LANES - 2 * half_a), BF16)
    sel_c = jnp.concatenate([pick, pick, zeros], axis=1)
    sel_s = jnp.concatenate([-pick, pick, zeros], axis=1)
    tile3 = lambda m: jnp.concatenate([m, m, m], axis=0)
    tt = min(T, 1024)
    tab = jax.ShapeDtypeStruct((B, T, LANES), F32)
    tab_t = jax.ShapeDtypeStruct((B, half_a, T), F32)
    blk = pl.BlockSpec((1, tt, LANES), lambda b, t: (b, t, 0))
    blk_t = pl.BlockSpec((1, half_a, tt), lambda b, t: (b, 0, t))
    return pl.pallas_call(
        _rope_tab_kernel,
        grid=(B, T // tt),
        in_specs=[pl.BlockSpec((1, 1, tt), lambda b, t: (b, 0, t)),
                  _const_spec((1, LANES)), _const_spec((3 * LANES, LANES)),
                  _const_spec((3 * LANES, LANES)), _const_spec((half_a, 3 * LANES))],
        out_specs=[blk, blk, blk_t, blk_t, blk, blk],
        out_shape=[tab, tab, tab_t, tab_t, tab, tab],
        compiler_params=_params("parallel", "parallel"),
        name="rope_tables",
    )(positions.reshape(B, 1, T), inv_r[None], tile3(sel_c), tile3(sel_s), tile3(pick).T)


def _rope_pad(v, c, s):
    q = LANES // 4
    return v * c + (pltpu.roll(v, 3 * q, 1) + pltpu.roll(v, q, 1)) * s


def _even_front_kernel(x_ref, g_ref, wlat_ref, gq_ref, wuqt_ref, gkv_ref, wuvt_ref,
                       ca_ref, sa_ref, cat_ref, sat_ref, qt_ref, kl_ref, vt_ref):
    q_scale = (MLA_NOPE + MLA_ROPE) ** -0.5 * LOG2E
    half = MLA_ROPE // 2
    for j, rows in enumerate(_sub_tiles(x_ref.shape[1])):
        h = _rms(x_ref[0, rows, :], g_ref[...]).astype(BF16)
        lat = _dot(h, wlat_ref[...])
        cq = _rms(lat[:, :MLA_Q_LORA], gq_ref[...]).astype(BF16)
        qt = _dot_nt(wuqt_ref[...], cq)
        ckv = _rms(lat[:, MLA_Q_LORA:MLA_Q_LORA + MLA_KV_LORA], gkv_ref[...]).astype(BF16)
        kl_ref[0, rows, :MLA_KV_LORA] = ckv
        vt_ref[0, :, rows] = _dot_nt(wuvt_ref[...], ckv).astype(BF16)
        kl_ref[0, rows, MLA_KV_LORA:] = _rope_pad(lat[:, MLA_Q_LORA + MLA_KV_LORA:],
                                                  ca_ref[0, rows, :], sa_ref[0, rows, :]).astype(BF16)
        ct = cat_ref[0, :, rows]
        st = sat_ref[0, :, rows]
        qt_ref[0, j, :MIX_A, :] = (qt[:MIX_A] * q_scale).astype(BF16)
        for hd in range(MLA_HEADS):
            r1 = MIX_A + hd * MLA_ROPE
            r2 = r1 + half
            r3 = r2 + half
            x1 = qt[r1:r2]
            x2 = qt[r2:r3]
            qt_ref[0, j, r1:r2, :] = ((x1 * ct - x2 * st) * q_scale).astype(BF16)
            qt_ref[0, j, r2:r3, :] = ((x2 * ct + x1 * st) * q_scale).astype(BF16)


def _q_fold_kernel(uk_ref, uq_ref, o_ref):
    o_ref[...] = lax.dot_general(uk_ref[...], uq_ref[...], (((1,), (1,)), ((), ())),
                                 precision=lax.Precision.HIGHEST,
                                 preferred_element_type=F32).astype(BF16)


def _q_fold(uq_nope, uk):
    return pl.pallas_call(
        _q_fold_kernel,
        grid=(MLA_HEADS,),
        in_specs=[pl.BlockSpec((MLA_KV_LORA, MLA_NOPE), lambda h: (0, h)),
                  pl.BlockSpec((MLA_Q_LORA, MLA_NOPE), lambda h: (0, h))],
        out_specs=pl.BlockSpec((MLA_KV_LORA, MLA_Q_LORA), lambda h: (h, 0)),
        out_shape=jax.ShapeDtypeStruct((MLA_HEADS * MLA_KV_LORA, MLA_Q_LORA), BF16),
        compiler_params=_params("parallel"),
        name="q_fold",
    )(uk, uq_nope)


def _even_front(x, g, wlat, gq, wuqt, gkv, wuvt, ca, sa, cat, sat):
    B, T, D = x.shape
    tm = min(T, ROW_TILE)
    sub = min(tm, SUB_TILE)
    row = lambda w: pl.BlockSpec((1, tm, w), lambda b, t: (b, t, 0))
    col = lambda r: pl.BlockSpec((1, r, tm), lambda b, t: (b, 0, t))
    consts = [g, wlat, gq, wuqt, gkv, wuvt]
    return pl.pallas_call(
        _even_front_kernel,
        grid=(B, T // tm),
        in_specs=[row(D)] + [_const_spec(c.shape) for c in consts]
        + [row(LANES), row(LANES), col(MLA_ROPE // 2), col(MLA_ROPE // 2)],
        out_specs=[pl.BlockSpec((1, tm // sub, Q_ROWS, sub), lambda b, t: (b, t, 0, 0)),
                   row(HEAD_PAD), col(MIX_A)],
        out_shape=[jax.ShapeDtypeStruct((B, T // sub, Q_ROWS, sub), BF16),
                   jax.ShapeDtypeStruct((B, T, HEAD_PAD), BF16),
                   jax.ShapeDtypeStruct((B, MIX_A, T), BF16)],
        compiler_params=_params("parallel", "parallel"),
        name="even_front",
    )(x, *consts, ca, sa, cat, sat)


def _attn_kernel(qn_ref, qr_ref, kl_ref, vt_ref, o_ref, s_ref, m_ref):
    n_q, tq = qn_ref.shape[1], qn_ref.shape[3]
    grp_blocks, n_kc, kc = s_ref.shape[1], s_ref.shape[2], s_ref.shape[3]
    n_blk = n_q * (qn_ref.shape[2] // MLA_NOPE)
    q_pad = jnp.zeros((HEAD_PAD - MLA_NOPE - MLA_ROPE, tq), BF16)
    n_grp = n_blk // grp_blocks
    sub = (kc // SUBLANES, SUBLANES, tq)

    def score_chunk(g, j, c, m8):
        hd, qb = divmod(g * grp_blocks + j, n_q)
        qt = jnp.concatenate([qn_ref[0, qb, hd * MLA_NOPE:(hd + 1) * MLA_NOPE, :],
                              qr_ref[0, qb, hd * MLA_ROPE:(hd + 1) * MLA_ROPE, :], q_pad], axis=0)
        if c % SCORE_CHUNKS:
            return m8
        s = _dot(kl_ref[0, c * kc:(c + SCORE_CHUNKS) * kc, :], qt)
        for i in range(SCORE_CHUNKS):
            si = s[i * kc:(i + 1) * kc]
            s_ref[g % 2, j, c + i] = si
            cm = jnp.max(si.reshape(sub), axis=0)
            m8 = cm if m8 is None else jnp.maximum(m8, cm)
        return m8

    def value_chunk(g, j, c, m, l8, acc):
        hd = (g * grp_blocks + j) // n_q
        p = jnp.exp2(s_ref[g % 2, j, c] - m)
        ps = jnp.sum(p.reshape(sub), axis=0)
        vt = vt_ref[0, hd * MLA_V:(hd + 1) * MLA_V, c * kc:(c + 1) * kc]
        pv = _dot(vt, p.astype(BF16))
        return (ps if l8 is None else l8 + ps), (pv if acc is None else acc + pv)

    def stage(k):
        run_v = k >= 1
        run_s = k < n_grp
        blocks = range(grp_blocks)
        if run_v:
            m = [jnp.max(m_ref[(k - 1) % 2, j], axis=0, keepdims=True) for j in blocks]
        m8 = [None] * grp_blocks
        l8 = [None] * grp_blocks
        acc = [None] * grp_blocks
        for c in range(n_kc):
            for j in blocks:
                if run_v:
                    l8[j], acc[j] = value_chunk(k - 1, j, c, m[j], l8[j], acc[j])
                if run_s:
                    m8[j] = score_chunk(k, j, c, m8[j])
        for j in blocks:
            if run_s:
                m_ref[k % 2, j] = m8[j]
            if run_v:
                hd, qb = divmod((k - 1) * grp_blocks + j, n_q)
                l = jnp.sum(l8[j], axis=0, keepdims=True)
                o_ref[0, qb * tq:(qb + 1) * tq, hd * MLA_V:(hd + 1) * MLA_V] = (
                    (acc[j] / l).T.astype(BF16))

    one = jnp.minimum(pl.program_id(0) + 1, 1)
    for k in range(n_grp + 1):
        lax.fori_loop(0, one, lambda _, carry, k=k: (stage(k), carry)[1], 0)


def _attention(qt, kl, vt):
    B, T, _ = kl.shape
    n_q, tq = qt.shape[1], qt.shape[3]
    kc = min(T, ATTN_KV_CHUNK)
    hs = ATTN_HEADS_PER_STEP
    rope_blk0 = MIX_A // (hs * MLA_ROPE)
    return pl.pallas_call(
        _attn_kernel,
        grid=(B, MLA_HEADS // hs),
        in_specs=[pl.BlockSpec((1, n_q, hs * MLA_NOPE, tq), lambda b, h: (b, 0, h, 0)),
                  pl.BlockSpec((1, n_q, hs * MLA_ROPE, tq), lambda b, h: (b, 0, rope_blk0 + h, 0)),
                  pl.BlockSpec((1, T, HEAD_PAD), lambda b, h: (b, 0, 0)),
                  pl.BlockSpec((1, hs * MLA_V, T), lambda b, h: (b, h, 0))],
        out_specs=pl.BlockSpec((1, T, hs * MLA_V), lambda b, h: (b, 0, h)),
        out_shape=jax.ShapeDtypeStruct((B, T, MIX_A), BF16),
        scratch_shapes=[pltpu.VMEM((2, ATTN_GROUP, T // kc, kc, tq), F32),
                        pltpu.VMEM((2, ATTN_GROUP, SUBLANES, tq), F32)],
        compiler_params=_params("parallel", "parallel"),
        name="mla_attention",
    )(qt, qt, kl, vt)


def _pool_fold_kernel(wu_ref, pw_ref, ps_ref, o_ref):
    w = lax.dot_general(wu_ref[...], pw_ref[0], (((1,), (0,)), ((), ())),
                        precision=lax.Precision.HIGHEST, preferred_element_type=F32)
    o_ref[...] = (w * ps_ref[...]).astype(BF16)


def _pool_fold(wu, pw, ps):
    D = wu.shape[0]
    n_g = len(POOL_WINDOWS)
    return pl.pallas_call(
        _pool_fold_kernel,
        grid=(n_g,),
        in_specs=[pl.BlockSpec((D, POOL_DIM), lambda g: (0, g)),
                  pl.BlockSpec((1, POOL_DIM, POOL_DIM), lambda g: (g, 0, 0)),
                  pl.BlockSpec((1, POOL_DIM), lambda g: (0, g))],
        out_specs=pl.BlockSpec((D, POOL_DIM), lambda g: (0, g)),
        out_shape=jax.ShapeDtypeStruct((D, MIX_B), BF16),
        compiler_params=_params("parallel"),
        name="pool_fold",
    )(wu, pw, ps)


def _even_back_kernel(x_ref, xp_ref, xn_ref, g_ref, wu_ref, wg_ref, a_ref, wo_ref, o_ref, *, seq_len):
    tm = x_ref.shape[1]
    g = g_ref[...]
    wu = wu_ref[...]
    for rows in _sub_tiles(tm):
        sub = rows.stop - rows.start
        t0 = pl.program_id(1) * tm + rows.start
        x = x_ref[0, rows, :]
        h = _rms(x, g).astype(BF16)
        xp = xp_ref[0] if rows.start == 0 else x_ref[0, rows.start - POOL_HALO:rows.start, :]
        xn = xn_ref[0] if rows.stop == tm else x_ref[0, rows.stop:rows.stop + POOL_HALO, :]
        hp = _rms(xp, g).astype(BF16)
        hn = _rms(xn, g).astype(BF16)
        up = jnp.where(t0 > 0, _dot(hp, wu), 0.0)
        un = jnp.where(t0 + sub < seq_len, _dot(hn, wu), 0.0)
        u = _dot(h, wu)
        ue = jnp.concatenate([up, u, un], axis=0)
        ext = sub + 2 * POOL_HALO
        t = (t0 + lax.broadcasted_iota(jnp.int32, (sub, 1), 0))
        gate = _dot(h, wg_ref[...])
        sg = _silu(gate)
        ya = (a_ref[0, rows, :].astype(F32) * sg[:, :MIX_A]).astype(BF16)
        y = _dot(ya, wo_ref[:MIX_A, :])
        for gi, w in enumerate(POOL_WINDOWS):
            left = w // 2
            right = w - 1 - left
            cols = slice(gi * POOL_DIM, (gi + 1) * POOL_DIM)
            run = ue[:, cols]
            n = 1
            while n < left:
                run = run + pltpu.roll(run, ext - n, 0)
                n *= 2
            acc = (run + pltpu.roll(run, left, 0))[POOL_HALO:POOL_HALO + sub]
            cnt = (jnp.minimum(t + right, seq_len - 1) - jnp.maximum(t - left, 0) + 1).astype(F32)
            bg = acc / cnt - u[:, cols]
            yb = (bg * sg[:, MIX_A + gi * POOL_DIM:MIX_A + (gi + 1) * POOL_DIM]).astype(BF16)
            y = y + _dot(yb, wo_ref[MIX_A + gi * POOL_DIM:MIX_A + (gi + 1) * POOL_DIM, :])
        o_ref[0, rows, :] = x + y


def _even_back(x, g, wu, wg, a, wo_all, layer):
    B, T, D = x.shape
    tm = min(T, ROW_TILE)
    nb = tm // POOL_HALO
    last = T // POOL_HALO - 1
    row = lambda w: pl.BlockSpec((1, tm, w), lambda b, t: (b, t, 0))
    return pl.pallas_call(
        functools.partial(_even_back_kernel, seq_len=T),
        grid=(B, T // tm),
        in_specs=[row(D),
                  pl.BlockSpec((1, POOL_HALO, D), lambda b, t: (b, jnp.maximum(t * nb - 1, 0), 0)),
                  pl.BlockSpec((1, POOL_HALO, D), lambda b, t: (b, jnp.minimum((t + 1) * nb, last), 0)),
                  _const_spec(g.shape), _const_spec(wu.shape), _const_spec(wg.shape), row(MIX_A),
                  _layer_spec(wo_all.shape[1:], layer)],
        out_specs=row(D),
        out_shape=jax.ShapeDtypeStruct((B, T, D), F32),
        compiler_params=_params("parallel", "parallel"),
        name="even_back",
    )(x, x, x, g, wu, wg, a, wo_all)


def _odd_front_kernel(x_ref, g_ref, w_ref, c_ref, s_ref, q_ref, k_ref, v_ref):
    half = RET_DK // 2
    k_scale = RET_DK ** -0.5
    for rows in _sub_tiles(x_ref.shape[1]):
        h = _rms(x_ref[0, rows, :], g_ref[...]).astype(BF16)
        qkv = _dot(h, w_ref[...])
        c = c_ref[0, rows, :]
        s = s_ref[0, rows, :]
        for hd in range(RET_HEADS):
            for base, ref, scale in ((0, q_ref, None), (RET_QK, k_ref, k_scale)):
                lo = base + hd * RET_DK
                x1 = qkv[:, lo:lo + half]
                x2 = qkv[:, lo + half:lo + RET_DK]
                o1 = x1 * c - x2 * s
                o2 = x2 * c + x1 * s
                if scale is not None:
                    o1 = o1 * scale
                    o2 = o2 * scale
                ref[0, rows, hd * RET_DK:hd * RET_DK + half] = o1.astype(BF16)
                ref[0, rows, hd * RET_DK + half:(hd + 1) * RET_DK] = o2.astype(BF16)
        v_ref[0, rows, :] = qkv[:, 2 * RET_QK:].astype(BF16)


def _odd_front(x, g, w_all, layer, c, s):
    B, T, D = x.shape
    tm = min(T, ROW_TILE)
    row = lambda wd: pl.BlockSpec((1, tm, wd), lambda b, t: (b, t, 0))
    qk = jax.ShapeDtypeStruct((B, T, RET_QK), BF16)
    return pl.pallas_call(
        _odd_front_kernel,
        grid=(B, T // tm),
        in_specs=[row(D), _const_spec(g.shape), _layer_spec((D, 2 * RET_QK + RET_V), layer),
                  row(LANES), row(LANES)],
        out_specs=[row(RET_QK), row(RET_QK), row(RET_V)],
        out_shape=[qk, qk, jax.ShapeDtypeStruct((B, T, RET_V), BF16)],
        compiler_params=_params("parallel", "parallel"),
        name="odd_front",
    )(x, g, w_all, c, s)


def _log_sigmoid(x):
    return jnp.minimum(x, 0.0) - jnp.log1p(jnp.exp(-jnp.abs(x)))


def _retention_kernel(q_ref, k_ref, v_ref, df_ref, db_ref, gn_ref, o_ref, sf_ref, acc_ref):
    T = q_ref.shape[1]
    C = min(T, RET_CHUNK)
    n_chunks = T // C
    heads = range(q_ref.shape[2] // RET_DK)
    ri = lax.broadcasted_iota(jnp.int32, (C, 1), 0).astype(F32)
    diff = (lax.broadcasted_iota(jnp.int32, (C, C), 0)
            - lax.broadcasted_iota(jnp.int32, (C, C), 1)).astype(F32)

    def decays(hd):
        lf1 = _log_sigmoid(df_ref[hd])[:, :1]
        lb1 = _log_sigmoid(db_ref[hd])[:, :1]
        return dict(
            dmat=jnp.exp(jnp.where(diff >= 0, diff * lf1, -diff * lb1)),
            xi_f=jnp.exp((ri + 1.0) * lf1),
            xi_b=jnp.exp((C - ri) * lb1),
            zeta_f=jnp.exp((C - 1.0 - ri) * lf1),
            zeta_b=jnp.exp(ri * lb1),
            cd_f=jnp.exp(C * lf1), cd_b=jnp.exp(C * lb1))

    dec = [decays(hd) for hd in heads]

    def state_update(hd, i, zeta, cd, first):
        rows = slice(i * C, (i + 1) * C)
        kz = (k_ref[0, rows, hd * RET_DK:(hd + 1) * RET_DK].astype(F32) * zeta).astype(BF16)
        upd = _dot_tn(kz, v_ref[0, rows, hd * RET_DV:(hd + 1) * RET_DV])
        acc_ref[hd] = upd if first else acc_ref[hd] * cd + upd

    for i in range(n_chunks - 1):
        for hd in heads:
            state_update(hd, i, dec[hd]["zeta_f"], dec[hd]["cd_f"], first=(i == 0))
            sf_ref[hd, i + 1] = acc_ref[hd].astype(BF16)

    for i in reversed(range(n_chunks)):
        rows = slice(i * C, (i + 1) * C)
        for hd in heads:
            d = dec[hd]
            vcols = slice(hd * RET_DV, (hd + 1) * RET_DV)
            q = q_ref[0, rows, hd * RET_DK:(hd + 1) * RET_DK]
            qf = q.astype(F32)
            s = (_dot_nt(q, k_ref[0, rows, hd * RET_DK:(hd + 1) * RET_DK]) * d["dmat"]).astype(BF16)
            o = _dot(s, v_ref[0, rows, vcols])
            if i > 0:
                o = o + _dot((qf * d["xi_f"]).astype(BF16), sf_ref[hd, i])
            if i < n_chunks - 1:
                o = o + _dot((qf * d["xi_b"]).astype(BF16), acc_ref[hd].astype(BF16))
            mu = jnp.mean(o, axis=-1, keepdims=True)
            oc = o - mu
            var = jnp.mean(oc * oc, axis=-1, keepdims=True)
            o_ref[0, rows, vcols] = (oc * lax.rsqrt(var + NORM_EPS) * gn_ref[:, vcols]).astype(BF16)
            if i > 0:
                state_update(hd, i, d["zeta_b"], d["cd_b"], first=(i == n_chunks - 1))


def _retention(q, k, v, dec_f, dec_b, gn_g):
    B, T, _ = q.shape
    C = min(T, RET_CHUNK)
    hs = RET_HEADS_PER_STEP
    head = lambda w: pl.BlockSpec((1, T, hs * w), lambda b, h: (b, 0, h))
    dec = pl.BlockSpec((hs, 1, LANES), lambda b, h: (h, 0, 0))
    return pl.pallas_call(
        _retention_kernel,
        grid=(B, RET_HEADS // hs),
        in_specs=[head(RET_DK), head(RET_DK), head(RET_DV), dec, dec,
                  pl.BlockSpec((1, hs * RET_DV), lambda b, h: (0, h))],
        out_specs=head(RET_DV),
        out_shape=jax.ShapeDtypeStruct((B, T, RET_V), BF16),
        scratch_shapes=[pltpu.VMEM((hs, T // C, RET_DK, RET_DV), BF16),
                        pltpu.VMEM((hs, RET_DK, RET_DV), F32)],
        compiler_params=_params("parallel", "parallel"),
        name="retention",
    )(q, k, v, dec_f, dec_b, gn_g)


def _odd_back_kernel(x_ref, g_ref, wg_ref, a_ref, wo_ref, fg_ref, o_ref, *, final_norm):
    for rows in _sub_tiles(x_ref.shape[1]):
        x = x_ref[0, rows, :]
        h = _rms(x, g_ref[...]).astype(BF16)
        gate = _dot(h, wg_ref[...])
        y = (a_ref[0, rows, :].astype(F32) * _silu(gate)).astype(BF16)
        out = x + _dot(y, wo_ref[...])
        if final_norm:
            out = _rms(out, fg_ref[...])
        o_ref[0, rows, :] = out


def _odd_back(x, g, w_all, a, wo_all, layer, fg, final_norm):
    B, T, D = x.shape
    tm = min(T, ROW_TILE)
    row = lambda w: pl.BlockSpec((1, tm, w), lambda b, t: (b, t, 0))
    return pl.pallas_call(
        functools.partial(_odd_back_kernel, final_norm=final_norm),
        grid=(B, T // tm),
        in_specs=[row(D), _const_spec(g.shape),
                  _layer_spec((D, RET_V), layer, col_block=(2 * RET_QK + RET_V) // RET_V),
                  row(RET_V), _layer_spec(wo_all.shape[1:], layer), _const_spec(fg.shape)],
        out_specs=row(D),
        out_shape=jax.ShapeDtypeStruct((B, T, D), F32),
        compiler_params=_params("parallel", "parallel"),
        name="odd_back",
    )(x, g, w_all, a, wo_all, fg)


def _even_weights(w_in, w_uq, w_ukv):
    n_lat = MLA_Q_LORA + MLA_KV_LORA
    wlat = jnp.concatenate(
        [w_in[:, :n_lat + MLA_ROPE], jnp.zeros((D_MODEL, LANES - MLA_ROPE), w_in.dtype)], axis=1)
    wu = w_in[:, n_lat + MLA_ROPE:n_lat + MLA_ROPE + MIX_B]
    wg = w_in[:, n_lat + MLA_ROPE + MIX_B:]
    uq = w_uq.reshape(MLA_Q_LORA, MLA_HEADS, MLA_NOPE + MLA_ROPE)
    uq_nope = uq[:, :, :MLA_NOPE].reshape(MLA_Q_LORA, MIX_A)
    uq_rope_t = uq[:, :, MLA_NOPE:].reshape(MLA_Q_LORA, MLA_HEADS * MLA_ROPE).T
    ukv = w_ukv.reshape(MLA_KV_LORA, MLA_HEADS, MLA_NOPE + MLA_V)
    uk = ukv[:, :, :MLA_NOPE].reshape(MLA_KV_LORA, MIX_A)
    uvt = ukv[:, :, MLA_NOPE:].reshape(MLA_KV_LORA, MIX_A).T
    return (wu, uq_nope, uk) + tuple(w.astype(BF16) for w in (wlat, wg, uq_rope_t, uvt))


def kernel(x, positions, a_norm_g, a_w_in, a_q_norm_g, a_w_uq, a_kv_norm_g, a_w_ukv, a_pool_w,
           a_pool_scale, a_w_out, r_norm_g, r_w_in, r_decay_fwd, r_decay_bwd, r_gn_g, r_w_out,
           final_norm_g):
    depth = a_norm_g.shape[0] + r_norm_g.shape[0]
    assert depth % 2 == 0, "the final norm is fused into the last (odd) layer's back kernel"
    ca, sa, cat, sat, cr, sr = _rope_tables(positions)
    fg = final_norm_g[None]
    a_w_out, r_w_in, r_w_out = (w.astype(BF16) for w in (a_w_out, r_w_in, r_w_out))
    for layer in range(depth):
        i = layer // 2
        if layer % 2 == 0:
            wu, uq_nope, uk, wlat, wg, uq_rope_t, wuvt = _even_weights(a_w_in[i], a_w_uq[i], a_w_ukv[i])
            g = a_norm_g[i][None]
            wuqt = jnp.concatenate([_q_fold(uq_nope, uk), uq_rope_t], axis=0)
            qt, kl, vt = _even_front(x, g, wlat, a_q_norm_g[i][None], wuqt, a_kv_norm_g[i][None],
                                     wuvt, ca, sa, cat, sat)
            a = _attention(qt, kl, vt)
            wu = _pool_fold(wu, a_pool_w[i], a_pool_scale[i][None])
            x = _even_back(x, g, wu, wg, a, a_w_out, i)
        else:
            g = r_norm_g[i][None]
            q, k, v = _odd_front(x, g, r_w_in, i, cr, sr)
            dec_f = jnp.broadcast_to(r_decay_fwd[i][:, None, None], (RET_HEADS, 1, LANES))
            dec_b = jnp.broadcast_to(r_decay_bwd[i][:, None, None], (RET_HEADS, 1, LANES))
            o = _retention(q, k, v, dec_f, dec_b, r_gn_g[i][None])
            x = _odd_back(x, g, r_w_in, o, r_w_out, i, fg, final_norm=(layer == depth - 1))
    return x
```

```python
import functools

import jax
import jax.numpy as jnp
from jax import lax
from jax.experimental import pallas as pl
from jax.experimental.pallas import tpu as pltpu

ROPE_BASE = 10000.0
NORM_EPS = 1e-6
LOG2E = 1.4426950408889634

D_MODEL = 1024
MLA_HEADS = 8
MLA_NOPE = 128
MLA_ROPE = 64
MLA_V = 128
MLA_Q_LORA = 384
MLA_KV_LORA = 128
POOL_WINDOWS = (2, 4, 8, 16)
POOL_DIM = 256
MIX_A = MLA_HEADS * MLA_V
MIX_B = len(POOL_WINDOWS) * POOL_DIM
RET_HEADS = 4
RET_DK = 256
RET_DV = 512
RET_QK = RET_HEADS * RET_DK
RET_V = RET_HEADS * RET_DV

LANES = 128
SUBLANES = 8
HEAD_PAD = 2 * LANES
Q_ROWS = MLA_HEADS * (MLA_NOPE + MLA_ROPE)
POOL_HALO = SUBLANES
VMEM_LIMIT = 56 * 1024 * 1024

ROW_TILE = 1024
SUB_TILE = 512
ATTN_KV_CHUNK = 256
ATTN_HEADS_PER_STEP = 8
ATTN_GROUP = 2
RET_CHUNK = 256
RET_HEADS_PER_STEP = 2

BF16 = jnp.bfloat16
F32 = jnp.float32


def _params(*sem, **kw):
    return pltpu.CompilerParams(dimension_semantics=sem, vmem_limit_bytes=VMEM_LIMIT, **kw)


def _const_spec(shape):
    nd = len(shape)
    return pl.BlockSpec(shape, lambda *_: (0,) * nd, pipeline_mode=pl.Buffered(1))


def _layer_spec(block, layer, col_block=0):
    idx = (layer,) + (0,) * (len(block) - 1) + (col_block,)
    return pl.BlockSpec((None,) + tuple(block), lambda *_: idx, pipeline_mode=pl.Buffered(1))


def _sub_tiles(rows):
    sub = min(rows, SUB_TILE)
    return [slice(r, r + sub) for r in range(0, rows, sub)]


def _rms(x, g):
    return x * lax.rsqrt(jnp.mean(x * x, axis=-1, keepdims=True) + NORM_EPS) * g


def _silu(x):
    return x / (1.0 + jnp.exp(-x))


def _dot(a, b):
    return jnp.dot(a, b, preferred_element_type=F32)


def _dot_nt(a, b):
    return lax.dot_general(a, b, (((1,), (1,)), ((), ())), preferred_element_type=F32)


def _dot_tn(a, b):
    return lax.dot_general(a, b, (((0,), (0,)), ((), ())), preferred_element_type=F32)


def _split3(a):
    hi = a.astype(BF16)
    r1 = a - hi.astype(F32)
    mid = r1.astype(BF16)
    lo = (r1 - mid.astype(F32)).astype(BF16)
    return jnp.concatenate([hi, mid, lo], axis=1)


def _rope_tab_kernel(pos_ref, inv_r_ref, selc_ref, sels_ref, selt_ref,
                     ca_ref, sa_ref, cat_ref, sat_ref, cr_ref, sr_ref):
    tt = pos_ref.shape[2]
    sub = min(tt, 256)
    for r in range(0, tt, sub):
        rows = slice(r, r + sub)
        pos = jnp.broadcast_to(pos_ref[0, :, rows].astype(F32), (SUBLANES, sub)).T[:, :1]
        ang = pos * inv_r_ref[...]
        c = jnp.cos(ang)
        s = jnp.sin(ang)
        cr_ref[0, rows, :] = c
        sr_ref[0, rows, :] = s
        c3 = _split3(c)
        s3 = _split3(s)
        ca_ref[0, rows, :] = _dot(c3, selc_ref[...])
        sa_ref[0, rows, :] = _dot(s3, sels_ref[...])
        cat_ref[0, :, rows] = _dot_nt(selt_ref[...], c3)
        sat_ref[0, :, rows] = _dot_nt(selt_ref[...], s3)


def _rope_tables(positions):
    B, T = positions.shape
    half_a = MLA_ROPE // 2
    stride = RET_DK // MLA_ROPE
    inv_r = 1.0 / (ROPE_BASE ** (jnp.arange(0, RET_DK, 2, dtype=F32) / RET_DK))
    assert inv_r.shape[0] == LANES and stride * half_a == LANES
    pick = (jnp.arange(LANES)[:, None] == stride * jnp.arange(half_a)[None, :]).astype(BF16)
    zeros = jnp.zeros((LANES, LANES - 2 * half_a), BF16)
    sel_c = jnp.concatenate([pick, pick, zeros], axis=1)
    sel_s = jnp.concatenate([-pick, pick, zeros], axis=1)
    tile3 = lambda m: jnp.concatenate([m, m, m], axis=0)
    tt = min(T, 1024)
    tab = jax.ShapeDtypeStruct((B, T, LANES), F32)
    tab_t = jax.ShapeDtypeStruct((B, half_a, T), F32)
    blk = pl.BlockSpec((1, tt, LANES), lambda b, t: (b, t, 0))
    blk_t = pl.BlockSpec((1, half_a, tt), lambda b, t: (b, 0, t))
    return pl.pallas_call(
        _rope_tab_kernel,
        grid=(B, T // tt),
        in_specs=[pl.BlockSpec((1, 1, tt), lambda b, t: (b, 0, t)),
                  _const_spec((1, LANES)), _const_spec((3 * LANES, LANES)),
                  _const_spec((3 * LANES, LANES)), _const_spec((half_a, 3 * LANES))],
        out_specs=[blk, blk, blk_t, blk_t, blk, blk],
        out_shape=[tab, tab, tab_t, tab_t, tab, tab],
        compiler_params=_params("parallel", "parallel"),
        name="rope_tables",
    )(positions.reshape(B, 1, T), inv_r[None], tile3(sel_c), tile3(sel_s), tile3(pick).T)


def _rope_pad(v, c, s):
    q = LANES // 4
    return v * c + (pltpu.roll(v, 3 * q, 1) + pltpu.roll(v, q, 1)) * s


def _even_front_kernel(x_ref, g_ref, wlat_ref, gq_ref, wuqt_ref, gkv_ref, wuvt_ref,
                       ca_ref, sa_ref, cat_ref, sat_ref, qt_ref, kl_ref, vt_ref):
    q_scale = (MLA_NOPE + MLA_ROPE) ** -0.5 * LOG2E
    half = MLA_ROPE // 2
    for j, rows in enumerate(_sub_tiles(x_ref.shape[1])):
        h = _rms(x_ref[0, rows, :], g_ref[...]).astype(BF16)
        lat = _dot(h, wlat_ref[...])
        cq = _rms(lat[:, :MLA_Q_LORA], gq_ref[...]).astype(BF16)
        qt = _dot_nt(wuqt_ref[...], cq)
        ckv = _rms(lat[:, MLA_Q_LORA:MLA_Q_LORA + MLA_KV_LORA], gkv_ref[...]).astype(BF16)
        kl_ref[0, rows, :MLA_KV_LORA] = ckv
        vt_ref[0, :, rows] = _dot_nt(wuvt_ref[...], ckv).astype(BF16)
        kl_ref[0, rows, MLA_KV_LORA:] = _rope_pad(lat[:, MLA_Q_LORA + MLA_KV_LORA:],
                                                  ca_ref[0, rows, :], sa_ref[0, rows, :]).astype(BF16)
        ct = cat_ref[0, :, rows]
        st = sat_ref[0, :, rows]
        qt_ref[0, j, :MIX_A, :] = (qt[:MIX_A] * q_scale).astype(BF16)
        for hd in range(MLA_HEADS):
            r1 = MIX_A + hd * MLA_ROPE
            r2 = r1 + half
            r3 = r2 + half
            x1 = qt[r1:r2]
            x2 = qt[r2:r3]
            qt_ref[0, j, r1:r2, :] = ((x1 * ct - x2 * st) * q_scale).astype(BF16)
            qt_ref[0, j, r2:r3, :] = ((x2 * ct + x1 * st) * q_scale).astype(BF16)


def _q_fold_kernel(uk_ref, uq_ref, o_ref):
    o_ref[...] = lax.dot_general(uk_ref[...], uq_ref[...], (((1,), (1,)), ((), ())),
                                 precision=lax.Precision.HIGHEST,
                                 preferred_element_type=F32).astype(BF16)


def _q_fold(uq_nope, uk):
    return pl.pallas_call(
        _q_fold_kernel,
        grid=(MLA_HEADS,),
        in_specs=[pl.BlockSpec((MLA_KV_LORA, MLA_NOPE), lambda h: (0, h)),
                  pl.BlockSpec((MLA_Q_LORA, MLA_NOPE), lambda h: (0, h))],
        out_specs=pl.BlockSpec((MLA_KV_LORA, MLA_Q_LORA), lambda h: (h, 0)),
        out_shape=jax.ShapeDtypeStruct((MLA_HEADS * MLA_KV_LORA, MLA_Q_LORA), BF16),
        compiler_params=_params("parallel"),
        name="q_fold",
    )(uk, uq_nope)


def _even_front(x, g, wlat, gq, wuqt, gkv, wuvt, ca, sa, cat, sat):
    B, T, D = x.shape
    tm = min(T, ROW_TILE)
    sub = min(tm, SUB_TILE)
    row = lambda w: pl.BlockSpec((1, tm, w), lambda b, t: (b, t, 0))
    col = lambda r: pl.BlockSpec((1, r, tm), lambda b, t: (b, 0, t))
    consts = [g, wlat, gq, wuqt, gkv, wuvt]
    return pl.pallas_call(
        _even_front_kernel,
        grid=(B, T // tm),
        in_specs=[row(D)] + [_const_spec(c.shape) for c in consts]
        + [row(LANES), row(LANES), col(MLA_ROPE // 2), col(MLA_ROPE // 2)],
        out_specs=[pl.BlockSpec((1, tm // sub, Q_ROWS, sub), lambda b, t: (b, t, 0, 0)),
                   row(HEAD_PAD), col(MIX_A)],
        out_shape=[jax.ShapeDtypeStruct((B, T // sub, Q_ROWS, sub), BF16),
                   jax.ShapeDtypeStruct((B, T, HEAD_PAD), BF16),
                   jax.ShapeDtypeStruct((B, MIX_A, T), BF16)],
        compiler_params=_params("parallel", "parallel"),
        name="even_front",
    )(x, *consts, ca, sa, cat, sat)


def _attn_kernel(qn_ref, qr_ref, kl_ref, vt_ref, o_ref, s_ref, m_ref):
    n_q, tq = qn_ref.shape[1], qn_ref.shape[3]
    grp_blocks, n_kc, kc = s_ref.shape[1], s_ref.shape[2], s_ref.shape[3]
    n_blk = n_q * (qn_ref.shape[2] // MLA_NOPE)
    q_pad = jnp.zeros((HEAD_PAD - MLA_NOPE - MLA_ROPE, tq), BF16)
    n_grp = n_blk // grp_blocks
    sub = (kc // SUBLANES, SUBLANES, tq)

    def score_chunk(g, j, c, m8):
        hd, qb = divmod(g * grp_blocks + j, n_q)
        qt = jnp.concatenate([qn_ref[0, qb, hd * MLA_NOPE:(hd + 1) * MLA_NOPE, :],
                              qr_ref[0, qb, hd * MLA_ROPE:(hd + 1) * MLA_ROPE, :], q_pad], axis=0)
        s = _dot(kl_ref[0, c * kc:(c + 1) * kc, :], qt)
        s_ref[g % 2, j, c] = s
        cm = jnp.max(s.reshape(sub), axis=0)
        return cm if m8 is None else jnp.maximum(m8, cm)

    def value_chunk(g, j, c, m, l8, acc):
        hd = (g * grp_blocks + j) // n_q
        p = jnp.exp2((s_ref[g % 2, j, c] - m).astype(BF16))
        ps = jnp.sum(p.astype(F32).reshape(sub), axis=0)
        vt = vt_ref[0, hd * MLA_V:(hd + 1) * MLA_V, c * kc:(c + 1) * kc]
        pv = _dot(vt, p)
        return (ps if l8 is None else l8 + ps), (pv if acc is None else acc + pv)

    def stage(k):
        run_v = k >= 1
        run_s = k < n_grp
        blocks = range(grp_blocks)
        if run_v:
            m = [jnp.max(m_ref[(k - 1) % 2, j], axis=0, keepdims=True) for j in blocks]
        m8 = [None] * grp_blocks
        l8 = [None] * grp_blocks
        acc = [None] * grp_blocks
        for c in range(n_kc):
            for j in blocks:
                if run_v:
                    l8[j], acc[j] = value_chunk(k - 1, j, c, m[j], l8[j], acc[j])
                if run_s:
                    m8[j] = score_chunk(k, j, c, m8[j])
        for j in blocks:
            if run_s:
                m_ref[k % 2, j] = m8[j]
            if run_v:
                hd, qb = divmod((k - 1) * grp_blocks + j, n_q)
                l = jnp.sum(l8[j], axis=0, keepdims=True)
                o_ref[0, qb * tq:(qb + 1) * tq, hd * MLA_V:(hd + 1) * MLA_V] = (
                    (acc[j] / l).T.astype(BF16))

    one = jnp.minimum(pl.program_id(0) + 1, 1)
    for k in range(n_grp + 1):
        lax.fori_loop(0, one, lambda _, carry, k=k: (stage(k), carry)[1], 0)


def _attention(qt, kl, vt):
    B, T, _ = kl.shape
    n_q, tq = qt.shape[1], qt.shape[3]
    kc = min(T, ATTN_KV_CHUNK)
    hs = ATTN_HEADS_PER_STEP
    rope_blk0 = MIX_A // (hs * MLA_ROPE)
    return pl.pallas_call(
        _attn_kernel,
        grid=(B, MLA_HEADS // hs),
        in_specs=[pl.BlockSpec((1, n_q, hs * MLA_NOPE, tq), lambda b, h: (b, 0, h, 0)),
                  pl.BlockSpec((1, n_q, hs * MLA_ROPE, tq), lambda b, h: (b, 0, rope_blk0 + h, 0)),
                  pl.BlockSpec((1, T, HEAD_PAD), lambda b, h: (b, 0, 0)),
                  pl.BlockSpec((1, hs * MLA_V, T), lambda b, h: (b, h, 0))],
        out_specs=pl.BlockSpec((1, T, hs * MLA_V), lambda b, h: (b, 0, h)),
        out_shape=jax.ShapeDtypeStruct((B, T, MIX_A), BF16),
        scratch_shapes=[pltpu.VMEM((2, ATTN_GROUP, T // kc, kc, tq), F32),
                        pltpu.VMEM((2, ATTN_GROUP, SUBLANES, tq), F32)],
        compiler_params=_params("parallel", "parallel"),
        name="mla_attention",
    )(qt, qt, kl, vt)


def _pool_fold_kernel(wu_ref, pw_ref, ps_ref, o_ref):
    w = lax.dot_general(wu_ref[...], pw_ref[0], (((1,), (0,)), ((), ())),
                        precision=lax.Precision.HIGHEST, preferred_element_type=F32)
    o_ref[...] = (w * ps_ref[...]).astype(BF16)


def _pool_fold(wu, pw, ps):
    D = wu.shape[0]
    n_g = len(POOL_WINDOWS)
    return pl.pallas_call(
        _pool_fold_kernel,
        grid=(n_g,),
        in_specs=[pl.BlockSpec((D, POOL_DIM), lambda g: (0, g)),
                  pl.BlockSpec((1, POOL_DIM, POOL_DIM), lambda g: (g, 0, 0)),
                  pl.BlockSpec((1, POOL_DIM), lambda g: (0, g))],
        out_specs=pl.BlockSpec((D, POOL_DIM), lambda g: (0, g)),
        out_shape=jax.ShapeDtypeStruct((D, MIX_B), BF16),
        compiler_params=_params("parallel"),
        name="pool_fold",
    )(wu, pw, ps)


def _even_back_kernel(x_ref, xp_ref, xn_ref, g_ref, wu_ref, wg_ref, a_ref, wo_ref, o_ref, *, seq_len):
    tm = x_ref.shape[1]
    g = g_ref[...]
    wu = wu_ref[...]
    for rows in _sub_tiles(tm):
        sub = rows.stop - rows.start
        t0 = pl.program_id(1) * tm + rows.start
        x = x_ref[0, rows, :]
        h = _rms(x, g).astype(BF16)
        xp = xp_ref[0] if rows.start == 0 else x_ref[0, rows.start - POOL_HALO:rows.start, :]
        xn = xn_ref[0] if rows.stop == tm else x_ref[0, rows.stop:rows.stop + POOL_HALO, :]
        hp = _rms(xp, g).astype(BF16)
        hn = _rms(xn, g).astype(BF16)
        up = jnp.where(t0 > 0, _dot(hp, wu), 0.0)
        un = jnp.where(t0 + sub < seq_len, _dot(hn, wu), 0.0)
        u = _dot(h, wu)
        ue = jnp.concatenate([up, u, un], axis=0)
        ext = sub + 2 * POOL_HALO
        t = (t0 + lax.broadcasted_iota(jnp.int32, (sub, 1), 0))
        gate = _dot(h, wg_ref[...])
        sg = _silu(gate)
        ya = (a_ref[0, rows, :].astype(F32) * sg[:, :MIX_A]).astype(BF16)
        y = _dot(ya, wo_ref[:MIX_A, :])
        for gi, w in enumerate(POOL_WINDOWS):
            left = w // 2
            right = w - 1 - left
            cols = slice(gi * POOL_DIM, (gi + 1) * POOL_DIM)
            run = ue[:, cols]
            n = 1
            while n < left:
                run = run + pltpu.roll(run, ext - n, 0)
                n *= 2
            acc = (run + pltpu.roll(run, left, 0))[POOL_HALO:POOL_HALO + sub]
            cnt = (jnp.minimum(t + right, seq_len - 1) - jnp.maximum(t - left, 0) + 1).astype(F32)
            bg = acc / cnt - u[:, cols]
            yb = (bg * sg[:, MIX_A + gi * POOL_DIM:MIX_A + (gi + 1) * POOL_DIM]).astype(BF16)
            y = y + _dot(yb, wo_ref[MIX_A + gi * POOL_DIM:MIX_A + (gi + 1) * POOL_DIM, :])
        o_ref[0, rows, :] = x + y


def _even_back(x, g, wu, wg, a, wo_all, layer):
    B, T, D = x.shape
    tm = min(T, ROW_TILE)
    nb = tm // POOL_HALO
    last = T // POOL_HALO - 1
    row = lambda w: pl.BlockSpec((1, tm, w), lambda b, t: (b, t, 0))
    return pl.pallas_call(
        functools.partial(_even_back_kernel, seq_len=T),
        grid=(B, T // tm),
        in_specs=[row(D),
                  pl.BlockSpec((1, POOL_HALO, D), lambda b, t: (b, jnp.maximum(t * nb - 1, 0), 0)),
                  pl.BlockSpec((1, POOL_HALO, D), lambda b, t: (b, jnp.minimum((t + 1) * nb, last), 0)),
                  _const_spec(g.shape), _const_spec(wu.shape), _const_spec(wg.shape), row(MIX_A),
                  _layer_spec(wo_all.shape[1:], layer)],
        out_specs=row(D),
        out_shape=jax.ShapeDtypeStruct((B, T, D), F32),
        compiler_params=_params("parallel", "parallel"),
        name="even_back",
    )(x, x, x, g, wu, wg, a, wo_all)


def _odd_front_kernel(x_ref, g_ref, w_ref, c_ref, s_ref, q_ref, k_ref, v_ref):
    half = RET_DK // 2
    k_scale = RET_DK ** -0.5
    for rows in _sub_tiles(x_ref.shape[1]):
        h = _rms(x_ref[0, rows, :], g_ref[...]).astype(BF16)
        qkv = _dot(h, w_ref[...])
        c = c_ref[0, rows, :]
        s = s_ref[0, rows, :]
        for hd in range(RET_HEADS):
            for base, ref, scale in ((0, q_ref, None), (RET_QK, k_ref, k_scale)):
                lo = base + hd * RET_DK
                x1 = qkv[:, lo:lo + half]
                x2 = qkv[:, lo + half:lo + RET_DK]
                o1 = x1 * c - x2 * s
                o2 = x2 * c + x1 * s
                if scale is not None:
                    o1 = o1 * scale
                    o2 = o2 * scale
                ref[0, rows, hd * RET_DK:hd * RET_DK + half] = o1.astype(BF16)
                ref[0, rows, hd * RET_DK + half:(hd + 1) * RET_DK] = o2.astype(BF16)
        v_ref[0, rows, :] = qkv[:, 2 * RET_QK:].astype(BF16)


def _odd_front(x, g, w_all, layer, c, s):
    B, T, D = x.shape
    tm = min(T, ROW_TILE)
    row = lambda wd: pl.BlockSpec((1, tm, wd), lambda b, t: (b, t, 0))
    qk = jax.ShapeDtypeStruct((B, T, RET_QK), BF16)
    return pl.pallas_call(
        _odd_front_kernel,
        grid=(B, T // tm),
        in_specs=[row(D), _const_spec(g.shape), _layer_spec((D, 2 * RET_QK + RET_V), layer),
                  row(LANES), row(LANES)],
        out_specs=[row(RET_QK), row(RET_QK), row(RET_V)],
        out_shape=[qk, qk, jax.ShapeDtypeStruct((B, T, RET_V), BF16)],
        compiler_params=_params("parallel", "parallel"),
        name="odd_front",
    )(x, g, w_all, c, s)


def _log_sigmoid(x):
    return jnp.minimum(x, 0.0) - jnp.log1p(jnp.exp(-jnp.abs(x)))


def _retention_kernel(q_ref, k_ref, v_ref, df_ref, db_ref, gn_ref, o_ref, sf_ref, acc_ref):
    T = q_ref.shape[1]
    C = min(T, RET_CHUNK)
    n_chunks = T // C
    heads = range(q_ref.shape[2] // RET_DK)
    ri = lax.broadcasted_iota(jnp.int32, (C, 1), 0).astype(F32)
    diff = (lax.broadcasted_iota(jnp.int32, (C, C), 0)
            - lax.broadcasted_iota(jnp.int32, (C, C), 1)).astype(F32)

    def decays(hd):
        lf1 = _log_sigmoid(df_ref[hd])[:, :1]
        lb1 = _log_sigmoid(db_ref[hd])[:, :1]
        return dict(
            dmat=jnp.exp(jnp.where(diff >= 0, diff * lf1, -diff * lb1)),
            xi_f=jnp.exp((ri + 1.0) * lf1),
            xi_b=jnp.exp((C - ri) * lb1),
            zeta_f=jnp.exp((C - 1.0 - ri) * lf1),
            zeta_b=jnp.exp(ri * lb1),
            cd_f=jnp.exp(C * lf1), cd_b=jnp.exp(C * lb1))

    dec = [decays(hd) for hd in heads]

    def state_update(hd, i, zeta, cd, first):
        rows = slice(i * C, (i + 1) * C)
        kz = (k_ref[0, rows, hd * RET_DK:(hd + 1) * RET_DK].astype(F32) * zeta).astype(BF16)
        upd = _dot_tn(kz, v_ref[0, rows, hd * RET_DV:(hd + 1) * RET_DV])
        acc_ref[hd] = upd if first else acc_ref[hd] * cd + upd

    for i in range(n_chunks - 1):
        for hd in heads:
            state_update(hd, i, dec[hd]["zeta_f"], dec[hd]["cd_f"], first=(i == 0))
            sf_ref[hd, i + 1] = acc_ref[hd].astype(BF16)

    for i in reversed(range(n_chunks)):
        rows = slice(i * C, (i + 1) * C)
        for hd in heads:
            d = dec[hd]
            vcols = slice(hd * RET_DV, (hd + 1) * RET_DV)
            q = q_ref[0, rows, hd * RET_DK:(hd + 1) * RET_DK]
            qf = q.astype(F32)
            s = (_dot_nt(q, k_ref[0, rows, hd * RET_DK:(hd + 1) * RET_DK]) * d["dmat"]).astype(BF16)
            o = _dot(s, v_ref[0, rows, vcols])
            if i > 0:
                o = o + _dot((qf * d["xi_f"]).astype(BF16), sf_ref[hd, i])
            if i < n_chunks - 1:
                o = o + _dot((qf * d["xi_b"]).astype(BF16), acc_ref[hd].astype(BF16))
            mu = jnp.mean(o, axis=-1, keepdims=True)
            oc = o - mu
            var = jnp.mean(oc * oc, axis=-1, keepdims=True)
            o_ref[0, rows, vcols] = (oc * lax.rsqrt(var + NORM_EPS) * gn_ref[:, vcols]).astype(BF16)
            if i > 0:
                state_update(hd, i, d["zeta_b"], d["cd_b"], first=(i == n_chunks - 1))


def _retention(q, k, v, dec_f, dec_b, gn_g):
    B, T, _ = q.shape
    C = min(T, RET_CHUNK)
    hs = RET_HEADS_PER_STEP
    head = lambda w: pl.BlockSpec((1, T, hs * w), lambda b, h: (b, 0, h))
    dec = pl.BlockSpec((hs, 1, LANES), lambda b, h: (h, 0, 0))
    return pl.pallas_call(
        _retention_kernel,
        grid=(B, RET_HEADS // hs),
        in_specs=[head(RET_DK), head(RET_DK), head(RET_DV), dec, dec,
                  pl.BlockSpec((1, hs * RET_DV), lambda b, h: (0, h))],
        out_specs=head(RET_DV),
        out_shape=jax.ShapeDtypeStruct((B, T, RET_V), BF16),
        scratch_shapes=[pltpu.VMEM((hs, T // C, RET_DK, RET_DV), BF16),
                        pltpu.VMEM((hs, RET_DK, RET_DV), F32)],
        compiler_params=_params("parallel", "parallel"),
        name="retention",
    )(q, k, v, dec_f, dec_b, gn_g)


def _odd_back_kernel(x_ref, g_ref, wg_ref, a_ref, wo_ref, fg_ref, o_ref, *, final_norm):
    for rows in _sub_tiles(x_ref.shape[1]):
        x = x_ref[0, rows, :]
        h = _rms(x, g_ref[...]).astype(BF16)
        gate = _dot(h, wg_ref[...])
        y = (a_ref[0, rows, :].astype(F32) * _silu(gate)).astype(BF16)
        out = x + _dot(y, wo_ref[...])
        if final_norm:
            out = _rms(out, fg_ref[...])
        o_ref[0, rows, :] = out


def _odd_back(x, g, w_all, a, wo_all, layer, fg, final_norm):
    B, T, D = x.shape
    tm = min(T, ROW_TILE)
    row = lambda w: pl.BlockSpec((1, tm, w), lambda b, t: (b, t, 0))
    return pl.pallas_call(
        functools.partial(_odd_back_kernel, final_norm=final_norm),
        grid=(B, T // tm),
        in_specs=[row(D), _const_spec(g.shape),
                  _layer_spec((D, RET_V), layer, col_block=(2 * RET_QK + RET_V) // RET_V),
                  row(RET_V), _layer_spec(wo_all.shape[1:], layer), _const_spec(fg.shape)],
        out_specs=row(D),
        out_shape=jax.ShapeDtypeStruct((B, T, D), F32),
        compiler_params=_params("parallel", "parallel"),
        name="odd_back",
    )(x, g, w_all, a, wo_all, fg)


def _even_weights(w_in, w_uq, w_ukv):
    n_lat = MLA_Q_LORA + MLA_KV_LORA
    wlat = jnp.concatenate(
        [w_in[:, :n_lat + MLA_ROPE], jnp.zeros((D_MODEL, LANES - MLA_ROPE), w_in.dtype)], axis=1)
    wu = w_in[:, n_lat + MLA_ROPE:n_lat + MLA_ROPE + MIX_B]
    wg = w_in[:, n_lat + MLA_ROPE + MIX_B:]
    uq = w_uq.reshape(MLA_Q_LORA, MLA_HEADS, MLA_NOPE + MLA_ROPE)
    uq_nope = uq[:, :, :MLA_NOPE].reshape(MLA_Q_LORA, MIX_A)
    uq_rope_t = uq[:, :, MLA_NOPE:].reshape(MLA_Q_LORA, MLA_HEADS * MLA_ROPE).T
    ukv = w_ukv.reshape(MLA_KV_LORA, MLA_HEADS, MLA_NOPE + MLA_V)
    uk = ukv[:, :, :MLA_NOPE].reshape(MLA_KV_LORA, MIX_A)
    uvt = ukv[:, :, MLA_NOPE:].reshape(MLA_KV_LORA, MIX_A).T
    return (wu, uq_nope, uk) + tuple(w.astype(BF16) for w in (wlat, wg, uq_rope_t, uvt))


def kernel(x, positions, a_norm_g, a_w_in, a_q_norm_g, a_w_uq, a_kv_norm_g, a_w_ukv, a_pool_w,
           a_pool_scale, a_w_out, r_norm_g, r_w_in, r_decay_fwd, r_decay_bwd, r_gn_g, r_w_out,
           final_norm_g):
    depth = a_norm_g.shape[0] + r_norm_g.shape[0]
    assert depth % 2 == 0, "the final norm is fused into the last (odd) layer's back kernel"
    ca, sa, cat, sat, cr, sr = _rope_tables(positions)
    fg = final_norm_g[None]
    a_w_out, r_w_in, r_w_out = (w.astype(BF16) for w in (a_w_out, r_w_in, r_w_out))
    for layer in range(depth):
        i = layer // 2
        if layer % 2 == 0:
            wu, uq_nope, uk, wlat, wg, uq_rope_t, wuvt = _even_weights(a_w_in[i], a_w_uq[i], a_w_ukv[i])
            g = a_norm_g[i][None]
            wuqt = jnp.concatenate([_q_fold(uq_nope, uk), uq_rope_t], axis=0)
            qt, kl, vt = _even_front(x, g, wlat, a_q_norm_g[i][None], wuqt, a_kv_norm_g[i][None],
                                     wuvt, ca, sa, cat, sat)
            a = _attention(qt, kl, vt)
            wu = _pool_fold(wu, a_pool_w[i], a_pool_scale[i][None])
            x = _even_back(x, g, wu, wg, a, a_w_out, i)
        else:
            g = r_norm_g[i][None]
            q, k, v = _odd_front(x, g, r_w_in, i, cr, sr)
            dec_f = jnp.broadcast_to(r_decay_fwd[i][:, None, None], (RET_HEADS, 1, LANES))
            dec_b = jnp.broadcast_to(r_decay_bwd[i][:, None, None], (RET_HEADS, 1, LANES))
            o = _retention(q, k, v, dec_f, dec_b, r_gn_g[i][None])
            x = _odd_back(x, g, r_w_in, o, r_w_out, i, fg, final_norm=(layer == depth - 1))
    return x
```

```python
import functools

import jax
import jax.numpy as jnp
from jax import lax
from jax.experimental import pallas as pl
from jax.experimental.pallas import tpu as pltpu

ROPE_BASE = 10000.0
NORM_EPS = 1e-6
LOG2E = 1.4426950408889634

D_MODEL = 1024
MLA_HEADS = 8
MLA_NOPE = 128
MLA_ROPE = 64
MLA_V = 128
MLA_Q_LORA = 384
MLA_KV_LORA = 128
POOL_WINDOWS = (2, 4, 8, 16)
POOL_DIM = 256
MIX_A = MLA_HEADS * MLA_V
MIX_B = len(POOL_WINDOWS) * POOL_DIM
RET_HEADS = 4
RET_DK = 256
RET_DV = 512
RET_QK = RET_HEADS * RET_DK
RET_V = RET_HEADS * RET_DV

LANES = 128
SUBLANES = 8
HEAD_PAD = 2 * LANES
Q_ROWS = MLA_HEADS * (MLA_NOPE + MLA_ROPE)
POOL_HALO = SUBLANES
VMEM_LIMIT = 56 * 1024 * 1024

ROW_TILE = 1024
EVEN_FRONT_TILE = 2048
SUB_TILE = 512
ATTN_KV_CHUNK = 256
ATTN_HEADS_PER_STEP = 8
ATTN_GROUP = 2
RET_CHUNK = 256
RET_HEADS_PER_STEP = 2

BF16 = jnp.bfloat16
F32 = jnp.float32


def _params(*sem, **kw):
    return pltpu.CompilerParams(dimension_semantics=sem, vmem_limit_bytes=VMEM_LIMIT, **kw)


def _const_spec(shape):
    nd = len(shape)
    return pl.BlockSpec(shape, lambda *_: (0,) * nd, pipeline_mode=pl.Buffered(1))


def _layer_spec(block, layer, col_block=0):
    idx = (layer,) + (0,) * (len(block) - 1) + (col_block,)
    return pl.BlockSpec((None,) + tuple(block), lambda *_: idx, pipeline_mode=pl.Buffered(1))


def _sub_tiles(rows):
    sub = min(rows, SUB_TILE)
    return [slice(r, r + sub) for r in range(0, rows, sub)]


def _rms(x, g):
    return x * lax.rsqrt(jnp.mean(x * x, axis=-1, keepdims=True) + NORM_EPS) * g


def _silu(x):
    return x / (1.0 + jnp.exp(-x))


def _dot(a, b):
    return jnp.dot(a, b, preferred_element_type=F32)


def _dot_nt(a, b):
    return lax.dot_general(a, b, (((1,), (1,)), ((), ())), preferred_element_type=F32)


def _dot_tn(a, b):
    return lax.dot_general(a, b, (((0,), (0,)), ((), ())), preferred_element_type=F32)


def _split3(a):
    hi = a.astype(BF16)
    r1 = a - hi.astype(F32)
    mid = r1.astype(BF16)
    lo = (r1 - mid.astype(F32)).astype(BF16)
    return jnp.concatenate([hi, mid, lo], axis=1)


def _rope_tab_kernel(pos_ref, inv_r_ref, selc_ref, sels_ref, selt_ref,
                     ca_ref, sa_ref, cat_ref, sat_ref, cr_ref, sr_ref):
    tt = pos_ref.shape[2]
    sub = min(tt, 256)
    for r in range(0, tt, sub):
        rows = slice(r, r + sub)
        pos = jnp.broadcast_to(pos_ref[0, :, rows].astype(F32), (SUBLANES, sub)).T[:, :1]
        ang = pos * inv_r_ref[...]
        c = jnp.cos(ang)
        s = jnp.sin(ang)
        cr_ref[0, rows, :] = c
        sr_ref[0, rows, :] = s
        c3 = _split3(c)
        s3 = _split3(s)
        ca_ref[0, rows, :] = _dot(c3, selc_ref[...])
        sa_ref[0, rows, :] = _dot(s3, sels_ref[...])
        cat_ref[0, :, rows] = _dot_nt(selt_ref[...], c3)
        sat_ref[0, :, rows] = _dot_nt(selt_ref[...], s3)


def _rope_tables(positions):
    B, T = positions.shape
    half_a = MLA_ROPE // 2
    stride = RET_DK // MLA_ROPE
    inv_r = 1.0 / (ROPE_BASE ** (jnp.arange(0, RET_DK, 2, dtype=F32) / RET_DK))
    assert inv_r.shape[0] == LANES and stride * half_a == LANES
    pick = (jnp.arange(LANES)[:, None] == stride * jnp.arange(half_a)[None, :]).astype(BF16)
    zeros = jnp.zeros((LANES, LANES - 2 * half_a), BF16)
    sel_c = jnp.concatenate([pick, pick, zeros], axis=1)
    sel_s = jnp.concatenate([-pick, pick, zeros], axis=1)
    tile3 = lambda m: jnp.concatenate([m, m, m], axis=0)
    tt = min(T, 1024)
    tab = jax.ShapeDtypeStruct((B, T, LANES), F32)
    tab_t = jax.ShapeDtypeStruct((B, half_a, T), F32)
    blk = pl.BlockSpec((1, tt, LANES), lambda b, t: (b, t, 0))
    blk_t = pl.BlockSpec((1, half_a, tt), lambda b, t: (b, 0, t))
    return pl.pallas_call(
        _rope_tab_kernel,
        grid=(B, T // tt),
        in_specs=[pl.BlockSpec((1, 1, tt), lambda b, t: (b, 0, t)),
                  _const_spec((1, LANES)), _const_spec((3 * LANES, LANES)),
                  _const_spec((3 * LANES, LANES)), _const_spec((half_a, 3 * LANES))],
        out_specs=[blk, blk, blk_t, blk_t, blk, blk],
        out_shape=[tab, tab, tab_t, tab_t, tab, tab],
        compiler_params=_params("parallel", "parallel"),
        name="rope_tables",
    )(positions.reshape(B, 1, T), inv_r[None], tile3(sel_c), tile3(sel_s), tile3(pick).T)


def _rope_pad(v, c, s):
    q = LANES // 4
    return v * c + (pltpu.roll(v, 3 * q, 1) + pltpu.roll(v, q, 1)) * s


def _even_front_kernel(x_ref, g_ref, wlat_ref, gq_ref, wuqt_ref, gkv_ref, wuvt_ref,
                       ca_ref, sa_ref, cat_ref, sat_ref, qt_ref, kl_ref, vt_ref):
    q_scale = (MLA_NOPE + MLA_ROPE) ** -0.5 * LOG2E
    half = MLA_ROPE // 2
    for j, rows in enumerate(_sub_tiles(x_ref.shape[1])):
        h = _rms(x_ref[0, rows, :], g_ref[...]).astype(BF16)
        lat = _dot(h, wlat_ref[...])
        cq = _rms(lat[:, :MLA_Q_LORA], gq_ref[...]).astype(BF16)
        qt = _dot_nt(wuqt_ref[...], cq)
        ckv = _rms(lat[:, MLA_Q_LORA:MLA_Q_LORA + MLA_KV_LORA], gkv_ref[...]).astype(BF16)
        kl_ref[0, rows, :MLA_KV_LORA] = ckv
        vt_ref[0, :, rows] = _dot_nt(wuvt_ref[...], ckv).astype(BF16)
        kl_ref[0, rows, MLA_KV_LORA:] = _rope_pad(lat[:, MLA_Q_LORA + MLA_KV_LORA:],
                                                  ca_ref[0, rows, :], sa_ref[0, rows, :]).astype(BF16)
        ct = cat_ref[0, :, rows]
        st = sat_ref[0, :, rows]
        qt_ref[0, j, :MIX_A, :] = (qt[:MIX_A] * q_scale).astype(BF16)
        for hd in range(MLA_HEADS):
            r1 = MIX_A + hd * MLA_ROPE
            r2 = r1 + half
            r3 = r2 + half
            x1 = qt[r1:r2]
            x2 = qt[r2:r3]
            qt_ref[0, j, r1:r2, :] = ((x1 * ct - x2 * st) * q_scale).astype(BF16)
            qt_ref[0, j, r2:r3, :] = ((x2 * ct + x1 * st) * q_scale).astype(BF16)


def _q_fold_kernel(uk_ref, uq_ref, o_ref):
    o_ref[...] = lax.dot_general(uk_ref[...], uq_ref[...], (((1,), (1,)), ((), ())),
                                 precision=lax.Precision.HIGHEST,
                                 preferred_element_type=F32).astype(BF16)


def _q_fold(uq_nope, uk):
    return pl.pallas_call(
        _q_fold_kernel,
        grid=(MLA_HEADS,),
        in_specs=[pl.BlockSpec((MLA_KV_LORA, MLA_NOPE), lambda h: (0, h)),
                  pl.BlockSpec((MLA_Q_LORA, MLA_NOPE), lambda h: (0, h))],
        out_specs=pl.BlockSpec((MLA_KV_LORA, MLA_Q_LORA), lambda h: (h, 0)),
        out_shape=jax.ShapeDtypeStruct((MLA_HEADS * MLA_KV_LORA, MLA_Q_LORA), BF16),
        compiler_params=_params("parallel"),
        name="q_fold",
    )(uk, uq_nope)


def _even_front(x, g, wlat, gq, wuqt, gkv, wuvt, ca, sa, cat, sat):
    B, T, D = x.shape
    tm = min(T, EVEN_FRONT_TILE)
    sub = min(tm, SUB_TILE)
    row = lambda w: pl.BlockSpec((1, tm, w), lambda b, t: (b, t, 0))
    col = lambda r: pl.BlockSpec((1, r, tm), lambda b, t: (b, 0, t))
    consts = [g, wlat, gq, wuqt, gkv, wuvt]
    return pl.pallas_call(
        _even_front_kernel,
        grid=(B, T // tm),
        in_specs=[row(D)] + [_const_spec(c.shape) for c in consts]
        + [row(LANES), row(LANES), col(MLA_ROPE // 2), col(MLA_ROPE // 2)],
        out_specs=[pl.BlockSpec((1, tm // sub, Q_ROWS, sub), lambda b, t: (b, t, 0, 0)),
                   row(HEAD_PAD), col(MIX_A)],
        out_shape=[jax.ShapeDtypeStruct((B, T // sub, Q_ROWS, sub), BF16),
                   jax.ShapeDtypeStruct((B, T, HEAD_PAD), BF16),
                   jax.ShapeDtypeStruct((B, MIX_A, T), BF16)],
        compiler_params=_params("parallel", "parallel"),
        name="even_front",
    )(x, *consts, ca, sa, cat, sat)


def _attn_kernel(qn_ref, qr_ref, kl_ref, vt_ref, o_ref, s_ref, m_ref):
    n_q, tq = qn_ref.shape[1], qn_ref.shape[3]
    grp_blocks, n_kc, kc = s_ref.shape[1], s_ref.shape[2], s_ref.shape[3]
    n_blk = n_q * (qn_ref.shape[2] // MLA_NOPE)
    q_pad = jnp.zeros((HEAD_PAD - MLA_NOPE - MLA_ROPE, tq), BF16)
    n_grp = n_blk // grp_blocks
    sub = (kc // SUBLANES, SUBLANES, tq)

    def score_chunk(g, j, c, m8):
        hd, qb = divmod(g * grp_blocks + j, n_q)
        qt = jnp.concatenate([qn_ref[0, qb, hd * MLA_NOPE:(hd + 1) * MLA_NOPE, :],
                              qr_ref[0, qb, hd * MLA_ROPE:(hd + 1) * MLA_ROPE, :], q_pad], axis=0)
        s = _dot(kl_ref[0, c * kc:(c + 1) * kc, :], qt)
        s_ref[g % 2, j, c] = s
        cm = jnp.max(s.reshape(sub), axis=0)
        return cm if m8 is None else jnp.maximum(m8, cm)

    def value_chunk(g, j, c, m, l8, acc):
        hd = (g * grp_blocks + j) // n_q
        p = jnp.exp2(s_ref[g % 2, j, c] - m)
        ps = jnp.sum(p.reshape(sub), axis=0)
        vt = vt_ref[0, hd * MLA_V:(hd + 1) * MLA_V, c * kc:(c + 1) * kc]
        pv = _dot(vt, p.astype(BF16))
        return (ps if l8 is None else l8 + ps), (pv if acc is None else acc + pv)

    def stage(k):
        run_v = k >= 1
        run_s = k < n_grp
        blocks = range(grp_blocks)
        if run_v:
            m = [jnp.max(m_ref[(k - 1) % 2, j], axis=0, keepdims=True) for j in blocks]
        m8 = [None] * grp_blocks
        l8 = [None] * grp_blocks
        acc = [None] * grp_blocks
        for c in range(n_kc):
            for j in blocks:
                if run_v:
                    l8[j], acc[j] = value_chunk(k - 1, j, c, m[j], l8[j], acc[j])
                if run_s:
                    m8[j] = score_chunk(k, j, c, m8[j])
        for j in blocks:
            if run_s:
                m_ref[k % 2, j] = m8[j]
            if run_v:
                hd, qb = divmod((k - 1) * grp_blocks + j, n_q)
                l = jnp.sum(l8[j], axis=0, keepdims=True)
                o_ref[0, qb * tq:(qb + 1) * tq, hd * MLA_V:(hd + 1) * MLA_V] = (
                    (acc[j] / l).T.astype(BF16))

    one = jnp.minimum(pl.program_id(0) + 1, 1)
    for k in range(n_grp + 1):
        lax.fori_loop(0, one, lambda _, carry, k=k: (stage(k), carry)[1], 0)


def _attention(qt, kl, vt):
    B, T, _ = kl.shape
    n_q, tq = qt.shape[1], qt.shape[3]
    kc = min(T, ATTN_KV_CHUNK)
    hs = ATTN_HEADS_PER_STEP
    rope_blk0 = MIX_A // (hs * MLA_ROPE)
    return pl.pallas_call(
        _attn_kernel,
        grid=(B, MLA_HEADS // hs),
        in_specs=[pl.BlockSpec((1, n_q, hs * MLA_NOPE, tq), lambda b, h: (b, 0, h, 0)),
                  pl.BlockSpec((1, n_q, hs * MLA_ROPE, tq), lambda b, h: (b, 0, rope_blk0 + h, 0)),
                  pl.BlockSpec((1, T, HEAD_PAD), lambda b, h: (b, 0, 0)),
                  pl.BlockSpec((1, hs * MLA_V, T), lambda b, h: (b, h, 0))],
        out_specs=pl.BlockSpec((1, T, hs * MLA_V), lambda b, h: (b, 0, h)),
        out_shape=jax.ShapeDtypeStruct((B, T, MIX_A), BF16),
        scratch_shapes=[pltpu.VMEM((2, ATTN_GROUP, T // kc, kc, tq), F32),
                        pltpu.VMEM((2, ATTN_GROUP, SUBLANES, tq), F32)],
        compiler_params=_params("parallel", "parallel"),
        name="mla_attention",
    )(qt, qt, kl, vt)


def _pool_fold_kernel(wu_ref, pw_ref, ps_ref, o_ref):
    w = lax.dot_general(wu_ref[...], pw_ref[0], (((1,), (0,)), ((), ())),
                        precision=lax.Precision.HIGHEST, preferred_element_type=F32)
    o_ref[...] = (w * ps_ref[...]).astype(BF16)


def _pool_fold(wu, pw, ps):
    D = wu.shape[0]
    n_g = len(POOL_WINDOWS)
    return pl.pallas_call(
        _pool_fold_kernel,
        grid=(n_g,),
        in_specs=[pl.BlockSpec((D, POOL_DIM), lambda g: (0, g)),
                  pl.BlockSpec((1, POOL_DIM, POOL_DIM), lambda g: (g, 0, 0)),
                  pl.BlockSpec((1, POOL_DIM), lambda g: (0, g))],
        out_specs=pl.BlockSpec((D, POOL_DIM), lambda g: (0, g)),
        out_shape=jax.ShapeDtypeStruct((D, MIX_B), BF16),
        compiler_params=_params("parallel"),
        name="pool_fold",
    )(wu, pw, ps)


def _even_back_kernel(x_ref, xp_ref, xn_ref, g_ref, wu_ref, wg_ref, a_ref, wo_ref, o_ref, *, seq_len):
    tm = x_ref.shape[1]
    g = g_ref[...]
    wu = wu_ref[...]
    for rows in _sub_tiles(tm):
        sub = rows.stop - rows.start
        t0 = pl.program_id(1) * tm + rows.start
        x = x_ref[0, rows, :]
        h = _rms(x, g).astype(BF16)
        xp = xp_ref[0] if rows.start == 0 else x_ref[0, rows.start - POOL_HALO:rows.start, :]
        xn = xn_ref[0] if rows.stop == tm else x_ref[0, rows.stop:rows.stop + POOL_HALO, :]
        hp = _rms(xp, g).astype(BF16)
        hn = _rms(xn, g).astype(BF16)
        up = jnp.where(t0 > 0, _dot(hp, wu), 0.0)
        un = jnp.where(t0 + sub < seq_len, _dot(hn, wu), 0.0)
        u = _dot(h, wu)
        ue = jnp.concatenate([up, u, un], axis=0)
        ext = sub + 2 * POOL_HALO
        t = (t0 + lax.broadcasted_iota(jnp.int32, (sub, 1), 0))
        gate = _dot(h, wg_ref[...])
        sg = _silu(gate)
        ya = (a_ref[0, rows, :].astype(F32) * sg[:, :MIX_A]).astype(BF16)
        y = _dot(ya, wo_ref[:MIX_A, :])
        for gi, w in enumerate(POOL_WINDOWS):
            left = w // 2
            right = w - 1 - left
            cols = slice(gi * POOL_DIM, (gi + 1) * POOL_DIM)
            run = ue[:, cols]
            n = 1
            while n < left:
                run = run + pltpu.roll(run, ext - n, 0)
                n *= 2
            acc = (run + pltpu.roll(run, left, 0))[POOL_HALO:POOL_HALO + sub]
            cnt = (jnp.minimum(t + right, seq_len - 1) - jnp.maximum(t - left, 0) + 1).astype(F32)
            bg = acc / cnt - u[:, cols]
            yb = (bg * sg[:, MIX_A + gi * POOL_DIM:MIX_A + (gi + 1) * POOL_DIM]).astype(BF16)
            y = y + _dot(yb, wo_ref[MIX_A + gi * POOL_DIM:MIX_A + (gi + 1) * POOL_DIM, :])
        o_ref[0, rows, :] = x + y


def _even_back(x, g, wu, wg, a, wo_all, layer):
    B, T, D = x.shape
    tm = min(T, ROW_TILE)
    nb = tm // POOL_HALO
    last = T // POOL_HALO - 1
    row = lambda w: pl.BlockSpec((1, tm, w), lambda b, t: (b, t, 0))
    return pl.pallas_call(
        functools.partial(_even_back_kernel, seq_len=T),
        grid=(B, T // tm),
        in_specs=[row(D),
                  pl.BlockSpec((1, POOL_HALO, D), lambda b, t: (b, jnp.maximum(t * nb - 1, 0), 0)),
                  pl.BlockSpec((1, POOL_HALO, D), lambda b, t: (b, jnp.minimum((t + 1) * nb, last), 0)),
                  _const_spec(g.shape), _const_spec(wu.shape), _const_spec(wg.shape), row(MIX_A),
                  _layer_spec(wo_all.shape[1:], layer)],
        out_specs=row(D),
        out_shape=jax.ShapeDtypeStruct((B, T, D), F32),
        compiler_params=_params("parallel", "parallel"),
        name="even_back",
    )(x, x, x, g, wu, wg, a, wo_all)


def _odd_front_kernel(x_ref, g_ref, w_ref, c_ref, s_ref, q_ref, k_ref, v_ref):
    half = RET_DK // 2
    k_scale = RET_DK ** -0.5
    for rows in _sub_tiles(x_ref.shape[1]):
        h = _rms(x_ref[0, rows, :], g_ref[...]).astype(BF16)
        qkv = _dot(h, w_ref[...])
        c = c_ref[0, rows, :]
        s = s_ref[0, rows, :]
        for hd in range(RET_HEADS):
            for base, ref, scale in ((0, q_ref, None), (RET_QK, k_ref, k_scale)):
                lo = base + hd * RET_DK
                x1 = qkv[:, lo:lo + half]
                x2 = qkv[:, lo + half:lo + RET_DK]
                o1 = x1 * c - x2 * s
                o2 = x2 * c + x1 * s
                if scale is not None:
                    o1 = o1 * scale
                    o2 = o2 * scale
                ref[0, rows, hd * RET_DK:hd * RET_DK + half] = o1.astype(BF16)
                ref[0, rows, hd * RET_DK + half:(hd + 1) * RET_DK] = o2.astype(BF16)
        v_ref[0, rows, :] = qkv[:, 2 * RET_QK:].astype(BF16)


def _odd_front(x, g, w_all, layer, c, s):
    B, T, D = x.shape
    tm = min(T, ROW_TILE)
    row = lambda wd: pl.BlockSpec((1, tm, wd), lambda b, t: (b, t, 0))
    qk = jax.ShapeDtypeStruct((B, T, RET_QK), BF16)
    return pl.pallas_call(
        _odd_front_kernel,
        grid=(B, T // tm),
        in_specs=[row(D), _const_spec(g.shape), _layer_spec((D, 2 * RET_QK + RET_V), layer),
                  row(LANES), row(LANES)],
        out_specs=[row(RET_QK), row(RET_QK), row(RET_V)],
        out_shape=[qk, qk, jax.ShapeDtypeStruct((B, T, RET_V), BF16)],
        compiler_params=_params("parallel", "parallel"),
        name="odd_front",
    )(x, g, w_all, c, s)


def _log_sigmoid(x):
    return jnp.minimum(x, 0.0) - jnp.log1p(jnp.exp(-jnp.abs(x)))


def _retention_kernel(q_ref, k_ref, v_ref, df_ref, db_ref, gn_ref, o_ref, sf_ref, acc_ref):
    T = q_ref.shape[1]
    C = min(T, RET_CHUNK)
    n_chunks = T // C
    heads = range(q_ref.shape[2] // RET_DK)
    ri = lax.broadcasted_iota(jnp.int32, (C, 1), 0).astype(F32)
    diff = (lax.broadcasted_iota(jnp.int32, (C, C), 0)
            - lax.broadcasted_iota(jnp.int32, (C, C), 1)).astype(F32)

    def decays(hd):
        lf1 = _log_sigmoid(df_ref[hd])[:, :1]
        lb1 = _log_sigmoid(db_ref[hd])[:, :1]
        return dict(
            dmat=jnp.exp(jnp.where(diff >= 0, diff * lf1, -diff * lb1)),
            xi_f=jnp.exp((ri + 1.0) * lf1),
            xi_b=jnp.exp((C - ri) * lb1),
            zeta_f=jnp.exp((C - 1.0 - ri) * lf1),
            zeta_b=jnp.exp(ri * lb1),
            cd_f=jnp.exp(C * lf1), cd_b=jnp.exp(C * lb1))

    dec = [decays(hd) for hd in heads]

    def state_update(hd, i, zeta, cd, first):
        rows = slice(i * C, (i + 1) * C)
        kz = (k_ref[0, rows, hd * RET_DK:(hd + 1) * RET_DK].astype(F32) * zeta).astype(BF16)
        upd = _dot_tn(kz, v_ref[0, rows, hd * RET_DV:(hd + 1) * RET_DV])
        acc_ref[hd] = upd if first else acc_ref[hd] * cd + upd

    for i in range(n_chunks - 1):
        for hd in heads:
            state_update(hd, i, dec[hd]["zeta_f"], dec[hd]["cd_f"], first=(i == 0))
            sf_ref[hd, i + 1] = acc_ref[hd].astype(BF16)

    for i in reversed(range(n_chunks)):
        rows = slice(i * C, (i + 1) * C)
        for hd in heads:
            d = dec[hd]
            vcols = slice(hd * RET_DV, (hd + 1) * RET_DV)
            q = q_ref[0, rows, hd * RET_DK:(hd + 1) * RET_DK]
            qf = q.astype(F32)
            s = (_dot_nt(q, k_ref[0, rows, hd * RET_DK:(hd + 1) * RET_DK]) * d["dmat"]).astype(BF16)
            o = _dot(s, v_ref[0, rows, vcols])
            if i > 0:
                o = o + _dot((qf * d["xi_f"]).astype(BF16), sf_ref[hd, i])
            if i < n_chunks - 1:
                o = o + _dot((qf * d["xi_b"]).astype(BF16), acc_ref[hd].astype(BF16))
            mu = jnp.mean(o, axis=-1, keepdims=True)
            oc = o - mu
            var = jnp.mean(oc * oc, axis=-1, keepdims=True)
            o_ref[0, rows, vcols] = (oc * lax.rsqrt(var + NORM_EPS) * gn_ref[:, vcols]).astype(BF16)
            if i > 0:
                state_update(hd, i, d["zeta_b"], d["cd_b"], first=(i == n_chunks - 1))


def _retention(q, k, v, dec_f, dec_b, gn_g):
    B, T, _ = q.shape
    C = min(T, RET_CHUNK)
    hs = RET_HEADS_PER_STEP
    head = lambda w: pl.BlockSpec((1, T, hs * w), lambda b, h: (b, 0, h))
    dec = pl.BlockSpec((hs, 1, LANES), lambda b, h: (h, 0, 0))
    return pl.pallas_call(
        _retention_kernel,
        grid=(B, RET_HEADS // hs),
        in_specs=[head(RET_DK), head(RET_DK), head(RET_DV), dec, dec,
                  pl.BlockSpec((1, hs * RET_DV), lambda b, h: (0, h))],
        out_specs=head(RET_DV),
        out_shape=jax.ShapeDtypeStruct((B, T, RET_V), BF16),
        scratch_shapes=[pltpu.VMEM((hs, T // C, RET_DK, RET_DV), BF16),
                        pltpu.VMEM((hs, RET_DK, RET_DV), F32)],
        compiler_params=_params("parallel", "parallel"),
        name="retention",
    )(q, k, v, dec_f, dec_b, gn_g)


def _odd_back_kernel(x_ref, g_ref, wg_ref, a_ref, wo_ref, fg_ref, o_ref, *, final_norm):
    for rows in _sub_tiles(x_ref.shape[1]):
        x = x_ref[0, rows, :]
        h = _rms(x, g_ref[...]).astype(BF16)
        gate = _dot(h, wg_ref[...])
        y = (a_ref[0, rows, :].astype(F32) * _silu(gate)).astype(BF16)
        out = x + _dot(y, wo_ref[...])
        if final_norm:
            out = _rms(out, fg_ref[...])
        o_ref[0, rows, :] = out


def _odd_back(x, g, w_all, a, wo_all, layer, fg, final_norm):
    B, T, D = x.shape
    tm = min(T, ROW_TILE)
    row = lambda w: pl.BlockSpec((1, tm, w), lambda b, t: (b, t, 0))
    return pl.pallas_call(
        functools.partial(_odd_back_kernel, final_norm=final_norm),
        grid=(B, T // tm),
        in_specs=[row(D), _const_spec(g.shape),
                  _layer_spec((D, RET_V), layer, col_block=(2 * RET_QK + RET_V) // RET_V),
                  row(RET_V), _layer_spec(wo_all.shape[1:], layer), _const_spec(fg.shape)],
        out_specs=row(D),
        out_shape=jax.ShapeDtypeStruct((B, T, D), F32),
        compiler_params=_params("parallel", "parallel"),
        name="odd_back",
    )(x, g, w_all, a, wo_all, fg)


def _even_weights(w_in, w_uq, w_ukv):
    n_lat = MLA_Q_LORA + MLA_KV_LORA
    wlat = jnp.concatenate(
        [w_in[:, :n_lat + MLA_ROPE], jnp.zeros((D_MODEL, LANES - MLA_ROPE), w_in.dtype)], axis=1)
    wu = w_in[:, n_lat + MLA_ROPE:n_lat + MLA_ROPE + MIX_B]
    wg = w_in[:, n_lat + MLA_ROPE + MIX_B:]
    uq = w_uq.reshape(MLA_Q_LORA, MLA_HEADS, MLA_NOPE + MLA_ROPE)
    uq_nope = uq[:, :, :MLA_NOPE].reshape(MLA_Q_LORA, MIX_A)
    uq_rope_t = uq[:, :, MLA_NOPE:].reshape(MLA_Q_LORA, MLA_HEADS * MLA_ROPE).T
    ukv = w_ukv.reshape(MLA_KV_LORA, MLA_HEADS, MLA_NOPE + MLA_V)
    uk = ukv[:, :, :MLA_NOPE].reshape(MLA_KV_LORA, MIX_A)
    uvt = ukv[:, :, MLA_NOPE:].reshape(MLA_KV_LORA, MIX_A).T
    return (wu, uq_nope, uk) + tuple(w.astype(BF16) for w in (wlat, wg, uq_rope_t, uvt))


def kernel(x, positions, a_norm_g, a_w_in, a_q_norm_g, a_w_uq, a_kv_norm_g, a_w_ukv, a_pool_w,
           a_pool_scale, a_w_out, r_norm_g, r_w_in, r_decay_fwd, r_decay_bwd, r_gn_g, r_w_out,
           final_norm_g):
    depth = a_norm_g.shape[0] + r_norm_g.shape[0]
    assert depth % 2 == 0, "the final norm is fused into the last (odd) layer's back kernel"
    ca, sa, cat, sat, cr, sr = _rope_tables(positions)
    fg = final_norm_g[None]
    a_w_out, r_w_in, r_w_out = (w.astype(BF16) for w in (a_w_out, r_w_in, r_w_out))
    for layer in range(depth):
        i = layer // 2
        if layer % 2 == 0:
            wu, uq_nope, uk, wlat, wg, uq_rope_t, wuvt = _even_weights(a_w_in[i], a_w_uq[i], a_w_ukv[i])
            g = a_norm_g[i][None]
            wuqt = jnp.concatenate([_q_fold(uq_nope, uk), uq_rope_t], axis=0)
            qt, kl, vt = _even_front(x, g, wlat, a_q_norm_g[i][None], wuqt, a_kv_norm_g[i][None],
                                     wuvt, ca, sa, cat, sat)
            a = _attention(qt, kl, vt)
            wu = _pool_fold(wu, a_pool_w[i], a_pool_scale[i][None])
            x = _even_back(x, g, wu, wg, a, a_w_out, i)
        else:
            g = r_norm_g[i][None]
            q, k, v = _odd_front(x, g, r_w_in, i, cr, sr)
            dec_f = jnp.broadcast_to(r_decay_fwd[i][:, None, None], (RET_HEADS, 1, LANES))
            dec_b = jnp.broadcast_to(r_decay_bwd[i][:, None, None], (RET_HEADS, 1, LANES))
            o = _retention(q, k, v, dec_f, dec_b, r_gn_g[i][None])
            x = _odd_back(x, g, r_w_in, o, r_w_out, i, fg, final_norm=(layer == depth - 1))
    return x
```

```python
import functools

import jax
import jax.numpy as jnp
from jax import lax
from jax.experimental import pallas as pl
from jax.experimental.pallas import tpu as pltpu

ROPE_BASE = 10000.0
NORM_EPS = 1e-6
LOG2E = 1.4426950408889634

D_MODEL = 1024
MLA_HEADS = 8
MLA_NOPE = 128
MLA_ROPE = 64
MLA_V = 128
MLA_Q_LORA = 384
MLA_KV_LORA = 128
POOL_WINDOWS = (2, 4, 8, 16)
POOL_DIM = 256
MIX_A = MLA_HEADS * MLA_V
MIX_B = len(POOL_WINDOWS) * POOL_DIM
RET_HEADS = 4
RET_DK = 256
RET_DV = 512
RET_QK = RET_HEADS * RET_DK
RET_V = RET_HEADS * RET_DV

LANES = 128
SUBLANES = 8
HEAD_PAD = 2 * LANES
Q_ROWS = MLA_HEADS * (MLA_NOPE + MLA_ROPE)
POOL_HALO = SUBLANES
VMEM_LIMIT = 56 * 1024 * 1024

ROW_TILE = 1024
EVEN_FRONT_TILE = 2048
SUB_TILE = 512
ATTN_KV_CHUNK = 256
ATTN_HEADS_PER_STEP = 8
ATTN_GROUP = 2
RET_CHUNK = 256
RET_HEADS_PER_STEP = 2

BF16 = jnp.bfloat16
F32 = jnp.float32


def _params(*sem, **kw):
    return pltpu.CompilerParams(dimension_semantics=sem, vmem_limit_bytes=VMEM_LIMIT, **kw)


def _const_spec(shape):
    nd = len(shape)
    return pl.BlockSpec(shape, lambda *_: (0,) * nd, pipeline_mode=pl.Buffered(1))


def _layer_spec(block, layer, col_block=0):
    idx = (layer,) + (0,) * (len(block) - 1) + (col_block,)
    return pl.BlockSpec((None,) + tuple(block), lambda *_: idx, pipeline_mode=pl.Buffered(1))


def _sub_tiles(rows):
    sub = min(rows, SUB_TILE)
    return [slice(r, r + sub) for r in range(0, rows, sub)]


def _rms(x, g):
    return x * lax.rsqrt(jnp.mean(x * x, axis=-1, keepdims=True) + NORM_EPS) * g


def _silu(x):
    return x / (1.0 + jnp.exp(-x))


def _dot(a, b):
    return jnp.dot(a, b, preferred_element_type=F32)


def _dot_nt(a, b):
    return lax.dot_general(a, b, (((1,), (1,)), ((), ())), preferred_element_type=F32)


def _dot_tn(a, b):
    return lax.dot_general(a, b, (((0,), (0,)), ((), ())), preferred_element_type=F32)


def _split3(a):
    hi = a.astype(BF16)
    r1 = a - hi.astype(F32)
    mid = r1.astype(BF16)
    lo = (r1 - mid.astype(F32)).astype(BF16)
    return jnp.concatenate([hi, mid, lo], axis=1)


def _rope_tab_kernel(pos_ref, inv_r_ref, selc_ref, sels_ref, selt_ref,
                     ca_ref, sa_ref, cat_ref, sat_ref, cr_ref, sr_ref):
    tt = pos_ref.shape[2]
    sub = min(tt, 256)
    for r in range(0, tt, sub):
        rows = slice(r, r + sub)
        pos = jnp.broadcast_to(pos_ref[0, :, rows].astype(F32), (SUBLANES, sub)).T[:, :1]
        ang = pos * inv_r_ref[...]
        c = jnp.cos(ang)
        s = jnp.sin(ang)
        cr_ref[0, rows, :] = c
        sr_ref[0, rows, :] = s
        c3 = _split3(c)
        s3 = _split3(s)
        ca_ref[0, rows, :] = _dot(c3, selc_ref[...])
        sa_ref[0, rows, :] = _dot(s3, sels_ref[...])
        cat_ref[0, :, rows] = _dot_nt(selt_ref[...], c3)
        sat_ref[0, :, rows] = _dot_nt(selt_ref[...], s3)


def _rope_tables(positions):
    B, T = positions.shape
    half_a = MLA_ROPE // 2
    stride = RET_DK // MLA_ROPE
    inv_r = 1.0 / (ROPE_BASE ** (jnp.arange(0, RET_DK, 2, dtype=F32) / RET_DK))
    assert inv_r.shape[0] == LANES and stride * half_a == LANES
    pick = (jnp.arange(LANES)[:, None] == stride * jnp.arange(half_a)[None, :]).astype(BF16)
    zeros = jnp.zeros((LANES, LANES - 2 * half_a), BF16)
    sel_c = jnp.concatenate([pick, pick, zeros], axis=1)
    sel_s = jnp.concatenate([-pick, pick, zeros], axis=1)
    tile3 = lambda m: jnp.concatenate([m, m, m], axis=0)
    tt = min(T, 1024)
    tab = jax.ShapeDtypeStruct((B, T, LANES), F32)
    tab_t = jax.ShapeDtypeStruct((B, half_a, T), F32)
    blk = pl.BlockSpec((1, tt, LANES), lambda b, t: (b, t, 0))
    blk_t = pl.BlockSpec((1, half_a, tt), lambda b, t: (b, 0, t))
    return pl.pallas_call(
        _rope_tab_kernel,
        grid=(B, T // tt),
        in_specs=[pl.BlockSpec((1, 1, tt), lambda b, t: (b, 0, t)),
                  _const_spec((1, LANES)), _const_spec((3 * LANES, LANES)),
                  _const_spec((3 * LANES, LANES)), _const_spec((half_a, 3 * LANES))],
        out_specs=[blk, blk, blk_t, blk_t, blk, blk],
        out_shape=[tab, tab, tab_t, tab_t, tab, tab],
        compiler_params=_params("parallel", "parallel"),
        name="rope_tables",
    )(positions.reshape(B, 1, T), inv_r[None], tile3(sel_c), tile3(sel_s), tile3(pick).T)


def _rope_pad(v, c, s):
    q = LANES // 4
    return v * c + (pltpu.roll(v, 3 * q, 1) + pltpu.roll(v, q, 1)) * s


def _even_front_kernel(x_ref, g_ref, wlat_ref, gq_ref, wuqt_ref, gkv_ref, wuvt_ref,
                       ca_ref, sa_ref, cat_ref, sat_ref, qt_ref, kl_ref, vt_ref):
    q_scale = (MLA_NOPE + MLA_ROPE) ** -0.5 * LOG2E
    half = MLA_ROPE // 2
    for j, rows in enumerate(_sub_tiles(x_ref.shape[1])):
        h = _rms(x_ref[0, rows, :], g_ref[...]).astype(BF16)
        lat = _dot(h, wlat_ref[...])
        cq = _rms(lat[:, :MLA_Q_LORA], gq_ref[...]).astype(BF16)
        qt = _dot_nt(wuqt_ref[...], cq)
        ckv = _rms(lat[:, MLA_Q_LORA:MLA_Q_LORA + MLA_KV_LORA], gkv_ref[...]).astype(BF16)
        kl_ref[0, rows, :MLA_KV_LORA] = ckv
        vt_ref[0, :, rows] = _dot_nt(wuvt_ref[...], ckv).astype(BF16)
        kl_ref[0, rows, MLA_KV_LORA:] = _rope_pad(lat[:, MLA_Q_LORA + MLA_KV_LORA:],
                                                  ca_ref[0, rows, :], sa_ref[0, rows, :]).astype(BF16)
        ct = cat_ref[0, :, rows]
        st = sat_ref[0, :, rows]
        qt_ref[0, j, :MIX_A, :] = (qt[:MIX_A] * q_scale).astype(BF16)
        for hd in range(MLA_HEADS):
            r1 = MIX_A + hd * MLA_ROPE
            r2 = r1 + half
            r3 = r2 + half
            x1 = qt[r1:r2]
            x2 = qt[r2:r3]
            qt_ref[0, j, r1:r2, :] = ((x1 * ct - x2 * st) * q_scale).astype(BF16)
            qt_ref[0, j, r2:r3, :] = ((x2 * ct + x1 * st) * q_scale).astype(BF16)


def _q_fold_kernel(uk_ref, uq_ref, o_ref):
    o_ref[...] = lax.dot_general(uk_ref[...], uq_ref[...], (((1,), (1,)), ((), ())),
                                 precision=lax.Precision.HIGHEST,
                                 preferred_element_type=F32).astype(BF16)


def _q_fold(uq_nope, uk):
    return pl.pallas_call(
        _q_fold_kernel,
        grid=(MLA_HEADS,),
        in_specs=[pl.BlockSpec((MLA_KV_LORA, MLA_NOPE), lambda h: (0, h)),
                  pl.BlockSpec((MLA_Q_LORA, MLA_NOPE), lambda h: (0, h))],
        out_specs=pl.BlockSpec((MLA_KV_LORA, MLA_Q_LORA), lambda h: (h, 0)),
        out_shape=jax.ShapeDtypeStruct((MLA_HEADS * MLA_KV_LORA, MLA_Q_LORA), BF16),
        compiler_params=_params("parallel"),
        name="q_fold",
    )(uk, uq_nope)


def _even_front(x, g, wlat, gq, wuqt, gkv, wuvt, ca, sa, cat, sat):
    B, T, D = x.shape
    tm = min(T, EVEN_FRONT_TILE)
    sub = min(tm, SUB_TILE)
    row = lambda w: pl.BlockSpec((1, tm, w), lambda b, t: (b, t, 0))
    col = lambda r: pl.BlockSpec((1, r, tm), lambda b, t: (b, 0, t))
    consts = [g, wlat, gq, wuqt, gkv, wuvt]
    return pl.pallas_call(
        _even_front_kernel,
        grid=(B, T // tm),
        in_specs=[row(D)] + [_const_spec(c.shape) for c in consts]
        + [row(LANES), row(LANES), col(MLA_ROPE // 2), col(MLA_ROPE // 2)],
        out_specs=[pl.BlockSpec((1, tm // sub, Q_ROWS, sub), lambda b, t: (b, t, 0, 0)),
                   row(HEAD_PAD), col(MIX_A)],
        out_shape=[jax.ShapeDtypeStruct((B, T // sub, Q_ROWS, sub), BF16),
                   jax.ShapeDtypeStruct((B, T, HEAD_PAD), BF16),
                   jax.ShapeDtypeStruct((B, MIX_A, T), BF16)],
        compiler_params=_params("parallel", "parallel"),
        name="even_front",
    )(x, *consts, ca, sa, cat, sat)


def _attn_kernel(qn_ref, qr_ref, kl_ref, vt_ref, o_ref, s_ref, m_ref):
    n_q, tq = qn_ref.shape[1], qn_ref.shape[3]
    grp_blocks, n_kc, kc = s_ref.shape[1], s_ref.shape[2], s_ref.shape[3]
    n_blk = n_q * (qn_ref.shape[2] // MLA_NOPE)
    q_pad = jnp.zeros((HEAD_PAD - MLA_NOPE - MLA_ROPE, tq), BF16)
    n_grp = n_blk // grp_blocks
    sub = (kc // SUBLANES, SUBLANES, tq)

    def score_chunk(g, j, c, m8):
        hd, qb = divmod(g * grp_blocks + j, n_q)
        qt = jnp.concatenate([qn_ref[0, qb, hd * MLA_NOPE:(hd + 1) * MLA_NOPE, :],
                              qr_ref[0, qb, hd * MLA_ROPE:(hd + 1) * MLA_ROPE, :], q_pad], axis=0)
        s = _dot(kl_ref[0, c * kc:(c + 1) * kc, :], qt)
        s_ref[g % 2, j, c] = s
        cm = jnp.max(s.reshape(sub), axis=0)
        return cm if m8 is None else jnp.maximum(m8, cm)

    def value_chunk(g, j, c, m, l8, acc):
        hd = (g * grp_blocks + j) // n_q
        p = jnp.exp2(s_ref[g % 2, j, c] - m)
        ps = jnp.sum(p.reshape(sub), axis=0)
        vt = vt_ref[0, hd * MLA_V:(hd + 1) * MLA_V, c * kc:(c + 1) * kc]
        pv = _dot(vt, p.astype(BF16))
        return (ps if l8 is None else l8 + ps), (pv if acc is None else acc + pv)

    def stage(k):
        run_v = k >= 1
        run_s = k < n_grp
        blocks = range(grp_blocks)
        if run_v:
            m = [jnp.max(m_ref[(k - 1) % 2, j], axis=0, keepdims=True) for j in blocks]
        m8 = [None] * grp_blocks
        l8 = [None] * grp_blocks
        acc = [None] * grp_blocks
        for c in range(n_kc):
            for j in blocks:
                if run_v:
                    l8[j], acc[j] = value_chunk(k - 1, j, c, m[j], l8[j], acc[j])
                if run_s:
                    m8[j] = score_chunk(k, j, c, m8[j])
        for j in blocks:
            if run_s:
                m_ref[k % 2, j] = m8[j]
            if run_v:
                hd, qb = divmod((k - 1) * grp_blocks + j, n_q)
                l = jnp.sum(l8[j], axis=0, keepdims=True)
                o_ref[0, qb * tq:(qb + 1) * tq, hd * MLA_V:(hd + 1) * MLA_V] = (
                    (acc[j] / l).T.astype(BF16))

    one = jnp.minimum(pl.program_id(0) + 1, 1)
    for k in range(n_grp + 1):
        lax.fori_loop(0, one, lambda _, carry, k=k: (stage(k), carry)[1], 0)


def _attention(qt, kl, vt):
    B, T, _ = kl.shape
    n_q, tq = qt.shape[1], qt.shape[3]
    kc = min(T, ATTN_KV_CHUNK)
    hs = ATTN_HEADS_PER_STEP
    rope_blk0 = MIX_A // (hs * MLA_ROPE)
    return pl.pallas_call(
        _attn_kernel,
        grid=(B, MLA_HEADS // hs),
        in_specs=[pl.BlockSpec((1, n_q, hs * MLA_NOPE, tq), lambda b, h: (b, 0, h, 0)),
                  pl.BlockSpec((1, n_q, hs * MLA_ROPE, tq), lambda b, h: (b, 0, rope_blk0 + h, 0)),
                  pl.BlockSpec((1, T, HEAD_PAD), lambda b, h: (b, 0, 0)),
                  pl.BlockSpec((1, hs * MLA_V, T), lambda b, h: (b, h, 0))],
        out_specs=pl.BlockSpec((1, T, hs * MLA_V), lambda b, h: (b, 0, h)),
        out_shape=jax.ShapeDtypeStruct((B, T, MIX_A), BF16),
        scratch_shapes=[pltpu.VMEM((2, ATTN_GROUP, T // kc, kc, tq), F32),
                        pltpu.VMEM((2, ATTN_GROUP, SUBLANES, tq), F32)],
        compiler_params=_params("parallel", "parallel"),
        name="mla_attention",
    )(qt, qt, kl, vt)


def _pool_fold_kernel(wu_ref, pw_ref, ps_ref, o_ref):
    w = lax.dot_general(wu_ref[...], pw_ref[0], (((1,), (0,)), ((), ())),
                        precision=lax.Precision.HIGHEST, preferred_element_type=F32)
    o_ref[...] = (w * ps_ref[...]).astype(BF16)


def _pool_fold(wu, pw, ps):
    D = wu.shape[0]
    n_g = len(POOL_WINDOWS)
    return pl.pallas_call(
        _pool_fold_kernel,
        grid=(n_g,),
        in_specs=[pl.BlockSpec((D, POOL_DIM), lambda g: (0, g)),
                  pl.BlockSpec((1, POOL_DIM, POOL_DIM), lambda g: (g, 0, 0)),
                  pl.BlockSpec((1, POOL_DIM), lambda g: (0, g))],
        out_specs=pl.BlockSpec((D, POOL_DIM), lambda g: (0, g)),
        out_shape=jax.ShapeDtypeStruct((D, MIX_B), BF16),
        compiler_params=_params("parallel"),
        name="pool_fold",
    )(wu, pw, ps)


def _even_back_kernel(x_ref, xp_ref, xn_ref, g_ref, wu_ref, wg_ref, a_ref, wo_ref, o_ref, *, seq_len):
    tm = x_ref.shape[1]
    g = g_ref[...]
    wu = wu_ref[...]
    for rows in _sub_tiles(tm):
        sub = rows.stop - rows.start
        t0 = pl.program_id(1) * tm + rows.start
        x = x_ref[0, rows, :]
        h = _rms(x, g).astype(BF16)
        xp = xp_ref[0] if rows.start == 0 else x_ref[0, rows.start - POOL_HALO:rows.start, :]
        xn = xn_ref[0] if rows.stop == tm else x_ref[0, rows.stop:rows.stop + POOL_HALO, :]
        hp = _rms(xp, g).astype(BF16)
        hn = _rms(xn, g).astype(BF16)
        up = jnp.where(t0 > 0, _dot(hp, wu), 0.0)
        un = jnp.where(t0 + sub < seq_len, _dot(hn, wu), 0.0)
        u = _dot(h, wu)
        ue = jnp.concatenate([up, u, un], axis=0)
        ext = sub + 2 * POOL_HALO
        t = (t0 + lax.broadcasted_iota(jnp.int32, (sub, 1), 0))
        gate = _dot(h, wg_ref[...])
        sg = _silu(gate)
        ya = (a_ref[0, rows, :].astype(F32) * sg[:, :MIX_A]).astype(BF16)
        y = _dot(ya, wo_ref[:MIX_A, :])
        for gi, w in enumerate(POOL_WINDOWS):
            left = w // 2
            right = w - 1 - left
            cols = slice(gi * POOL_DIM, (gi + 1) * POOL_DIM)
            run = ue[:, cols]
            n = 1
            while n < left:
                run = run + pltpu.roll(run, ext - n, 0)
                n *= 2
            acc = (run + pltpu.roll(run, left, 0))[POOL_HALO:POOL_HALO + sub]
            cnt = (jnp.minimum(t + right, seq_len - 1) - jnp.maximum(t - left, 0) + 1).astype(F32)
            bg = acc / cnt - u[:, cols]
            yb = (bg * sg[:, MIX_A + gi * POOL_DIM:MIX_A + (gi + 1) * POOL_DIM]).astype(BF16)
            y = y + _dot(yb, wo_ref[MIX_A + gi * POOL_DIM:MIX_A + (gi + 1) * POOL_DIM, :])
        o_ref[0, rows, :] = x + y


def _even_back(x, g, wu, wg, a, wo_all, layer):
    B, T, D = x.shape
    tm = min(T, ROW_TILE)
    nb = tm // POOL_HALO
    last = T // POOL_HALO - 1
    row = lambda w: pl.BlockSpec((1, tm, w), lambda b, t: (b, t, 0))
    return pl.pallas_call(
        functools.partial(_even_back_kernel, seq_len=T),
        grid=(B, T // tm),
        in_specs=[row(D),
                  pl.BlockSpec((1, POOL_HALO, D), lambda b, t: (b, jnp.maximum(t * nb - 1, 0), 0)),
                  pl.BlockSpec((1, POOL_HALO, D), lambda b, t: (b, jnp.minimum((t + 1) * nb, last), 0)),
                  _const_spec(g.shape), _const_spec(wu.shape), _const_spec(wg.shape), row(MIX_A),
                  _layer_spec(wo_all.shape[1:], layer)],
        out_specs=row(D),
        out_shape=jax.ShapeDtypeStruct((B, T, D), F32),
        compiler_params=_params("parallel", "parallel",
                                allow_input_fusion=[False] * 5 + [True, False, True]),
        name="even_back",
    )(x, x, x, g, wu, wg, a, wo_all)


def _odd_front_kernel(x_ref, g_ref, w_ref, c_ref, s_ref, q_ref, k_ref, v_ref):
    half = RET_DK // 2
    k_scale = RET_DK ** -0.5
    for rows in _sub_tiles(x_ref.shape[1]):
        h = _rms(x_ref[0, rows, :], g_ref[...]).astype(BF16)
        qkv = _dot(h, w_ref[...])
        c = c_ref[0, rows, :]
        s = s_ref[0, rows, :]
        for hd in range(RET_HEADS):
            for base, ref, scale in ((0, q_ref, None), (RET_QK, k_ref, k_scale)):
                lo = base + hd * RET_DK
                x1 = qkv[:, lo:lo + half]
                x2 = qkv[:, lo + half:lo + RET_DK]
                o1 = x1 * c - x2 * s
                o2 = x2 * c + x1 * s
                if scale is not None:
                    o1 = o1 * scale
                    o2 = o2 * scale
                ref[0, rows, hd * RET_DK:hd * RET_DK + half] = o1.astype(BF16)
                ref[0, rows, hd * RET_DK + half:(hd + 1) * RET_DK] = o2.astype(BF16)
        v_ref[0, rows, :] = qkv[:, 2 * RET_QK:].astype(BF16)


def _odd_front(x, g, w_all, layer, c, s):
    B, T, D = x.shape
    tm = min(T, ROW_TILE)
    row = lambda wd: pl.BlockSpec((1, tm, wd), lambda b, t: (b, t, 0))
    qk = jax.ShapeDtypeStruct((B, T, RET_QK), BF16)
    return pl.pallas_call(
        _odd_front_kernel,
        grid=(B, T // tm),
        in_specs=[row(D), _const_spec(g.shape), _layer_spec((D, 2 * RET_QK + RET_V), layer),
                  row(LANES), row(LANES)],
        out_specs=[row(RET_QK), row(RET_QK), row(RET_V)],
        out_shape=[qk, qk, jax.ShapeDtypeStruct((B, T, RET_V), BF16)],
        compiler_params=_params("parallel", "parallel",
                                allow_input_fusion=[False, False, True, False, False]),
        name="odd_front",
    )(x, g, w_all, c, s)


def _log_sigmoid(x):
    return jnp.minimum(x, 0.0) - jnp.log1p(jnp.exp(-jnp.abs(x)))


def _retention_kernel(q_ref, k_ref, v_ref, df_ref, db_ref, gn_ref, o_ref, sf_ref, acc_ref):
    T = q_ref.shape[1]
    C = min(T, RET_CHUNK)
    n_chunks = T // C
    heads = range(q_ref.shape[2] // RET_DK)
    ri = lax.broadcasted_iota(jnp.int32, (C, 1), 0).astype(F32)
    diff = (lax.broadcasted_iota(jnp.int32, (C, C), 0)
            - lax.broadcasted_iota(jnp.int32, (C, C), 1)).astype(F32)

    def decays(hd):
        lf1 = _log_sigmoid(df_ref[hd])[:, :1]
        lb1 = _log_sigmoid(db_ref[hd])[:, :1]
        return dict(
            dmat=jnp.exp(jnp.where(diff >= 0, diff * lf1, -diff * lb1)),
            xi_f=jnp.exp((ri + 1.0) * lf1),
            xi_b=jnp.exp((C - ri) * lb1),
            zeta_f=jnp.exp((C - 1.0 - ri) * lf1),
            zeta_b=jnp.exp(ri * lb1),
            cd_f=jnp.exp(C * lf1), cd_b=jnp.exp(C * lb1))

    dec = [decays(hd) for hd in heads]

    def state_update(hd, i, zeta, cd, first):
        rows = slice(i * C, (i + 1) * C)
        kz = (k_ref[0, rows, hd * RET_DK:(hd + 1) * RET_DK].astype(F32) * zeta).astype(BF16)
        upd = _dot_tn(kz, v_ref[0, rows, hd * RET_DV:(hd + 1) * RET_DV])
        acc_ref[hd] = upd if first else acc_ref[hd] * cd + upd

    for i in range(n_chunks - 1):
        for hd in heads:
            state_update(hd, i, dec[hd]["zeta_f"], dec[hd]["cd_f"], first=(i == 0))
            sf_ref[hd, i + 1] = acc_ref[hd].astype(BF16)

    for i in reversed(range(n_chunks)):
        rows = slice(i * C, (i + 1) * C)
        for hd in heads:
            d = dec[hd]
            vcols = slice(hd * RET_DV, (hd + 1) * RET_DV)
            q = q_ref[0, rows, hd * RET_DK:(hd + 1) * RET_DK]
            qf = q.astype(F32)
            s = (_dot_nt(q, k_ref[0, rows, hd * RET_DK:(hd + 1) * RET_DK]) * d["dmat"]).astype(BF16)
            o = _dot(s, v_ref[0, rows, vcols])
            if i > 0:
                o = o + _dot((qf * d["xi_f"]).astype(BF16), sf_ref[hd, i])
            if i < n_chunks - 1:
                o = o + _dot((qf * d["xi_b"]).astype(BF16), acc_ref[hd].astype(BF16))
            mu = jnp.mean(o, axis=-1, keepdims=True)
            oc = o - mu
            var = jnp.mean(oc * oc, axis=-1, keepdims=True)
            o_ref[0, rows, vcols] = (oc * lax.rsqrt(var + NORM_EPS) * gn_ref[:, vcols]).astype(BF16)
            if i > 0:
                state_update(hd, i, d["zeta_b"], d["cd_b"], first=(i == n_chunks - 1))


def _retention(q, k, v, dec_f, dec_b, gn_g):
    B, T, _ = q.shape
    C = min(T, RET_CHUNK)
    hs = RET_HEADS_PER_STEP
    head = lambda w: pl.BlockSpec((1, T, hs * w), lambda b, h: (b, 0, h))
    dec = pl.BlockSpec((hs, 1, LANES), lambda b, h: (h, 0, 0))
    return pl.pallas_call(
        _retention_kernel,
        grid=(B, RET_HEADS // hs),
        in_specs=[head(RET_DK), head(RET_DK), head(RET_DV), dec, dec,
                  pl.BlockSpec((1, hs * RET_DV), lambda b, h: (0, h))],
        out_specs=head(RET_DV),
        out_shape=jax.ShapeDtypeStruct((B, T, RET_V), BF16),
        scratch_shapes=[pltpu.VMEM((hs, T // C, RET_DK, RET_DV), BF16),
                        pltpu.VMEM((hs, RET_DK, RET_DV), F32)],
        compiler_params=_params("parallel", "parallel"),
        name="retention",
    )(q, k, v, dec_f, dec_b, gn_g)


def _odd_back_kernel(x_ref, g_ref, wg_ref, a_ref, wo_ref, fg_ref, o_ref, *, final_norm):
    for rows in _sub_tiles(x_ref.shape[1]):
        x = x_ref[0, rows, :]
        h = _rms(x, g_ref[...]).astype(BF16)
        gate = _dot(h, wg_ref[...])
        y = (a_ref[0, rows, :].astype(F32) * _silu(gate)).astype(BF16)
        out = x + _dot(y, wo_ref[...])
        if final_norm:
            out = _rms(out, fg_ref[...])
        o_ref[0, rows, :] = out


def _odd_back(x, g, w_all, a, wo_all, layer, fg, final_norm):
    B, T, D = x.shape
    tm = min(T, ROW_TILE)
    row = lambda w: pl.BlockSpec((1, tm, w), lambda b, t: (b, t, 0))
    return pl.pallas_call(
        functools.partial(_odd_back_kernel, final_norm=final_norm),
        grid=(B, T // tm),
        in_specs=[row(D), _const_spec(g.shape),
                  _layer_spec((D, RET_V), layer, col_block=(2 * RET_QK + RET_V) // RET_V),
                  row(RET_V), _layer_spec(wo_all.shape[1:], layer), _const_spec(fg.shape)],
        out_specs=row(D),
        out_shape=jax.ShapeDtypeStruct((B, T, D), F32),
        compiler_params=_params("parallel", "parallel",
                                allow_input_fusion=[False, False, True, False, True, False]),
        name="odd_back",
    )(x, g, w_all, a, wo_all, fg)


def _even_weights(w_in, w_uq, w_ukv):
    n_lat = MLA_Q_LORA + MLA_KV_LORA
    wlat = jnp.concatenate(
        [w_in[:, :n_lat + MLA_ROPE], jnp.zeros((D_MODEL, LANES - MLA_ROPE), w_in.dtype)], axis=1)
    wu = w_in[:, n_lat + MLA_ROPE:n_lat + MLA_ROPE + MIX_B]
    wg = w_in[:, n_lat + MLA_ROPE + MIX_B:]
    uq = w_uq.reshape(MLA_Q_LORA, MLA_HEADS, MLA_NOPE + MLA_ROPE)
    uq_nope = uq[:, :, :MLA_NOPE].reshape(MLA_Q_LORA, MIX_A)
    uq_rope_t = uq[:, :, MLA_NOPE:].reshape(MLA_Q_LORA, MLA_HEADS * MLA_ROPE).T
    ukv = w_ukv.reshape(MLA_KV_LORA, MLA_HEADS, MLA_NOPE + MLA_V)
    uk = ukv[:, :, :MLA_NOPE].reshape(MLA_KV_LORA, MIX_A)
    uvt = ukv[:, :, MLA_NOPE:].reshape(MLA_KV_LORA, MIX_A).T
    return (wu, uq_nope, uk) + tuple(w.astype(BF16) for w in (wlat, wg, uq_rope_t, uvt))


def kernel(x, positions, a_norm_g, a_w_in, a_q_norm_g, a_w_uq, a_kv_norm_g, a_w_ukv, a_pool_w,
           a_pool_scale, a_w_out, r_norm_g, r_w_in, r_decay_fwd, r_decay_bwd, r_gn_g, r_w_out,
           final_norm_g):
    depth = a_norm_g.shape[0] + r_norm_g.shape[0]
    assert depth % 2 == 0, "the final norm is fused into the last (odd) layer's back kernel"
    ca, sa, cat, sat, cr, sr = _rope_tables(positions)
    fg = final_norm_g[None]
    a_w_out, r_w_in, r_w_out = (w.astype(BF16) for w in (a_w_out, r_w_in, r_w_out))
    for layer in range(depth):
        i = layer // 2
        if layer % 2 == 0:
            wu, uq_nope, uk, wlat, wg, uq_rope_t, wuvt = _even_weights(a_w_in[i], a_w_uq[i], a_w_ukv[i])
            g = a_norm_g[i][None]
            wuqt = jnp.concatenate([_q_fold(uq_nope, uk), uq_rope_t], axis=0)
            qt, kl, vt = _even_front(x, g, wlat, a_q_norm_g[i][None], wuqt, a_kv_norm_g[i][None],
                                     wuvt, ca, sa, cat, sat)
            a = _attention(qt, kl, vt)
            wu = _pool_fold(wu, a_pool_w[i], a_pool_scale[i][None])
            x = _even_back(x, g, wu, wg, a, a_w_out, i)
        else:
            g = r_norm_g[i][None]
            q, k, v = _odd_front(x, g, r_w_in, i, cr, sr)
            dec_f = jnp.broadcast_to(r_decay_fwd[i][:, None, None], (RET_HEADS, 1, LANES))
            dec_b = jnp.broadcast_to(r_decay_bwd[i][:, None, None], (RET_HEADS, 1, LANES))
            o = _retention(q, k, v, dec_f, dec_b, r_gn_g[i][None])
            x = _odd_back(x, g, r_w_in, o, r_w_out, i, fg, final_norm=(layer == depth - 1))
    return x
```

```python
import functools

import jax
import jax.numpy as jnp
from jax import lax
from jax.experimental import pallas as pl
from jax.experimental.pallas import tpu as pltpu

ROPE_BASE = 10000.0
NORM_EPS = 1e-6
LOG2E = 1.4426950408889634

D_MODEL = 1024
MLA_HEADS = 8
MLA_NOPE = 128
MLA_ROPE = 64
MLA_V = 128
MLA_Q_LORA = 384
MLA_KV_LORA = 128
POOL_WINDOWS = (2, 4, 8, 16)
POOL_DIM = 256
MIX_A = MLA_HEADS * MLA_V
MIX_B = len(POOL_WINDOWS) * POOL_DIM
RET_HEADS = 4
RET_DK = 256
RET_DV = 512
RET_QK = RET_HEADS * RET_DK
RET_V = RET_HEADS * RET_DV

LANES = 128
SUBLANES = 8
HEAD_PAD = 2 * LANES
Q_ROWS = MLA_HEADS * (MLA_NOPE + MLA_ROPE)
POOL_HALO = SUBLANES
VMEM_LIMIT = 56 * 1024 * 1024
SMALL_CONST_ELEMS = 64 * 1024

ROW_TILE = 1024
EVEN_FRONT_TILE = 2048
SUB_TILE = 512
ATTN_KV_CHUNK = 256
ATTN_HEADS_PER_STEP = 8
ATTN_GROUP = 2
RET_CHUNK = 256
RET_HEADS_PER_STEP = 2

BF16 = jnp.bfloat16
F32 = jnp.float32


def _params(*sem, **kw):
    return pltpu.CompilerParams(dimension_semantics=sem, vmem_limit_bytes=VMEM_LIMIT, **kw)


def _const_spec(shape):
    nd = len(shape)
    small = functools.reduce(lambda a, b: a * b, shape, 1) <= SMALL_CONST_ELEMS
    mode = {} if small else {"pipeline_mode": pl.Buffered(1)}
    return pl.BlockSpec(shape, lambda *_: (0,) * nd, **mode)


def _layer_spec(block, layer, col_block=0):
    idx = (layer,) + (0,) * (len(block) - 1) + (col_block,)
    return pl.BlockSpec((None,) + tuple(block), lambda *_: idx, pipeline_mode=pl.Buffered(1))


def _sub_tiles(rows):
    sub = min(rows, SUB_TILE)
    return [slice(r, r + sub) for r in range(0, rows, sub)]


def _rms(x, g):
    return x * lax.rsqrt(jnp.mean(x * x, axis=-1, keepdims=True) + NORM_EPS) * g


def _silu(x):
    return x / (1.0 + jnp.exp(-x))


def _dot(a, b):
    return jnp.dot(a, b, preferred_element_type=F32)


def _dot_nt(a, b):
    return lax.dot_general(a, b, (((1,), (1,)), ((), ())), preferred_element_type=F32)


def _dot_tn(a, b):
    return lax.dot_general(a, b, (((0,), (0,)), ((), ())), preferred_element_type=F32)


def _split3(a):
    hi = a.astype(BF16)
    r1 = a - hi.astype(F32)
    mid = r1.astype(BF16)
    lo = (r1 - mid.astype(F32)).astype(BF16)
    return jnp.concatenate([hi, mid, lo], axis=1)


def _rope_tab_kernel(pos_ref, inv_r_ref, selc_ref, sels_ref, selt_ref,
                     ca_ref, sa_ref, cat_ref, sat_ref, cr_ref, sr_ref):
    tt = pos_ref.shape[2]
    sub = min(tt, 256)
    for r in range(0, tt, sub):
        rows = slice(r, r + sub)
        pos = jnp.broadcast_to(pos_ref[0, :, rows].astype(F32), (SUBLANES, sub)).T[:, :1]
        ang = pos * inv_r_ref[...]
        c = jnp.cos(ang)
        s = jnp.sin(ang)
        cr_ref[0, rows, :] = c
        sr_ref[0, rows, :] = s
        c3 = _split3(c)
        s3 = _split3(s)
        ca_ref[0, rows, :] = _dot(c3, selc_ref[...])
        sa_ref[0, rows, :] = _dot(s3, sels_ref[...])
        cat_ref[0, :, rows] = _dot_nt(selt_ref[...], c3)
        sat_ref[0, :, rows] = _dot_nt(selt_ref[...], s3)


def _rope_tables(positions):
    B, T = positions.shape
    half_a = MLA_ROPE // 2
    stride = RET_DK // MLA_ROPE
    inv_r = 1.0 / (ROPE_BASE ** (jnp.arange(0, RET_DK, 2, dtype=F32) / RET_DK))
    assert inv_r.shape[0] == LANES and stride * half_a == LANES
    pick = (jnp.arange(LANES)[:, None] == stride * jnp.arange(half_a)[None, :]).astype(BF16)
    zeros = jnp.zeros((LANES, LANES - 2 * half_a), BF16)
    sel_c = jnp.concatenate([pick, pick, zeros], axis=1)
    sel_s = jnp.concatenate([-pick, pick, zeros], axis=1)
    tile3 = lambda m: jnp.concatenate([m, m, m], axis=0)
    tt = min(T, 1024)
    tab = jax.ShapeDtypeStruct((B, T, LANES), F32)
    tab_t = jax.ShapeDtypeStruct((B, half_a, T), F32)
    blk = pl.BlockSpec((1, tt, LANES), lambda b, t: (b, t, 0))
    blk_t = pl.BlockSpec((1, half_a, tt), lambda b, t: (b, 0, t))
    return pl.pallas_call(
        _rope_tab_kernel,
        grid=(B, T // tt),
        in_specs=[pl.BlockSpec((1, 1, tt), lambda b, t: (b, 0, t)),
                  _const_spec((1, LANES)), _const_spec((3 * LANES, LANES)),
                  _const_spec((3 * LANES, LANES)), _const_spec((half_a, 3 * LANES))],
        out_specs=[blk, blk, blk_t, blk_t, blk, blk],
        out_shape=[tab, tab, tab_t, tab_t, tab, tab],
        compiler_params=_params("parallel", "parallel"),
        name="rope_tables",
    )(positions.reshape(B, 1, T), inv_r[None], tile3(sel_c), tile3(sel_s), tile3(pick).T)


def _rope_pad(v, c, s):
    q = LANES // 4
    return v * c + (pltpu.roll(v, 3 * q, 1) + pltpu.roll(v, q, 1)) * s


def _even_front_kernel(x_ref, g_ref, wlat_ref, gq_ref, wuqt_ref, gkv_ref, wuvt_ref,
                       ca_ref, sa_ref, cat_ref, sat_ref, qt_ref, kl_ref, vt_ref):
    q_scale = (MLA_NOPE + MLA_ROPE) ** -0.5 * LOG2E
    half = MLA_ROPE // 2
    for j, rows in enumerate(_sub_tiles(x_ref.shape[1])):
        h = _rms(x_ref[0, rows, :], g_ref[...]).astype(BF16)
        lat = _dot(h, wlat_ref[...])
        cq = _rms(lat[:, :MLA_Q_LORA], gq_ref[...]).astype(BF16)
        qt = _dot_nt(wuqt_ref[...], cq)
        ckv = _rms(lat[:, MLA_Q_LORA:MLA_Q_LORA + MLA_KV_LORA], gkv_ref[...]).astype(BF16)
        kl_ref[0, rows, :MLA_KV_LORA] = ckv
        vt_ref[0, :, rows] = _dot_nt(wuvt_ref[...], ckv).astype(BF16)
        kl_ref[0, rows, MLA_KV_LORA:] = _rope_pad(lat[:, MLA_Q_LORA + MLA_KV_LORA:],
                                                  ca_ref[0, rows, :], sa_ref[0, rows, :]).astype(BF16)
        ct = cat_ref[0, :, rows]
        st = sat_ref[0, :, rows]
        qt_ref[0, j, :MIX_A, :] = (qt[:MIX_A] * q_scale).astype(BF16)
        for hd in range(MLA_HEADS):
            r1 = MIX_A + hd * MLA_ROPE
            r2 = r1 + half
            r3 = r2 + half
            x1 = qt[r1:r2]
            x2 = qt[r2:r3]
            qt_ref[0, j, r1:r2, :] = ((x1 * ct - x2 * st) * q_scale).astype(BF16)
            qt_ref[0, j, r2:r3, :] = ((x2 * ct + x1 * st) * q_scale).astype(BF16)


def _q_fold_kernel(uk_ref, uq_ref, o_ref):
    o_ref[...] = lax.dot_general(uk_ref[...], uq_ref[...], (((1,), (1,)), ((), ())),
                                 precision=lax.Precision.HIGHEST,
                                 preferred_element_type=F32).astype(BF16)


def _q_fold(uq_nope, uk):
    return pl.pallas_call(
        _q_fold_kernel,
        grid=(MLA_HEADS,),
        in_specs=[pl.BlockSpec((MLA_KV_LORA, MLA_NOPE), lambda h: (0, h)),
                  pl.BlockSpec((MLA_Q_LORA, MLA_NOPE), lambda h: (0, h))],
        out_specs=pl.BlockSpec((MLA_KV_LORA, MLA_Q_LORA), lambda h: (h, 0)),
        out_shape=jax.ShapeDtypeStruct((MLA_HEADS * MLA_KV_LORA, MLA_Q_LORA), BF16),
        compiler_params=_params("parallel"),
        name="q_fold",
    )(uk, uq_nope)


def _even_front(x, g, wlat, gq, wuqt, gkv, wuvt, ca, sa, cat, sat):
    B, T, D = x.shape
    tm = min(T, EVEN_FRONT_TILE)
    sub = min(tm, SUB_TILE)
    row = lambda w: pl.BlockSpec((1, tm, w), lambda b, t: (b, t, 0))
    col = lambda r: pl.BlockSpec((1, r, tm), lambda b, t: (b, 0, t))
    consts = [g, wlat, gq, wuqt, gkv, wuvt]
    return pl.pallas_call(
        _even_front_kernel,
        grid=(B, T // tm),
        in_specs=[row(D)] + [_const_spec(c.shape) for c in consts]
        + [row(LANES), row(LANES), col(MLA_ROPE // 2), col(MLA_ROPE // 2)],
        out_specs=[pl.BlockSpec((1, tm // sub, Q_ROWS, sub), lambda b, t: (b, t, 0, 0)),
                   row(HEAD_PAD), col(MIX_A)],
        out_shape=[jax.ShapeDtypeStruct((B, T // sub, Q_ROWS, sub), BF16),
                   jax.ShapeDtypeStruct((B, T, HEAD_PAD), BF16),
                   jax.ShapeDtypeStruct((B, MIX_A, T), BF16)],
        compiler_params=_params("parallel", "parallel"),
        name="even_front",
    )(x, *consts, ca, sa, cat, sat)


def _attn_kernel(qn_ref, qr_ref, kl_ref, vt_ref, o_ref, s_ref, m_ref):
    n_q, tq = qn_ref.shape[1], qn_ref.shape[3]
    grp_blocks, n_kc, kc = s_ref.shape[1], s_ref.shape[2], s_ref.shape[3]
    n_blk = n_q * (qn_ref.shape[2] // MLA_NOPE)
    q_pad = jnp.zeros((HEAD_PAD - MLA_NOPE - MLA_ROPE, tq), BF16)
    n_grp = n_blk // grp_blocks
    sub = (kc // SUBLANES, SUBLANES, tq)

    def score_chunk(g, j, c, m8):
        hd, qb = divmod(g * grp_blocks + j, n_q)
        qt = jnp.concatenate([qn_ref[0, qb, hd * MLA_NOPE:(hd + 1) * MLA_NOPE, :],
                              qr_ref[0, qb, hd * MLA_ROPE:(hd + 1) * MLA_ROPE, :], q_pad], axis=0)
        s = _dot(kl_ref[0, c * kc:(c + 1) * kc, :], qt)
        s_ref[g % 2, j, c] = s
        cm = jnp.max(s.reshape(sub), axis=0)
        return cm if m8 is None else jnp.maximum(m8, cm)

    def value_chunk(g, j, c, m, l8, acc):
        hd = (g * grp_blocks + j) // n_q
        p = jnp.exp2(s_ref[g % 2, j, c] - m)
        ps = jnp.sum(p.reshape(sub), axis=0)
        vt = vt_ref[0, hd * MLA_V:(hd + 1) * MLA_V, c * kc:(c + 1) * kc]
        pv = _dot(vt, p.astype(BF16))
        return (ps if l8 is None else l8 + ps), (pv if acc is None else acc + pv)

    def stage(k):
        run_v = k >= 1
        run_s = k < n_grp
        blocks = range(grp_blocks)
        if run_v:
            m = [jnp.max(m_ref[(k - 1) % 2, j], axis=0, keepdims=True) for j in blocks]
        m8 = [None] * grp_blocks
        l8 = [None] * grp_blocks
        acc = [None] * grp_blocks
        for c in range(n_kc):
            for j in blocks:
                if run_v:
                    l8[j], acc[j] = value_chunk(k - 1, j, c, m[j], l8[j], acc[j])
                if run_s:
                    m8[j] = score_chunk(k, j, c, m8[j])
        for j in blocks:
            if run_s:
                m_ref[k % 2, j] = m8[j]
            if run_v:
                hd, qb = divmod((k - 1) * grp_blocks + j, n_q)
                l = jnp.sum(l8[j], axis=0, keepdims=True)
                o_ref[0, qb * tq:(qb + 1) * tq, hd * MLA_V:(hd + 1) * MLA_V] = (
                    (acc[j] / l).T.astype(BF16))

    one = jnp.minimum(pl.program_id(0) + 1, 1)
    for k in range(n_grp + 1):
        lax.fori_loop(0, one, lambda _, carry, k=k: (stage(k), carry)[1], 0)


def _attention(qt, kl, vt):
    B, T, _ = kl.shape
    n_q, tq = qt.shape[1], qt.shape[3]
    kc = min(T, ATTN_KV_CHUNK)
    hs = ATTN_HEADS_PER_STEP
    rope_blk0 = MIX_A // (hs * MLA_ROPE)
    return pl.pallas_call(
        _attn_kernel,
        grid=(B, MLA_HEADS // hs),
        in_specs=[pl.BlockSpec((1, n_q, hs * MLA_NOPE, tq), lambda b, h: (b, 0, h, 0)),
                  pl.BlockSpec((1, n_q, hs * MLA_ROPE, tq), lambda b, h: (b, 0, rope_blk0 + h, 0)),
                  pl.BlockSpec((1, T, HEAD_PAD), lambda b, h: (b, 0, 0)),
                  pl.BlockSpec((1, hs * MLA_V, T), lambda b, h: (b, h, 0))],
        out_specs=pl.BlockSpec((1, T, hs * MLA_V), lambda b, h: (b, 0, h)),
        out_shape=jax.ShapeDtypeStruct((B, T, MIX_A), BF16),
        scratch_shapes=[pltpu.VMEM((2, ATTN_GROUP, T // kc, kc, tq), F32),
                        pltpu.VMEM((2, ATTN_GROUP, SUBLANES, tq), F32)],
        compiler_params=_params("parallel", "parallel"),
        name="mla_attention",
    )(qt, qt, kl, vt)


def _pool_fold_kernel(wu_ref, pw_ref, ps_ref, o_ref):
    w = lax.dot_general(wu_ref[...], pw_ref[0], (((1,), (0,)), ((), ())),
                        precision=lax.Precision.HIGHEST, preferred_element_type=F32)
    o_ref[...] = (w * ps_ref[...]).astype(BF16)


def _pool_fold(wu, pw, ps):
    D = wu.shape[0]
    n_g = len(POOL_WINDOWS)
    return pl.pallas_call(
        _pool_fold_kernel,
        grid=(n_g,),
        in_specs=[pl.BlockSpec((D, POOL_DIM), lambda g: (0, g)),
                  pl.BlockSpec((1, POOL_DIM, POOL_DIM), lambda g: (g, 0, 0)),
                  pl.BlockSpec((1, POOL_DIM), lambda g: (0, g))],
        out_specs=pl.BlockSpec((D, POOL_DIM), lambda g: (0, g)),
        out_shape=jax.ShapeDtypeStruct((D, MIX_B), BF16),
        compiler_params=_params("parallel"),
        name="pool_fold",
    )(wu, pw, ps)


def _even_back_kernel(x_ref, xp_ref, xn_ref, g_ref, wu_ref, wg_ref, a_ref, wo_ref, o_ref, *, seq_len):
    tm = x_ref.shape[1]
    g = g_ref[...]
    wu = wu_ref[...]
    for rows in _sub_tiles(tm):
        sub = rows.stop - rows.start
        t0 = pl.program_id(1) * tm + rows.start
        x = x_ref[0, rows, :]
        h = _rms(x, g).astype(BF16)
        xp = xp_ref[0] if rows.start == 0 else x_ref[0, rows.start - POOL_HALO:rows.start, :]
        xn = xn_ref[0] if rows.stop == tm else x_ref[0, rows.stop:rows.stop + POOL_HALO, :]
        hp = _rms(xp, g).astype(BF16)
        hn = _rms(xn, g).astype(BF16)
        up = jnp.where(t0 > 0, _dot(hp, wu), 0.0)
        un = jnp.where(t0 + sub < seq_len, _dot(hn, wu), 0.0)
        u = _dot(h, wu)
        ue = jnp.concatenate([up, u, un], axis=0)
        ext = sub + 2 * POOL_HALO
        t = (t0 + lax.broadcasted_iota(jnp.int32, (sub, 1), 0))
        gate = _dot(h, wg_ref[...])
        sg = _silu(gate)
        ya = (a_ref[0, rows, :].astype(F32) * sg[:, :MIX_A]).astype(BF16)
        y = _dot(ya, wo_ref[:MIX_A, :])
        for gi, w in enumerate(POOL_WINDOWS):
            left = w // 2
            right = w - 1 - left
            cols = slice(gi * POOL_DIM, (gi + 1) * POOL_DIM)
            run = ue[:, cols]
            n = 1
            while n < left:
                run = run + pltpu.roll(run, ext - n, 0)
                n *= 2
            acc = (run + pltpu.roll(run, left, 0))[POOL_HALO:POOL_HALO + sub]
            cnt = (jnp.minimum(t + right, seq_len - 1) - jnp.maximum(t - left, 0) + 1).astype(F32)
            bg = acc / cnt - u[:, cols]
            yb = (bg * sg[:, MIX_A + gi * POOL_DIM:MIX_A + (gi + 1) * POOL_DIM]).astype(BF16)
            y = y + _dot(yb, wo_ref[MIX_A + gi * POOL_DIM:MIX_A + (gi + 1) * POOL_DIM, :])
        o_ref[0, rows, :] = x + y


def _even_back(x, g, wu, wg, a, wo_all, layer):
    B, T, D = x.shape
    tm = min(T, ROW_TILE)
    nb = tm // POOL_HALO
    last = T // POOL_HALO - 1
    row = lambda w: pl.BlockSpec((1, tm, w), lambda b, t: (b, t, 0))
    return pl.pallas_call(
        functools.partial(_even_back_kernel, seq_len=T),
        grid=(B, T // tm),
        in_specs=[row(D),
                  pl.BlockSpec((1, POOL_HALO, D), lambda b, t: (b, jnp.maximum(t * nb - 1, 0), 0)),
                  pl.BlockSpec((1, POOL_HALO, D), lambda b, t: (b, jnp.minimum((t + 1) * nb, last), 0)),
                  _const_spec(g.shape), _const_spec(wu.shape), _const_spec(wg.shape), row(MIX_A),
                  _layer_spec(wo_all.shape[1:], layer)],
        out_specs=row(D),
        out_shape=jax.ShapeDtypeStruct((B, T, D), F32),
        compiler_params=_params("parallel", "parallel"),
        name="even_back",
    )(x, x, x, g, wu, wg, a, wo_all)


def _odd_front_kernel(x_ref, g_ref, w_ref, c_ref, s_ref, q_ref, k_ref, v_ref):
    half = RET_DK // 2
    k_scale = RET_DK ** -0.5
    for rows in _sub_tiles(x_ref.shape[1]):
        h = _rms(x_ref[0, rows, :], g_ref[...]).astype(BF16)
        qkv = _dot(h, w_ref[...])
        c = c_ref[0, rows, :]
        s = s_ref[0, rows, :]
        for hd in range(RET_HEADS):
            for base, ref, scale in ((0, q_ref, None), (RET_QK, k_ref, k_scale)):
                lo = base + hd * RET_DK
                x1 = qkv[:, lo:lo + half]
                x2 = qkv[:, lo + half:lo + RET_DK]
                o1 = x1 * c - x2 * s
                o2 = x2 * c + x1 * s
                if scale is not None:
                    o1 = o1 * scale
                    o2 = o2 * scale
                ref[0, rows, hd * RET_DK:hd * RET_DK + half] = o1.astype(BF16)
                ref[0, rows, hd * RET_DK + half:(hd + 1) * RET_DK] = o2.astype(BF16)
        v_ref[0, rows, :] = qkv[:, 2 * RET_QK:].astype(BF16)


def _odd_front(x, g, w_all, layer, c, s):
    B, T, D = x.shape
    tm = min(T, ROW_TILE)
    row = lambda wd: pl.BlockSpec((1, tm, wd), lambda b, t: (b, t, 0))
    qk = jax.ShapeDtypeStruct((B, T, RET_QK), BF16)
    return pl.pallas_call(
        _odd_front_kernel,
        grid=(B, T // tm),
        in_specs=[row(D), _const_spec(g.shape), _layer_spec((D, 2 * RET_QK + RET_V), layer),
                  row(LANES), row(LANES)],
        out_specs=[row(RET_QK), row(RET_QK), row(RET_V)],
        out_shape=[qk, qk, jax.ShapeDtypeStruct((B, T, RET_V), BF16)],
        compiler_params=_params("parallel", "parallel"),
        name="odd_front",
    )(x, g, w_all, c, s)


def _log_sigmoid(x):
    return jnp.minimum(x, 0.0) - jnp.log1p(jnp.exp(-jnp.abs(x)))


def _retention_kernel(q_ref, k_ref, v_ref, df_ref, db_ref, gn_ref, o_ref, sf_ref, acc_ref):
    T = q_ref.shape[1]
    C = min(T, RET_CHUNK)
    n_chunks = T // C
    heads = range(q_ref.shape[2] // RET_DK)
    ri = lax.broadcasted_iota(jnp.int32, (C, 1), 0).astype(F32)
    diff = (lax.broadcasted_iota(jnp.int32, (C, C), 0)
            - lax.broadcasted_iota(jnp.int32, (C, C), 1)).astype(F32)

    def decays(hd):
        lf1 = _log_sigmoid(df_ref[hd])[:, :1]
        lb1 = _log_sigmoid(db_ref[hd])[:, :1]
        return dict(
            dmat=jnp.exp(jnp.where(diff >= 0, diff * lf1, -diff * lb1)),
            xi_f=jnp.exp((ri + 1.0) * lf1),
            xi_b=jnp.exp((C - ri) * lb1),
            zeta_f=jnp.exp((C - 1.0 - ri) * lf1),
            zeta_b=jnp.exp(ri * lb1),
            cd_f=jnp.exp(C * lf1), cd_b=jnp.exp(C * lb1))

    dec = [decays(hd) for hd in heads]

    def state_update(hd, i, zeta, cd, first):
        rows = slice(i * C, (i + 1) * C)
        kz = (k_ref[0, rows, hd * RET_DK:(hd + 1) * RET_DK].astype(F32) * zeta).astype(BF16)
        upd = _dot_tn(kz, v_ref[0, rows, hd * RET_DV:(hd + 1) * RET_DV])
        acc_ref[hd] = upd if first else acc_ref[hd] * cd + upd

    for i in range(n_chunks - 1):
        for hd in heads:
            state_update(hd, i, dec[hd]["zeta_f"], dec[hd]["cd_f"], first=(i == 0))
            sf_ref[hd, i + 1] = acc_ref[hd].astype(BF16)

    for i in reversed(range(n_chunks)):
        rows = slice(i * C, (i + 1) * C)
        for hd in heads:
            d = dec[hd]
            vcols = slice(hd * RET_DV, (hd + 1) * RET_DV)
            q = q_ref[0, rows, hd * RET_DK:(hd + 1) * RET_DK]
            qf = q.astype(F32)
            s = (_dot_nt(q, k_ref[0, rows, hd * RET_DK:(hd + 1) * RET_DK]) * d["dmat"]).astype(BF16)
            o = _dot(s, v_ref[0, rows, vcols])
            if i > 0:
                o = o + _dot((qf * d["xi_f"]).astype(BF16), sf_ref[hd, i])
            if i < n_chunks - 1:
                o = o + _dot((qf * d["xi_b"]).astype(BF16), acc_ref[hd].astype(BF16))
            mu = jnp.mean(o, axis=-1, keepdims=True)
            oc = o - mu
            var = jnp.mean(oc * oc, axis=-1, keepdims=True)
            o_ref[0, rows, vcols] = (oc * lax.rsqrt(var + NORM_EPS) * gn_ref[:, vcols]).astype(BF16)
            if i > 0:
                state_update(hd, i, d["zeta_b"], d["cd_b"], first=(i == n_chunks - 1))


def _retention(q, k, v, dec_f, dec_b, gn_g):
    B, T, _ = q.shape
    C = min(T, RET_CHUNK)
    hs = RET_HEADS_PER_STEP
    head = lambda w: pl.BlockSpec((1, T, hs * w), lambda b, h: (b, 0, h))
    dec = pl.BlockSpec((hs, 1, LANES), lambda b, h: (h, 0, 0))
    return pl.pallas_call(
        _retention_kernel,
        grid=(B, RET_HEADS // hs),
        in_specs=[head(RET_DK), head(RET_DK), head(RET_DV), dec, dec,
                  pl.BlockSpec((1, hs * RET_DV), lambda b, h: (0, h))],
        out_specs=head(RET_DV),
        out_shape=jax.ShapeDtypeStruct((B, T, RET_V), BF16),
        scratch_shapes=[pltpu.VMEM((hs, T // C, RET_DK, RET_DV), BF16),
                        pltpu.VMEM((hs, RET_DK, RET_DV), F32)],
        compiler_params=_params("parallel", "parallel"),
        name="retention",
    )(q, k, v, dec_f, dec_b, gn_g)


def _odd_back_kernel(x_ref, g_ref, wg_ref, a_ref, wo_ref, fg_ref, o_ref, *, final_norm):
    for rows in _sub_tiles(x_ref.shape[1]):
        x = x_ref[0, rows, :]
        h = _rms(x, g_ref[...]).astype(BF16)
        gate = _dot(h, wg_ref[...])
        y = (a_ref[0, rows, :].astype(F32) * _silu(gate)).astype(BF16)
        out = x + _dot(y, wo_ref[...])
        if final_norm:
            out = _rms(out, fg_ref[...])
        o_ref[0, rows, :] = out


def _odd_back(x, g, w_all, a, wo_all, layer, fg, final_norm):
    B, T, D = x.shape
    tm = min(T, ROW_TILE)
    row = lambda w: pl.BlockSpec((1, tm, w), lambda b, t: (b, t, 0))
    return pl.pallas_call(
        functools.partial(_odd_back_kernel, final_norm=final_norm),
        grid=(B, T // tm),
        in_specs=[row(D), _const_spec(g.shape),
                  _layer_spec((D, RET_V), layer, col_block=(2 * RET_QK + RET_V) // RET_V),
                  row(RET_V), _layer_spec(wo_all.shape[1:], layer), _const_spec(fg.shape)],
        out_specs=row(D),
        out_shape=jax.ShapeDtypeStruct((B, T, D), F32),
        compiler_params=_params("parallel", "parallel"),
        name="odd_back",
    )(x, g, w_all, a, wo_all, fg)


def _even_weights(w_in, w_uq, w_ukv):
    n_lat = MLA_Q_LORA + MLA_KV_LORA
    wlat = jnp.concatenate(
        [w_in[:, :n_lat + MLA_ROPE], jnp.zeros((D_MODEL, LANES - MLA_ROPE), w_in.dtype)], axis=1)
    wu = w_in[:, n_lat + MLA_ROPE:n_lat + MLA_ROPE + MIX_B]
    wg = w_in[:, n_lat + MLA_ROPE + MIX_B:]
    uq = w_uq.reshape(MLA_Q_LORA, MLA_HEADS, MLA_NOPE + MLA_ROPE)
    uq_nope = uq[:, :, :MLA_NOPE].reshape(MLA_Q_LORA, MIX_A)
    uq_rope_t = uq[:, :, MLA_NOPE:].reshape(MLA_Q_LORA, MLA_HEADS * MLA_ROPE).T
    ukv = w_ukv.reshape(MLA_KV_LORA, MLA_HEADS, MLA_NOPE + MLA_V)
    uk = ukv[:, :, :MLA_NOPE].reshape(MLA_KV_LORA, MIX_A)
    uvt = ukv[:, :, MLA_NOPE:].reshape(MLA_KV_LORA, MIX_A).T
    return (wu, uq_nope, uk) + tuple(w.astype(BF16) for w in (wlat, wg, uq_rope_t, uvt))


def kernel(x, positions, a_norm_g, a_w_in, a_q_norm_g, a_w_uq, a_kv_norm_g, a_w_ukv, a_pool_w,
           a_pool_scale, a_w_out, r_norm_g, r_w_in, r_decay_fwd, r_decay_bwd, r_gn_g, r_w_out,
           final_norm_g):
    depth = a_norm_g.shape[0] + r_norm_g.shape[0]
    assert depth % 2 == 0, "the final norm is fused into the last (odd) layer's back kernel"
    ca, sa, cat, sat, cr, sr = _rope_tables(positions)
    fg = final_norm_g[None]
    a_w_out, r_w_in, r_w_out = (w.astype(BF16) for w in (a_w_out, r_w_in, r_w_out))
    for layer in range(depth):
        i = layer // 2
        if layer % 2 == 0:
            wu, uq_nope, uk, wlat, wg, uq_rope_t, wuvt = _even_weights(a_w_in[i], a_w_uq[i], a_w_ukv[i])
            g = a_norm_g[i][None]
            wuqt = jnp.concatenate([_q_fold(uq_nope, uk), uq_rope_t], axis=0)
            qt, kl, vt = _even_front(x, g, wlat, a_q_norm_g[i][None], wuqt, a_kv_norm_g[i][None],
                                     wuvt, ca, sa, cat, sat)
            a = _attention(qt, kl, vt)
            wu = _pool_fold(wu, a_pool_w[i], a_pool_scale[i][None])
            x = _even_back(x, g, wu, wg, a, a_w_out, i)
        else:
            g = r_norm_g[i][None]
            q, k, v = _odd_front(x, g, r_w_in, i, cr, sr)
            dec_f = jnp.broadcast_to(r_decay_fwd[i][:, None, None], (RET_HEADS, 1, LANES))
            dec_b = jnp.broadcast_to(r_decay_bwd[i][:, None, None], (RET_HEADS, 1, LANES))
            o = _retention(q, k, v, dec_f, dec_b, r_gn_g[i][None])
            x = _odd_back(x, g, r_w_in, o, r_w_out, i, fg, final_norm=(layer == depth - 1))
    return x
```
